```python
import math
import jax, jax.numpy as jnp
from jax import lax
import numpy as np

D_MODEL = 1024
BATCH = 4
SEQ = 8192
DEPTH = 4

GRID_W = 64
CTX_LEN = 256
N_MIXERS = 2
A_HEADS = 4
A_DK = 128
A_DV = 256
N_DIR = 2
CHUNK = 64
D_FF = 2816
EPS = 1e-6
A_PROJ = 2 * A_HEADS * A_DK + A_HEADS * A_DV + D_MODEL + 2 * N_DIR * A_HEADS
N_A = sum(1 for i in range(DEPTH) if i % N_MIXERS == 0)
N_B = DEPTH - N_A

kernel_name = "hybrid_mlstm_shortconv_convglu_prefix_dit"


def rmsnorm(x, g):
    xf = x.astype(jnp.float32)
    y = xf * lax.rsqrt(jnp.mean(xf * xf, axis=-1, keepdims=True) + EPS)
    return (y * g.astype(jnp.float32)).astype(x.dtype)


def modulate(x, g, shift, scale):
    return rmsnorm(x, g) * (1 + scale) + shift


def conv3(x, w, axis):
    pad = [(0, 0)] * x.ndim
    pad[axis] = (1, 1)
    xp = jnp.pad(x, pad)
    n = x.shape[axis]
    sl = lambda s: lax.slice_in_dim(xp, s, s + n, axis=axis)
    return sl(0) * w[0] + sl(1) * w[1] + sl(2) * w[2]


def mlstm_inputs(h, w_in, b_gate):
    bn, t, _ = h.shape
    p = h @ w_in
    o1 = A_HEADS * A_DK
    o2 = 2 * o1
    o3 = o2 + A_HEADS * A_DV
    o4 = o3 + D_MODEL
    heads = lambda a, d: a.reshape(bn, t, A_HEADS, d).transpose(0, 2, 1, 3).astype(jnp.float32)
    q = heads(p[..., :o1], A_DK)
    k = heads(p[..., o1:o2], A_DK) * (A_DK ** -0.5)
    v = heads(p[..., o2:o3], A_DV)
    o = p[..., o3:o4]
    gates = (p[..., o4:] + b_gate).astype(jnp.float32)
    gates = gates.reshape(bn, t, 2, N_DIR, A_HEADS).transpose(2, 3, 0, 4, 1)
    it = gates[0]
    lf = jax.nn.log_sigmoid(gates[1])
    return q, k, v, it, lf, o


def mlstm_chunk_scan(q, k, v, it, lf, state, emit):
    bn, nh, t, _ = q.shape
    nc = t // CHUNK
    chunks = lambda a: jnp.moveaxis(a.reshape(a.shape[:2] + (nc, CHUNK) + a.shape[3:]), 2, 0)
    xs = (chunks(q), chunks(k), chunks(v), chunks(it), chunks(lf))
    mask = jnp.tril(jnp.ones((CHUNK, CHUNK), dtype=bool))

    def body(carry, inp):
        cm, nv, m = carry
        qc, kc, vc, ic, fc = inp
        b = jnp.cumsum(fc, axis=-1)
        g = b[..., -1]
        a = g[..., None] - b + ic
        m_new = jnp.maximum(g + m, jnp.max(a, axis=-1))
        decay = jnp.exp(g + m - m_new)
        wa = jnp.exp(a - m_new[..., None])
        c_new = decay[..., None, None] * cm + jnp.einsum('bhsv,bhsd->bhvd', wa[..., None] * vc, kc)
        n_new = decay[..., None] * nv + jnp.einsum('bhs,bhsd->bhd', wa, kc)
        if not emit:
            return (c_new, n_new, m_new), None
        dmat = jnp.where(mask, b[..., :, None] - b[..., None, :] + ic[..., None, :], -jnp.inf)
        inter = b + m[..., None]
        m_t = jnp.maximum(inter, jnp.max(dmat, axis=-1))
        w = jnp.exp(dmat - m_t[..., None]) * jnp.einsum('bhtd,bhsd->bhts', qc, kc)
        wi = jnp.exp(inter - m_t)
        num = jnp.einsum('bhts,bhsv->bhtv', w, vc) + wi[..., None] * jnp.einsum('bhvd,bhtd->bhtv', cm, qc)
        den = jnp.sum(w, axis=-1) + wi * jnp.einsum('bhd,bhtd->bht', nv, qc)
        h = num / jnp.maximum(jnp.abs(den), jnp.exp(-m_t))[..., None]
        return (c_new, n_new, m_new), h

    state, hs = lax.scan(body, state, xs)
    if not emit:
        return None, state
    return jnp.moveaxis(hs, 0, 2).reshape(bn, nh, t, A_DV), state


def mlstm_readout(h, o, gain, w_out):
    bn, nh, t, dv = h.shape
    hn = h * lax.rsqrt(jnp.mean(h * h, axis=-1, keepdims=True) + EPS)
    hn = hn.transpose(0, 2, 1, 3).reshape(bn, t, nh * dv) * gain.astype(jnp.float32)
    y = hn * jax.nn.sigmoid(o.astype(jnp.float32))
    return y.astype(o.dtype) @ w_out


def mlstm_mixer(h_lat, h_ctx, w_in, b_gate, head_gain, w_out, emit_ctx):
    ql, kl, vl, il, fl, ol = mlstm_inputs(h_lat, w_in, b_gate)
    qc, kc, vc, ic, fc, oc = mlstm_inputs(h_ctx, w_in, b_gate)
    bn = h_lat.shape[0]
    s0 = (jnp.zeros((bn, A_HEADS, A_DV, A_DK), jnp.float32),
          jnp.zeros((bn, A_HEADS, A_DK), jnp.float32),
          jnp.zeros((bn, A_HEADS), jnp.float32))
    rq = lambda a: jnp.flip(a, axis=2)
    rg = lambda a: jnp.flip(a, axis=-1)
    hc_f, s_f = mlstm_chunk_scan(qc, kc, vc, ic[0], fc[0], s0, emit_ctx)
    hl_f, _ = mlstm_chunk_scan(ql, kl, vl, il[0], fl[0], s_f, True)
    hc_b, s_b = mlstm_chunk_scan(rq(qc), rq(kc), rq(vc), rg(ic[1]), rg(fc[1]), s0, emit_ctx)
    hl_b, _ = mlstm_chunk_scan(rq(ql), rq(kl), rq(vl), rg(il[1]), rg(fl[1]), s_b, True)
    y = mlstm_readout(hl_f + rq(hl_b), ol, head_gain, w_out)
    yc = mlstm_readout(hc_f + rq(hc_b), oc, head_gain, w_out) if emit_ctx else None
    return y, yc


def shortconv_mixer(h, w_in, w_conv, w_out, conv_fn):
    bg, cg, xv = jnp.split(h @ w_in, 3, axis=-1)
    return (bg * conv_fn(cg * xv, w_conv)) @ w_out


def conv_glu(h, w_up, w_conv, b_conv, w_down, conv_fn):
    gate, val = jnp.split(h @ w_up, 2, axis=-1)
    gate = conv_fn(gate, w_conv) + b_conv
    return (jax.nn.silu(gate) * val) @ w_down


def setup_inputs(seed: int = 0) -> dict:
    key = jax.random.key(seed)
    ks = jax.random.split(key, 24)
    nrm = lambda k, s, sc: jax.random.normal(k, s, jnp.float32) * sc
    D = D_MODEL
    b_i = nrm(ks[10], (N_A, N_DIR, A_HEADS), 0.1)
    b_f = jnp.linspace(3.0, 6.0, A_HEADS, dtype=jnp.float32) + nrm(ks[11], (N_A, N_DIR, A_HEADS), 0.1)
    a_b_gate = jnp.stack([b_i, b_f], axis=1).reshape(N_A, 2 * N_DIR * A_HEADS)
    return {
        "x": nrm(ks[0], (BATCH, SEQ, D), 1.0),
        "c": nrm(ks[1], (BATCH, D), 1.0),
        "ctx": nrm(ks[2], (BATCH, CTX_LEN, D), 1.0),
        "c_ctx": nrm(ks[3], (D,), 1.0),
        "w_mod": nrm(ks[4], (DEPTH, D, 6 * D), 0.5 * D ** -0.5),
        "b_mod": nrm(ks[5], (DEPTH, 6 * D), 0.02),
        "g_mix": 1.0 + nrm(ks[6], (DEPTH, D), 0.05),
        "g_ffn": 1.0 + nrm(ks[7], (DEPTH, D), 0.05),
        "a_w_in": nrm(ks[8], (N_A, D, A_PROJ), D ** -0.5),
        "a_b_gate": a_b_gate,
        "a_head_gain": 1.0 + nrm(ks[9], (N_A, A_HEADS * A_DV), 0.05),
        "a_w_out": nrm(ks[12], (N_A, A_HEADS * A_DV, D), (A_HEADS * A_DV) ** -0.5),
        "b_w_in": nrm(ks[13], (N_B, D, 3 * D), D ** -0.5),
        "b_w_conv": nrm(ks[14], (N_B, 3, D), 3 ** -0.5),
        "b_w_out": nrm(ks[15], (N_B, D, D), D ** -0.5),
        "f_w_up": nrm(ks[16], (DEPTH, D, 2 * D_FF), D ** -0.5),
        "f_w_conv": nrm(ks[17], (DEPTH, 3, D_FF), 3 ** -0.5),
        "f_b_conv": nrm(ks[18], (DEPTH, D_FF), 0.02),
        "f_w_down": nrm(ks[19], (DEPTH, D_FF, D), D_FF ** -0.5),
        "g_final": 1.0 + nrm(ks[20], (D,), 0.05),
    }


def reference(x, c, ctx, c_ctx, w_mod, b_mod, g_mix, g_ffn, a_w_in, a_b_gate, a_head_gain,
              a_w_out, b_w_in, b_w_conv, b_w_out, f_w_up, f_w_conv, f_b_conv, f_w_down, g_final):
    bn, t, d = x.shape
    rows = t // GRID_W
    lat_row_conv = lambda a, w: conv3(a.reshape(bn, rows, GRID_W, a.shape[-1]), w, 2).reshape(a.shape)
    lat_col_conv = lambda a, w: conv3(a.reshape(bn, rows, GRID_W, a.shape[-1]), w, 1).reshape(a.shape)
    seq_conv = lambda a, w: conv3(a, w, 1)
    rec_layers = [i for i in range(DEPTH) if i % N_MIXERS == 0]
    last_rec = max(rec_layers) if rec_layers else -1

    for i in range(DEPTH):
        j = i // N_MIXERS
        ctx_read = i <= last_rec
        ctx_live = i < last_rec
        mod = jax.nn.silu(c) @ w_mod[i] + b_mod[i]
        sh1, sc1, ga1, sh2, sc2, ga2 = [m[:, None, :] for m in jnp.split(mod, 6, axis=-1)]
        if ctx_read:
            mod_c = jax.nn.silu(c_ctx) @ w_mod[i] + b_mod[i]
            csh1, csc1, cga1, csh2, csc2, cga2 = jnp.split(mod_c, 6, axis=-1)
            hc = modulate(ctx, g_mix[i], csh1, csc1)
        hx = modulate(x, g_mix[i], sh1, sc1)
        if i % N_MIXERS == 0:
            y, yc = mlstm_mixer(hx, hc, a_w_in[j], a_b_gate[j], a_head_gain[j], a_w_out[j], ctx_live)
        else:
            y = shortconv_mixer(hx, b_w_in[j], b_w_conv[j], b_w_out[j], lat_row_conv)
            yc = shortconv_mixer(hc, b_w_in[j], b_w_conv[j], b_w_out[j], seq_conv) if ctx_live else None
        x = x + ga1 * y
        hx = modulate(x, g_ffn[i], sh2, sc2)
        x = x + ga2 * conv_glu(hx, f_w_up[i], f_w_conv[i], f_b_conv[i], f_w_down[i], lat_col_conv)
        if ctx_live:
            ctx = ctx + cga1 * yc
            hc = modulate(ctx, g_ffn[i], csh2, csc2)
            ctx = ctx + cga2 * conv_glu(hc, f_w_up[i], f_w_conv[i], f_b_conv[i], f_w_down[i], seq_conv)

    return rmsnorm(x, g_final)
```

```python
import functools

import jax
import jax.numpy as jnp
from jax import lax
from jax.experimental import pallas as pl
from jax.experimental.pallas import tpu as pltpu

F32 = jnp.float32
BF16 = jnp.bfloat16

D_MODEL = 1024
GRID_W = 64
N_HEADS = 4
DK = 128
DV = 256
D_FF = 2816
EPS = 1e-6
N_GATE = 16
GATE_PAD = 128

SCAN_CHUNK = 128
FF_CHUNK = 256
TOKEN_TILE = 512
VMEM_LIMIT = 56 * 1024 * 1024


def _params(n_axes):
    return pltpu.CompilerParams(
        dimension_semantics=("arbitrary",) * n_axes,
        vmem_limit_bytes=VMEM_LIMIT)


def _const_spec(shape):
    zeros = (0,) * len(shape)
    return pl.BlockSpec(shape, lambda *_: zeros, pipeline_mode=pl.Buffered(1))


def _dot(a, b):
    return jnp.dot(a, b, preferred_element_type=F32)


def _sigmoid(v):
    return 1.0 / (1.0 + jnp.exp(-v))


def _rmsnorm(x, g):
    ms = jnp.mean(x * x, axis=-1, keepdims=True)
    return (x * lax.rsqrt(ms + EPS)) * g


def _modulate(x, g, shift, scale):
    return _rmsnorm(x, g) * (1.0 + scale) + shift


def _shift_conv(u, w_ref, seg):
    n = u.shape[0]
    pos = lax.broadcasted_iota(jnp.int32, (n, 1), 0) & (seg - 1)
    prev = jnp.where(pos >= 1, pltpu.roll(u, 1, axis=0), 0.0)
    nxt = jnp.where(pos < seg - 1, pltpu.roll(u, n - 1, axis=0), 0.0)
    return prev * w_ref[0:1, :] + u * w_ref[1:2, :] + nxt * w_ref[2:3, :]


def _mod_kernel(cc_ref, w_ref, b_ref, o_ref):
    cc = cc_ref[...]
    s = cc * _sigmoid(cc)
    o_ref[...] = _dot(s, w_ref[...]) + b_ref[...]


def _mod_all(cc, w_mod, b_mod):
    depth, d, n = w_mod.shape
    tn = 1536
    return pl.pallas_call(
        _mod_kernel,
        grid=(depth, n // tn),
        in_specs=[
            pl.BlockSpec((8, d), lambda l, j: (0, 0)),
            pl.BlockSpec((None, d, tn), lambda l, j: (l, 0, j)),
            pl.BlockSpec((None, 1, tn), lambda l, j: (l, 0, j)),
        ],
        out_specs=pl.BlockSpec((None, 8, tn), lambda l, j: (l, 0, j)),
        out_shape=jax.ShapeDtypeStruct((depth, 8, n), F32),
        compiler_params=_params(2),
        name="adaln_mod",
    )(cc, w_mod, b_mod.reshape(depth, 1, n))


def _shortconv_kernel(x_ref, mod_ref, g_ref, win_ref, wconv_ref, wout_ref,
                      o_ref, z_ref, *, seg):
    d = D_MODEL
    x = x_ref[...]
    hx = _modulate(x, g_ref[...], mod_ref[0:1, :], mod_ref[1:2, :]).astype(BF16)
    cw = 256
    for j in range(d // cw):
        lo, hi = j * cw, (j + 1) * cw
        bg = _dot(hx, win_ref[:, lo:hi])
        cg = _dot(hx, win_ref[:, d + lo:d + hi])
        xv = _dot(hx, win_ref[:, 2 * d + lo:2 * d + hi])
        cv = _shift_conv(cg * xv, wconv_ref.at[:, lo:hi], seg)
        z_ref[:, lo:hi] = (bg * cv).astype(BF16)
    y = _dot(z_ref[...], wout_ref[...])
    o_ref[...] = x + mod_ref[2:3, :] * y


def _shortconv(x2d, mod_l, g, w_in, w_conv, w_out, *, tm, seg, row_fn):
    n, d = x2d.shape
    return pl.pallas_call(
        functools.partial(_shortconv_kernel, seg=seg),
        grid=(n // tm,),
        in_specs=[
            pl.BlockSpec((tm, d), lambda i: (i, 0)),
            pl.BlockSpec((None, 6, d), lambda i: (row_fn(i), 0, 0)),
            _const_spec((1, d)),
            _const_spec((d, 3 * d)),
            _const_spec((3, d)),
            _const_spec((d, d)),
        ],
        out_specs=pl.BlockSpec((tm, d), lambda i: (i, 0)),
        out_shape=jax.ShapeDtypeStruct((n, d), F32),
        scratch_shapes=[pltpu.VMEM((tm, d), BF16)],
        compiler_params=_params(1),
        name="shortconv_mixer",
    )(x2d, mod_l, g, w_in, w_conv, w_out)


def _ffn_kernel(*refs, halo, seg, tiles_per_seg, final):
    if halo:
        xp_ref, x_ref, xn_ref = refs[:3]
        refs = refs[3:]
    else:
        x_ref = refs[0]
        refs = refs[1:]
    (mod_ref, g_ref, wup_ref, wconv_ref, bconv_ref, wdown_ref, gfin_ref,
     o_ref, act_ref) = refs
    x = x_ref[...]
    tm = x.shape[0]
    g = g_ref[...]
    shift, scale = mod_ref[3:4, :], mod_ref[4:5, :]
    hx = _modulate(x, g, shift, scale).astype(BF16)
    if halo:
        j = pl.program_id(0) % tiles_per_seg
        hp = _modulate(xp_ref[...], g, shift, scale).astype(BF16)
        hn = _modulate(xn_ref[...], g, shift, scale).astype(BF16)
        hext = jnp.concatenate([hp, hx, hn], axis=0)
        row = lax.broadcasted_iota(jnp.int32, (tm, 1), 0)
        keep_prev = jnp.logical_or(j > 0, row >= halo)
        keep_next = jnp.logical_or(j < tiles_per_seg - 1, row < tm - halo)
    for c in range(D_FF // FF_CHUNK):
        lo, hi = c * FF_CHUNK, (c + 1) * FF_CHUNK
        val = _dot(hx, wup_ref[:, D_FF + lo:D_FF + hi])
        if halo:
            ge = _dot(hext, wup_ref[:, lo:hi])
            gp = jnp.where(keep_prev, ge[0:tm], 0.0)
            gm = ge[halo:halo + tm]
            gn = jnp.where(keep_next, ge[2 * halo:2 * halo + tm], 0.0)
            gate = (gp * wconv_ref[0:1, lo:hi] + gm * wconv_ref[1:2, lo:hi]
                    + gn * wconv_ref[2:3, lo:hi])
        else:
            gate = _shift_conv(_dot(hx, wup_ref[:, lo:hi]),
                               wconv_ref.at[:, lo:hi], seg)
        gate = gate + bconv_ref[:, lo:hi]
        act_ref[:, lo:hi] = (gate * _sigmoid(gate) * val).astype(BF16)
    y = _dot(act_ref[...], wdown_ref[...])
    out = x + mod_ref[5:6, :] * y
    if final:
        out = _rmsnorm(out, gfin_ref[...])
    o_ref[...] = out


def _ffn(x2d, mod_l, g, w_up, w_conv, b_conv, w_down, g_final, *, tm, halo,
         seg, row_fn, final):
    n, d = x2d.shape
    tiles_per_seg = seg // tm
    x_specs = [pl.BlockSpec((tm, d), lambda i: (i, 0))]
    x_args = [x2d]
    if halo:
        r = tm // halo
        last = n // halo - 1
        x_specs = [
            pl.BlockSpec((halo, d), lambda i: (jnp.maximum(i * r - 1, 0), 0)),
            x_specs[0],
            pl.BlockSpec((halo, d), lambda i: (jnp.minimum((i + 1) * r, last), 0)),
        ]
        x_args = [x2d, x2d, x2d]
    return pl.pallas_call(
        functools.partial(_ffn_kernel, halo=halo, seg=seg,
                          tiles_per_seg=tiles_per_seg, final=final),
        grid=(n // tm,),
        in_specs=x_specs + [
            pl.BlockSpec((None, 6, d), lambda i: (row_fn(i), 0, 0)),
            _const_spec((1, d)),
            _const_spec((d, 2 * D_FF)),
            _const_spec((3, D_FF)),
            _const_spec((1, D_FF)),
            _const_spec((D_FF, d)),
            _const_spec((1, d)),
        ],
        out_specs=pl.BlockSpec((tm, d), lambda i: (i, 0)),
        out_shape=jax.ShapeDtypeStruct((n, d), F32),
        scratch_shapes=[pltpu.VMEM((tm, D_FF), BF16)],
        compiler_params=_params(1),
        name="convglu_ffn",
    )(*x_args, mod_l, g, w_up, w_conv, b_conv, w_down, g_final)


def _log_sigmoid(v):
    return -(jnp.maximum(-v, 0.0) + jnp.log(1.0 + jnp.exp(-jnp.abs(v))))


def _chunk_cumsums(v, axis, chunk):
    n = v.shape[axis]
    shape = (n, 1) if axis == 0 else (1, n)
    pos = lax.broadcasted_iota(jnp.int32, shape, axis) & (chunk - 1)
    pre, suf = v, v
    step = 1
    while step < chunk:
        pre = pre + jnp.where(pos >= step, pltpu.roll(pre, step, axis=axis), 0.0)
        suf = suf + jnp.where(pos < chunk - step,
                              pltpu.roll(suf, n - step, axis=axis), 0.0)
        step *= 2
    return pre, suf


def _mlstm_proj_kernel(x_ref, mod_ref, g_ref, wqkv_ref, wkt_ref, wg_ref,
                       wgt_ref, bg_ref, bgt_ref,
                       qkv_ref, kt_ref, gc_ref, gr_ref):
    nq = N_HEADS * DK
    x = x_ref[...]
    hx = _modulate(x, g_ref[...], mod_ref[0:1, :], mod_ref[1:2, :]).astype(BF16)
    k_scale = DK ** -0.5
    qkv_ref[:, 0:nq] = _dot(hx, wqkv_ref[:, 0:nq]).astype(BF16)
    qkv_ref[:, nq:2 * nq] = (_dot(hx, wqkv_ref[:, nq:2 * nq]) * k_scale).astype(BF16)
    qkv_ref[:, 2 * nq:] = _dot(hx, wqkv_ref[:, 2 * nq:]).astype(BF16)
    nt = (((1,), (1,)), ((), ()))
    kt = lax.dot_general(wkt_ref[...], hx, nt, preferred_element_type=F32)
    kt_ref[...] = (kt * k_scale).astype(BF16)

    half = N_GATE // 2
    gc = _dot(hx, wg_ref[...]) + bg_ref[...]
    pre, suf = _chunk_cumsums(_log_sigmoid(gc), 0, SCAN_CHUNK)
    lane = lax.broadcasted_iota(jnp.int32, (1, GATE_PAD), 1)
    gc_ref[...] = jnp.where(lane < half, gc,
                            jnp.where(lane < half + N_HEADS, pre, suf))
    gr = lax.dot_general(wgt_ref[...], hx, nt, preferred_element_type=F32)
    gr = gr + bgt_ref[...]
    pre, suf = _chunk_cumsums(_log_sigmoid(gr), 1, SCAN_CHUNK)
    sub = lax.broadcasted_iota(jnp.int32, (N_GATE, 1), 0)
    gr_ref[...] = jnp.where(sub < half, gr,
                            jnp.where(sub < half + N_HEADS, pre, suf))


def _mlstm_proj(x2d, mod_l, g, wts, *, tm, row_fn):
    n, d = x2d.shape
    nq = N_HEADS * DK
    nqkv = 2 * nq + N_HEADS * DV
    return pl.pallas_call(
        _mlstm_proj_kernel,
        grid=(n // tm,),
        in_specs=[
            pl.BlockSpec((tm, d), lambda i: (i, 0)),
            pl.BlockSpec((None, 6, d), lambda i: (row_fn(i), 0, 0)),
            _const_spec((1, d)),
            _const_spec((d, nqkv)),
            _const_spec((nq, d)),
            _const_spec((d, GATE_PAD)),
            _const_spec((N_GATE, d)),
            _const_spec((1, GATE_PAD)),
            _const_spec((N_GATE, 1)),
        ],
        out_specs=[
            pl.BlockSpec((tm, nqkv), lambda i: (i, 0)),
            pl.BlockSpec((nq, tm), lambda i: (0, i)),
            pl.BlockSpec((tm, GATE_PAD), lambda i: (i, 0)),
            pl.BlockSpec((N_GATE, tm), lambda i: (0, i)),
        ],
        out_shape=[
            jax.ShapeDtypeStruct((n, nqkv), BF16),
            jax.ShapeDtypeStruct((nq, n), BF16),
            jax.ShapeDtypeStruct((n, GATE_PAD), F32),
            jax.ShapeDtypeStruct((N_GATE, n), F32),
        ],
        compiler_params=_params(1),
        name="mlstm_proj",
    )(x2d, mod_l, g, wts["w_qkv"], wts["w_kt"], wts["w_g"], wts["w_gt"],
      wts["b_g"], wts["b_gt"])


def _scan_kernel(qkv_ref, kt_ref, gc_ref, gr_ref, c0_ref, n0_ref, m0_ref,
                 h_ref, cf_ref, nf_ref, mf_ref, c_s, n_s, m_s, *, rev):
    t = pl.program_id(1)
    n_t = pl.num_programs(1)
    L = SCAN_CHUNK
    nq = N_HEADS * DK
    ts = h_ref.shape[0]
    n_chunks = ts // L
    half = N_GATE // 2

    @pl.when(t == 0)
    def _():
        c_s[...] = c0_ref[...]
        n_s[...] = n0_ref[...]
        m_s[...] = m0_ref[...]

    ti = lax.broadcasted_iota(jnp.int32, (L, L), 0)
    si = lax.broadcasted_iota(jnp.int32, (L, L), 1)
    mask = (si >= ti) if rev else (si <= ti)
    gcol = N_HEADS if rev else 0

    order = range(n_chunks - 1, -1, -1) if rev else range(n_chunks)
    for c in order:
        r0, r1 = c * L, (c + 1) * L
        for h in range(N_HEADS):
            q = qkv_ref[r0:r1, h * DK:(h + 1) * DK]
            k = qkv_ref[r0:r1, nq + h * DK:nq + (h + 1) * DK]
            v = qkv_ref[r0:r1, 2 * nq + h * DV:2 * nq + (h + 1) * DV]
            kt = kt_ref[h * DK:(h + 1) * DK, r0:r1]
            gi = gcol + h
            irow = gr_ref[gi:gi + 1, r0:r1]
            brow = gr_ref[half + gi:half + gi + 1, r0:r1]
            icol = gc_ref[r0:r1, gi:gi + 1]
            bcol = gc_ref[r0:r1, half + gi:half + gi + 1]
            g = brow[:, 0:1] if rev else brow[:, L - 1:L]
            m = m_s[h][:, 0:1]
            arow = g - brow + irow
            m_new = jnp.maximum(g + m, jnp.max(arow, axis=1, keepdims=True))
            decay = jnp.exp(g + m - m_new)
            wa_row = jnp.exp(arow - m_new)
            wa_col = jnp.exp(g - bcol + icol - m_new)

            dmat = jnp.where(mask, bcol - brow + irow, -jnp.inf)
            inter = bcol + m
            m_t = jnp.maximum(inter, jnp.max(dmat, axis=1, keepdims=True))
            s = _dot(q, kt)
            w = jnp.exp(dmat - m_t) * s
            wi = jnp.exp(inter - m_t)
            ct = c_s[h]
            nrow = n_s[h]
            num = _dot(w.astype(BF16), v) + wi * _dot(q, ct.astype(BF16))
            qn = jnp.sum(q.astype(F32) * nrow, axis=1, keepdims=True)
            den = jnp.sum(w, axis=1, keepdims=True) + wi * qn
            hh = num / jnp.maximum(jnp.abs(den), jnp.exp(-m_t))
            h_ref[r0:r1, h * DV:(h + 1) * DV] = hh

            ktw = (kt.astype(F32) * wa_row).astype(BF16)
            c_s[h] = decay * ct + _dot(ktw, v)
            n_s[h] = decay * nrow + jnp.sum(wa_col * k.astype(F32), axis=0,
                                            keepdims=True)
            m_s[h] = jnp.broadcast_to(m_new, (1, DK))

    @pl.when(t == n_t - 1)
    def _():
        cf_ref[...] = c_s[...]
        nf_ref[...] = n_s[...]
        mf_ref[...] = m_s[...]


def _scan(qkv, kt, gc, gr, state, *, batch, ts, rev):
    n = qkv.shape[0]
    n_t = n // batch // ts
    nq = N_HEADS * DK
    nqkv = 2 * nq + N_HEADS * DV

    def blk(b, t):
        return b * n_t + ((n_t - 1 - t) if rev else t)

    st_shapes = [(N_HEADS, DK, DV), (N_HEADS, 1, DK), (N_HEADS, 1, DK)]
    st_specs = [pl.BlockSpec((None,) + s, lambda b, t: (b, 0, 0, 0))
                for s in st_shapes]
    h, cf, nf, mf = pl.pallas_call(
        functools.partial(_scan_kernel, rev=rev),
        grid=(batch, n_t),
        in_specs=[
            pl.BlockSpec((ts, nqkv), lambda b, t: (blk(b, t), 0)),
            pl.BlockSpec((nq, ts), lambda b, t: (0, blk(b, t))),
            pl.BlockSpec((ts, GATE_PAD), lambda b, t: (blk(b, t), 0)),
            pl.BlockSpec((N_GATE, ts), lambda b, t: (0, blk(b, t))),
        ] + st_specs,
        out_specs=[pl.BlockSpec((ts, N_HEADS * DV), lambda b, t: (blk(b, t), 0))]
        + st_specs,
        out_shape=[jax.ShapeDtypeStruct((n, N_HEADS * DV), F32)]
        + [jax.ShapeDtypeStruct((batch,) + s, F32) for s in st_shapes],
        scratch_shapes=[pltpu.VMEM(s, F32) for s in st_shapes],
        compiler_params=_params(2),
        name="mlstm_scan_bwd" if rev else "mlstm_scan_fwd",
    )(qkv, kt, gc, gr, *state)
    return h, (cf, nf, mf)


def _readout_kernel(x_ref, hf_ref, hb_ref, mod_ref, g_ref, wo_ref, gain_ref,
                    wout_ref, o_ref, z_ref):
    x = x_ref[...]
    hx = _modulate(x, g_ref[...], mod_ref[0:1, :], mod_ref[1:2, :]).astype(BF16)
    for h in range(N_HEADS):
        lo, hi = h * DV, (h + 1) * DV
        hh = hf_ref[:, lo:hi] + hb_ref[:, lo:hi]
        ms = jnp.mean(hh * hh, axis=-1, keepdims=True)
        hn = hh * lax.rsqrt(ms + EPS) * gain_ref[:, lo:hi]
        o = _dot(hx, wo_ref[:, lo:hi])
        z_ref[:, lo:hi] = (hn * _sigmoid(o)).astype(BF16)
    y = _dot(z_ref[...], wout_ref[...])
    o_ref[...] = x + mod_ref[2:3, :] * y


def _readout(x2d, hf, hb, mod_l, g, w_o, gain, w_out, *, tm, row_fn):
    n, d = x2d.shape
    nv = N_HEADS * DV
    return pl.pallas_call(
        _readout_kernel,
        grid=(n // tm,),
        in_specs=[
            pl.BlockSpec((tm, d), lambda i: (i, 0)),
            pl.BlockSpec((tm, nv), lambda i: (i, 0)),
            pl.BlockSpec((tm, nv), lambda i: (i, 0)),
            pl.BlockSpec((None, 6, d), lambda i: (row_fn(i), 0, 0)),
            _const_spec((1, d)),
            _const_spec((d, nv)),
            _const_spec((1, nv)),
            _const_spec((nv, d)),
        ],
        out_specs=pl.BlockSpec((tm, d), lambda i: (i, 0)),
        out_shape=jax.ShapeDtypeStruct((n, d), F32),
        scratch_shapes=[pltpu.VMEM((tm, nv), BF16)],
        compiler_params=_params(1),
        name="mlstm_readout",
    )(x2d, hf, hb, mod_l, g, w_o, gain, w_out)


def _mlstm_weights(w_in, b_gate):
    nq = N_HEADS * DK
    o2 = 2 * nq
    o3 = o2 + N_HEADS * DV
    o4 = o3 + D_MODEL
    w_g = w_in[:, o4:]
    return {
        "w_qkv": w_in[:, :o3].astype(BF16),
        "w_kt": w_in[:, nq:o2].T.astype(BF16),
        "w_o": w_in[:, o3:o4].astype(BF16),
        "w_g": jnp.pad(w_g, ((0, 0), (0, GATE_PAD - N_GATE))).astype(BF16),
        "w_gt": w_g.T.astype(BF16),
        "b_g": jnp.pad(b_gate, (0, GATE_PAD - N_GATE)).reshape(1, GATE_PAD),
        "b_gt": b_gate.reshape(N_GATE, 1),
    }


def kernel(x, c, ctx, c_ctx, w_mod, b_mod, g_mix, g_ffn, a_w_in, a_b_gate,
           a_head_gain, a_w_out, b_w_in, b_w_conv, b_w_out, f_w_up, f_w_conv,
           f_b_conv, f_w_down, g_final):
    bn, t, d = x.shape
    n_ctx = ctx.shape[1]
    depth = w_mod.shape[0]
    n_mixers = 2
    assert d == D_MODEL and bn <= 7 and t % TOKEN_TILE == 0
    assert n_ctx % SCAN_CHUNK == 0 and TOKEN_TILE % SCAN_CHUNK == 0

    rec_layers = [i for i in range(depth) if i % n_mixers == 0]
    last_rec = max(rec_layers) if rec_layers else -1

    cc = jnp.zeros((8, d), F32).at[:bn].set(c).at[bn].set(c_ctx)
    mod = _mod_all(cc, w_mod, b_mod).reshape(depth, 8, 6, d)

    tm = TOKEN_TILE
    tiles_per_batch = t // tm
    lat_row = lambda i: i // tiles_per_batch
    ctx_row = lambda i: bn

    xs = x.reshape(bn * t, d)
    cs = ctx.reshape(bn * n_ctx, d)
    zero_state = (jnp.zeros((bn, N_HEADS, DK, DV), F32),
                  jnp.zeros((bn, N_HEADS, 1, DK), F32),
                  jnp.zeros((bn, N_HEADS, 1, DK), F32))

    for i in range(depth):
        j = i // n_mixers
        ctx_read = i <= last_rec
        ctx_live = i < last_rec
        gm = g_mix[i].reshape(1, d)
        gf = g_ffn[i].reshape(1, d)
        if i % n_mixers == 0:
            wts = _mlstm_weights(a_w_in[j], a_b_gate[j])
            w_out = a_w_out[j].astype(BF16)
            gain = a_head_gain[j].reshape(1, N_HEADS * DV)
            s_f = s_b = zero_state
            if ctx_read:
                pc = _mlstm_proj(cs, mod[i], gm, wts, tm=n_ctx, row_fn=ctx_row)
                hc_f, s_f = _scan(*pc, zero_state, batch=bn, ts=n_ctx, rev=False)
                hc_b, s_b = _scan(*pc, zero_state, batch=bn, ts=n_ctx, rev=True)
            px = _mlstm_proj(xs, mod[i], gm, wts, tm=tm, row_fn=lat_row)
            hl_f, _ = _scan(*px, s_f, batch=bn, ts=tm, rev=False)
            hl_b, _ = _scan(*px, s_b, batch=bn, ts=tm, rev=True)
            xs = _readout(xs, hl_f, hl_b, mod[i], gm, wts["w_o"], gain, w_out,
                          tm=tm, row_fn=lat_row)
            if ctx_live:
                cs = _readout(cs, hc_f, hc_b, mod[i], gm, wts["w_o"], gain,
                              w_out, tm=n_ctx, row_fn=ctx_row)
        else:
            w_in = b_w_in[j].astype(BF16)
            w_out = b_w_out[j].astype(BF16)
            xs = _shortconv(xs, mod[i], gm, w_in, b_w_conv[j], w_out,
                            tm=tm, seg=GRID_W, row_fn=lat_row)
            if ctx_live:
                cs = _shortconv(cs, mod[i], gm, w_in, b_w_conv[j], w_out,
                                tm=n_ctx, seg=n_ctx, row_fn=ctx_row)
        w_up = f_w_up[i].astype(BF16)
        w_down = f_w_down[i].astype(BF16)
        b_conv = f_b_conv[i].reshape(1, D_FF)
        gfin = g_final.reshape(1, d)
        xs = _ffn(xs, mod[i], gf, w_up, f_w_conv[i], b_conv, w_down, gfin,
                  tm=tm, halo=GRID_W, seg=t, row_fn=lat_row,
                  final=(i == depth - 1))
        if ctx_live:
            cs = _ffn(cs, mod[i], gf, w_up, f_w_conv[i], b_conv, w_down, gfin,
                      tm=n_ctx, halo=0, seg=n_ctx, row_fn=ctx_row, final=False)

    return xs.reshape(bn, t, d)
```

```python
import functools

import jax
import jax.numpy as jnp
from jax import lax
from jax.experimental import pallas as pl
from jax.experimental.pallas import tpu as pltpu

F32 = jnp.float32
BF16 = jnp.bfloat16

D_MODEL = 1024
GRID_W = 64
N_HEADS = 4
DK = 128
DV = 256
D_FF = 2816
EPS = 1e-6
N_GATE = 16
GATE_PAD = 128

SCAN_CHUNK = 128
FF_CHUNK = 256
TOKEN_TILE = 512
VMEM_LIMIT = 56 * 1024 * 1024


def _params(n_axes):
    return pltpu.CompilerParams(
        dimension_semantics=("arbitrary",) * n_axes,
        vmem_limit_bytes=VMEM_LIMIT)


def _const_spec(shape):
    zeros = (0,) * len(shape)
    return pl.BlockSpec(shape, lambda *_: zeros, pipeline_mode=pl.Buffered(1))


def _dot(a, b):
    return jnp.dot(a, b, preferred_element_type=F32)


def _sigmoid(v):
    return 1.0 / (1.0 + jnp.exp(-v))


def _rmsnorm(x, g):
    ms = jnp.mean(x * x, axis=-1, keepdims=True)
    return (x * lax.rsqrt(ms + EPS)) * g


def _modulate(x, g, shift, scale):
    return _rmsnorm(x, g) * (1.0 + scale) + shift


def _shift_conv(u, w_ref, seg):
    n = u.shape[0]
    pos = lax.broadcasted_iota(jnp.int32, (n, 1), 0) & (seg - 1)
    prev = jnp.where(pos >= 1, pltpu.roll(u, 1, axis=0), 0.0)
    nxt = jnp.where(pos < seg - 1, pltpu.roll(u, n - 1, axis=0), 0.0)
    return prev * w_ref[0:1, :] + u * w_ref[1:2, :] + nxt * w_ref[2:3, :]


def _mod_kernel(cc_ref, w_ref, b_ref, o_ref):
    cc = cc_ref[...]
    s = cc * _sigmoid(cc)
    o_ref[...] = _dot(s, w_ref[...]) + b_ref[...]


def _mod_all(cc, w_mod, b_mod):
    depth, d, n = w_mod.shape
    tn = 1536
    return pl.pallas_call(
        _mod_kernel,
        grid=(depth, n // tn),
        in_specs=[
            pl.BlockSpec((8, d), lambda l, j: (0, 0)),
            pl.BlockSpec((None, d, tn), lambda l, j: (l, 0, j)),
            pl.BlockSpec((None, 1, tn), lambda l, j: (l, 0, j)),
        ],
        out_specs=pl.BlockSpec((None, 8, tn), lambda l, j: (l, 0, j)),
        out_shape=jax.ShapeDtypeStruct((depth, 8, n), F32),
        compiler_params=_params(2),
        name="adaln_mod",
    )(cc, w_mod, b_mod.reshape(depth, 1, n))


def _shortconv_kernel(x_ref, mod_ref, g_ref, win_ref, wconv_ref, wout_ref,
                      o_ref, z_ref, *, seg):
    d = D_MODEL
    x = x_ref[...]
    hx = _modulate(x, g_ref[...], mod_ref[0:1, :], mod_ref[1:2, :]).astype(BF16)
    cw = 256
    for j in range(d // cw):
        lo, hi = j * cw, (j + 1) * cw
        bg = _dot(hx, win_ref[:, lo:hi])
        cg = _dot(hx, win_ref[:, d + lo:d + hi])
        xv = _dot(hx, win_ref[:, 2 * d + lo:2 * d + hi])
        cv = _shift_conv(cg * xv, wconv_ref.at[:, lo:hi], seg)
        z_ref[:, lo:hi] = (bg * cv).astype(BF16)
    y = _dot(z_ref[...], wout_ref[...])
    o_ref[...] = x + mod_ref[2:3, :] * y


def _shortconv(x2d, mod_l, g, w_in, w_conv, w_out, *, tm, seg, row_fn):
    n, d = x2d.shape
    return pl.pallas_call(
        functools.partial(_shortconv_kernel, seg=seg),
        grid=(n // tm,),
        in_specs=[
            pl.BlockSpec((tm, d), lambda i: (i, 0)),
            pl.BlockSpec((None, 6, d), lambda i: (row_fn(i), 0, 0)),
            _const_spec((1, d)),
            _const_spec((d, 3 * d)),
            _const_spec((3, d)),
            _const_spec((d, d)),
        ],
        out_specs=pl.BlockSpec((tm, d), lambda i: (i, 0)),
        out_shape=jax.ShapeDtypeStruct((n, d), F32),
        scratch_shapes=[pltpu.VMEM((tm, d), BF16)],
        compiler_params=_params(1),
        name="shortconv_mixer",
    )(x2d, mod_l, g, w_in, w_conv, w_out)


def _ffn_kernel(*refs, halo, seg, tiles_per_seg, final):
    if halo:
        xp_ref, x_ref, xn_ref = refs[:3]
        refs = refs[3:]
    else:
        x_ref = refs[0]
        refs = refs[1:]
    (mod_ref, g_ref, wup_ref, wconv_ref, bconv_ref, wdown_ref, gfin_ref,
     o_ref, act_ref) = refs
    x = x_ref[...]
    tm = x.shape[0]
    g = g_ref[...]
    shift, scale = mod_ref[3:4, :], mod_ref[4:5, :]
    hx = _modulate(x, g, shift, scale).astype(BF16)
    if halo:
        j = pl.program_id(0) % tiles_per_seg
        hp = _modulate(xp_ref[...], g, shift, scale).astype(BF16)
        hn = _modulate(xn_ref[...], g, shift, scale).astype(BF16)
        hext = jnp.concatenate([hp, hx, hn], axis=0)
        row = lax.broadcasted_iota(jnp.int32, (tm, 1), 0)
        keep_prev = jnp.logical_or(j > 0, row >= halo)
        keep_next = jnp.logical_or(j < tiles_per_seg - 1, row < tm - halo)
    for c in range(D_FF // FF_CHUNK):
        lo, hi = c * FF_CHUNK, (c + 1) * FF_CHUNK
        val = _dot(hx, wup_ref[:, D_FF + lo:D_FF + hi])
        if halo:
            ge = _dot(hext, wup_ref[:, lo:hi])
            gp = jnp.where(keep_prev, ge[0:tm], 0.0)
            gm = ge[halo:halo + tm]
            gn = jnp.where(keep_next, ge[2 * halo:2 * halo + tm], 0.0)
            gate = (gp * wconv_ref[0:1, lo:hi] + gm * wconv_ref[1:2, lo:hi]
                    + gn * wconv_ref[2:3, lo:hi])
        else:
            gate = _shift_conv(_dot(hx, wup_ref[:, lo:hi]),
                               wconv_ref.at[:, lo:hi], seg)
        gate = gate + bconv_ref[:, lo:hi]
        act_ref[:, lo:hi] = (gate * _sigmoid(gate) * val).astype(BF16)
    y = _dot(act_ref[...], wdown_ref[...])
    out = x + mod_ref[5:6, :] * y
    if final:
        out = _rmsnorm(out, gfin_ref[...])
    o_ref[...] = out


def _ffn(x2d, mod_l, g, w_up, w_conv, b_conv, w_down, g_final, *, tm, halo,
         seg, row_fn, final):
    n, d = x2d.shape
    tiles_per_seg = seg // tm
    x_specs = [pl.BlockSpec((tm, d), lambda i: (i, 0))]
    x_args = [x2d]
    if halo:
        r = tm // halo
        last = n // halo - 1
        x_specs = [
            pl.BlockSpec((halo, d), lambda i: (jnp.maximum(i * r - 1, 0), 0)),
            x_specs[0],
            pl.BlockSpec((halo, d), lambda i: (jnp.minimum((i + 1) * r, last), 0)),
        ]
        x_args = [x2d, x2d, x2d]
    return pl.pallas_call(
        functools.partial(_ffn_kernel, halo=halo, seg=seg,
                          tiles_per_seg=tiles_per_seg, final=final),
        grid=(n // tm,),
        in_specs=x_specs + [
            pl.BlockSpec((None, 6, d), lambda i: (row_fn(i), 0, 0)),
            _const_spec((1, d)),
            _const_spec((d, 2 * D_FF)),
            _const_spec((3, D_FF)),
            _const_spec((1, D_FF)),
            _const_spec((D_FF, d)),
            _const_spec((1, d)),
        ],
        out_specs=pl.BlockSpec((tm, d), lambda i: (i, 0)),
        out_shape=jax.ShapeDtypeStruct((n, d), F32),
        scratch_shapes=[pltpu.VMEM((tm, D_FF), BF16)],
        compiler_params=_params(1),
        name="convglu_ffn",
    )(*x_args, mod_l, g, w_up, w_conv, b_conv, w_down, g_final)


def _log_sigmoid(v):
    return -(jnp.maximum(-v, 0.0) + jnp.log(1.0 + jnp.exp(-jnp.abs(v))))


def _chunk_scans(v, axis, chunk, op, identity):
    n = v.shape[axis]
    shape = (n, 1) if axis == 0 else (1, n)
    pos = lax.broadcasted_iota(jnp.int32, shape, axis) & (chunk - 1)
    pre, suf = v, v
    step = 1
    while step < chunk:
        pre = op(pre, jnp.where(pos >= step,
                                pltpu.roll(pre, step, axis=axis), identity))
        suf = op(suf, jnp.where(pos < chunk - step,
                                pltpu.roll(suf, n - step, axis=axis), identity))
        step *= 2
    return pre, suf


def _mlstm_proj_kernel(x_ref, mod_ref, g_ref, wqv_ref, wkt_ref, wg_ref,
                       wgt_ref, bg_ref, bgt_ref,
                       qv_ref, kt_ref, gc_ref, gr_ref):
    x = x_ref[...]
    hx = _modulate(x, g_ref[...], mod_ref[0:1, :], mod_ref[1:2, :]).astype(BF16)
    k_scale = DK ** -0.5
    qv_ref[...] = _dot(hx, wqv_ref[...]).astype(BF16)
    nt = (((1,), (1,)), ((), ()))
    kt = lax.dot_general(wkt_ref[...], hx, nt, preferred_element_type=F32)
    kt_ref[...] = (kt * k_scale).astype(BF16)

    half = N_GATE // 2
    gc = _dot(hx, wg_ref[...]) + bg_ref[...]
    lane = lax.broadcasted_iota(jnp.int32, (1, GATE_PAD), 1)
    fwd = (lane & (half - 1)) < N_HEADS
    pre, suf = _chunk_scans(_log_sigmoid(gc), 0, SCAN_CHUNK, jnp.add, 0.0)
    b = jnp.where(fwd, pre, suf)
    u = gc - pltpu.roll(b, GATE_PAD - half, axis=1)
    pre, suf = _chunk_scans(u, 0, SCAN_CHUNK, jnp.maximum, -jnp.inf)
    gc_ref[...] = jnp.where(lane < half, jnp.where(fwd, pre, suf), b)

    gr = lax.dot_general(wgt_ref[...], hx, nt, preferred_element_type=F32)
    gr = gr + bgt_ref[...]
    sub = lax.broadcasted_iota(jnp.int32, (N_GATE, 1), 0)
    pre, suf = _chunk_scans(_log_sigmoid(gr), 1, SCAN_CHUNK, jnp.add, 0.0)
    b = jnp.where((sub & (half - 1)) < N_HEADS, pre, suf)
    gr_ref[0:half, :] = gr[0:half] - b[half:]
    gr_ref[half:, :] = b[half:]


def _mlstm_proj(x2d, mod_l, g, wts, *, tm, row_fn):
    n, d = x2d.shape
    nq = N_HEADS * DK
    nqv = nq + N_HEADS * DV
    return pl.pallas_call(
        _mlstm_proj_kernel,
        grid=(n // tm,),
        in_specs=[
            pl.BlockSpec((tm, d), lambda i: (i, 0)),
            pl.BlockSpec((None, 6, d), lambda i: (row_fn(i), 0, 0)),
            _const_spec((1, d)),
            _const_spec((d, nqv)),
            _const_spec((nq, d)),
            _const_spec((d, GATE_PAD)),
            _const_spec((N_GATE, d)),
            _const_spec((1, GATE_PAD)),
            _const_spec((N_GATE, 1)),
        ],
        out_specs=[
            pl.BlockSpec((tm, nqv), lambda i: (i, 0)),
            pl.BlockSpec((nq, tm), lambda i: (0, i)),
            pl.BlockSpec((tm, GATE_PAD), lambda i: (i, 0)),
            pl.BlockSpec((N_GATE, tm), lambda i: (0, i)),
        ],
        out_shape=[
            jax.ShapeDtypeStruct((n, nqv), BF16),
            jax.ShapeDtypeStruct((nq, n), BF16),
            jax.ShapeDtypeStruct((n, GATE_PAD), F32),
            jax.ShapeDtypeStruct((N_GATE, n), F32),
        ],
        compiler_params=_params(1),
        name="mlstm_proj",
    )(x2d, mod_l, g, wts["w_qv"], wts["w_kt"], wts["w_g"], wts["w_gt"],
      wts["b_g"], wts["b_gt"])


def _scan_kernel(qv_ref, kt_ref, gc_ref, gr_ref, c0_ref, m0_ref,
                 h_ref, cf_ref, mf_ref, c_s, m_s, a_s, x_s, col_s, *, rev):
    t = pl.program_id(1)
    n_t = pl.num_programs(1)
    L = SCAN_CHUNK
    nq = N_HEADS * DK
    n_chunks = h_ref.shape[0] // L
    half = N_GATE // 2

    @pl.when(t == 0)
    def _():
        c_s[...] = c0_ref[...]
        m_s[...] = m0_ref[...]

    ti = lax.broadcasted_iota(jnp.int32, (L, L), 0)
    si = lax.broadcasted_iota(jnp.int32, (L, L), 1)
    mask = (si >= ti) if rev else (si <= ti)
    gcol = N_HEADS if rev else 0
    ones = jnp.ones((L, DK), BF16)
    order = range(n_chunks - 1, -1, -1) if rev else range(n_chunks)

    for c in order:
        r0, r1 = c * L, (c + 1) * L
        for h in range(N_HEADS):
            q = qv_ref[r0:r1, h * DK:(h + 1) * DK]
            v = qv_ref[r0:r1, nq + h * DV:nq + (h + 1) * DV]
            v_aug = jnp.concatenate([v, ones], axis=1)
            kt = kt_ref[h * DK:(h + 1) * DK, r0:r1]
            gi = gcol + h
            u_row = gr_ref[gi:gi + 1, r0:r1]
            cm_bc = jnp.broadcast_to(gc_ref[r0:r1, gi:gi + 1], (L, DK))
            b_bc = jnp.broadcast_to(gc_ref[r0:r1, half + gi:half + gi + 1],
                                    (L, DK))
            col_s[c, h, 0] = cm_bc
            col_s[c, h, 1] = b_bc
            cm_end = jnp.max(u_row, axis=1, keepdims=True)
            s = _dot(q, kt)
            w0 = jnp.exp(jnp.where(mask, u_row - cm_bc, -jnp.inf)) * s
            a_s[c, h] = _dot(w0.astype(BF16), v_aug)
            ktw = (kt.astype(F32) * jnp.exp(u_row - cm_end)).astype(BF16)
            x_s[c, h] = _dot(ktw, v_aug)

    for c in order:
        r0, r1 = c * L, (c + 1) * L
        for h in range(N_HEADS):
            gi = gcol + h
            q = qv_ref[r0:r1, h * DK:(h + 1) * DK]
            u_row = gr_ref[gi:gi + 1, r0:r1]
            b_row = gr_ref[half + gi:half + gi + 1, r0:r1]
            g = b_row[:, 0:1] if rev else b_row[:, L - 1:L]
            cm_end = jnp.max(u_row, axis=1, keepdims=True)
            m_prev = m_s[h]
            c_prev = c_s[h]
            a_aug = a_s[c, h]
            b_aug = _dot(q, c_prev.astype(BF16))
            cm_bc = col_s[c, h, 0]
            b_bc = col_s[c, h, 1]
            r = jnp.maximum(m_prev, cm_bc)
            rho = jnp.exp(cm_bc - r)
            wi = jnp.exp(m_prev - r)
            den = rho * a_aug[:, DV:] + wi * b_aug[:, DV:]
            inv = 1.0 / jnp.maximum(jnp.abs(den), jnp.exp(-(b_bc + r)))
            alpha = rho * inv
            beta = wi * inv
            for j in range(DV // DK):
                lo, hi = j * DK, (j + 1) * DK
                h_ref[r0:r1, h * DV + lo:h * DV + hi] = (
                    alpha * a_aug[:, lo:hi] + beta * b_aug[:, lo:hi])
            mx = jnp.maximum(m_prev, cm_end)
            decay = jnp.exp(m_prev - mx)
            gamma = jnp.exp(cm_end - mx)
            for j in range((DV + DK) // DK):
                lo, hi = j * DK, (j + 1) * DK
                c_s[h, :, lo:hi] = (decay * c_prev[:, lo:hi]
                                    + gamma * x_s[c, h, :, lo:hi])
            m_s[h] = g + mx

    @pl.when(t == n_t - 1)
    def _():
        cf_ref[...] = c_s[...]
        mf_ref[...] = m_s[...]


def _scan(qv, kt, gc, gr, state, *, batch, ts, rev):
    n = qv.shape[0]
    n_t = n // batch // ts
    n_chunks = ts // SCAN_CHUNK
    nq = N_HEADS * DK
    nqv = nq + N_HEADS * DV

    def blk(b, t):
        return b * n_t + ((n_t - 1 - t) if rev else t)

    st_shapes = [(N_HEADS, DK, DV + DK), (N_HEADS, 1, DK)]
    st_specs = [pl.BlockSpec((None,) + s, lambda b, t: (b, 0, 0, 0))
                for s in st_shapes]
    h, cf, mf = pl.pallas_call(
        functools.partial(_scan_kernel, rev=rev),
        grid=(batch, n_t),
        in_specs=[
            pl.BlockSpec((ts, nqv), lambda b, t: (blk(b, t), 0)),
            pl.BlockSpec((nq, ts), lambda b, t: (0, blk(b, t))),
            pl.BlockSpec((ts, GATE_PAD), lambda b, t: (blk(b, t), 0)),
            pl.BlockSpec((N_GATE, ts), lambda b, t: (0, blk(b, t))),
        ] + st_specs,
        out_specs=[pl.BlockSpec((ts, N_HEADS * DV), lambda b, t: (blk(b, t), 0))]
        + st_specs,
        out_shape=[jax.ShapeDtypeStruct((n, N_HEADS * DV), F32)]
        + [jax.ShapeDtypeStruct((batch,) + s, F32) for s in st_shapes],
        scratch_shapes=[pltpu.VMEM(s, F32) for s in st_shapes] + [
            pltpu.VMEM((n_chunks, N_HEADS, SCAN_CHUNK, DV + DK), F32),
            pltpu.VMEM((n_chunks, N_HEADS, DK, DV + DK), F32),
            pltpu.VMEM((n_chunks, N_HEADS, 2, SCAN_CHUNK, DK), F32),
        ],
        compiler_params=_params(2),
        name="mlstm_scan_bwd" if rev else "mlstm_scan_fwd",
    )(qv, kt, gc, gr, *state)
    return h, (cf, mf)


def _readout_kernel(x_ref, hf_ref, hb_ref, mod_ref, g_ref, wo_ref, gain_ref,
                    wout_ref, o_ref, z_ref):
    x = x_ref[...]
    hx = _modulate(x, g_ref[...], mod_ref[0:1, :], mod_ref[1:2, :]).astype(BF16)
    for h in range(N_HEADS):
        lo, hi = h * DV, (h + 1) * DV
        hh = hf_ref[:, lo:hi] + hb_ref[:, lo:hi]
        ms = jnp.mean(hh * hh, axis=-1, keepdims=True)
        hn = hh * lax.rsqrt(ms + EPS) * gain_ref[:, lo:hi]
        o = _dot(hx, wo_ref[:, lo:hi])
        z_ref[:, lo:hi] = (hn * _sigmoid(o)).astype(BF16)
    y = _dot(z_ref[...], wout_ref[...])
    o_ref[...] = x + mod_ref[2:3, :] * y


def _readout(x2d, hf, hb, mod_l, g, w_o, gain, w_out, *, tm, row_fn):
    n, d = x2d.shape
    nv = N_HEADS * DV
    return pl.pallas_call(
        _readout_kernel,
        grid=(n // tm,),
        in_specs=[
            pl.BlockSpec((tm, d), lambda i: (i, 0)),
            pl.BlockSpec((tm, nv), lambda i: (i, 0)),
            pl.BlockSpec((tm, nv), lambda i: (i, 0)),
            pl.BlockSpec((None, 6, d), lambda i: (row_fn(i), 0, 0)),
            _const_spec((1, d)),
            _const_spec((d, nv)),
            _const_spec((1, nv)),
            _const_spec((nv, d)),
        ],
        out_specs=pl.BlockSpec((tm, d), lambda i: (i, 0)),
        out_shape=jax.ShapeDtypeStruct((n, d), F32),
        scratch_shapes=[pltpu.VMEM((tm, nv), BF16)],
        compiler_params=_params(1),
        name="mlstm_readout",
    )(x2d, hf, hb, mod_l, g, w_o, gain, w_out)


def _mlstm_weights(w_in, b_gate):
    nq = N_HEADS * DK
    o2 = 2 * nq
    o3 = o2 + N_HEADS * DV
    o4 = o3 + D_MODEL
    w_g = w_in[:, o4:]
    return {
        "w_qv": jnp.concatenate([w_in[:, :nq], w_in[:, o2:o3]], axis=1).astype(BF16),
        "w_kt": w_in[:, nq:o2].T.astype(BF16),
        "w_o": w_in[:, o3:o4].astype(BF16),
        "w_g": jnp.pad(w_g, ((0, 0), (0, GATE_PAD - N_GATE))).astype(BF16),
        "w_gt": w_g.T.astype(BF16),
        "b_g": jnp.pad(b_gate, (0, GATE_PAD - N_GATE)).reshape(1, GATE_PAD),
        "b_gt": b_gate.reshape(N_GATE, 1),
    }


def kernel(x, c, ctx, c_ctx, w_mod, b_mod, g_mix, g_ffn, a_w_in, a_b_gate,
           a_head_gain, a_w_out, b_w_in, b_w_conv, b_w_out, f_w_up, f_w_conv,
           f_b_conv, f_w_down, g_final):
    bn, t, d = x.shape
    n_ctx = ctx.shape[1]
    depth = w_mod.shape[0]
    n_mixers = 2
    assert d == D_MODEL and bn <= 7 and t % TOKEN_TILE == 0
    assert n_ctx % SCAN_CHUNK == 0 and TOKEN_TILE % SCAN_CHUNK == 0

    rec_layers = [i for i in range(depth) if i % n_mixers == 0]
    last_rec = max(rec_layers) if rec_layers else -1

    cc = jnp.zeros((8, d), F32).at[:bn].set(c).at[bn].set(c_ctx)
    mod = _mod_all(cc, w_mod, b_mod).reshape(depth, 8, 6, d)

    tm = TOKEN_TILE
    tiles_per_batch = t // tm
    lat_row = lambda i: i // tiles_per_batch
    ctx_row = lambda i: bn

    xs = x.reshape(bn * t, d)
    cs = ctx.reshape(bn * n_ctx, d)
    zero_state = (jnp.zeros((bn, N_HEADS, DK, DV + DK), F32),
                  jnp.zeros((bn, N_HEADS, 1, DK), F32))

    for i in range(depth):
        j = i // n_mixers
        ctx_read = i <= last_rec
        ctx_live = i < last_rec
        gm = g_mix[i].reshape(1, d)
        gf = g_ffn[i].reshape(1, d)
        if i % n_mixers == 0:
            wts = _mlstm_weights(a_w_in[j], a_b_gate[j])
            w_out = a_w_out[j].astype(BF16)
            gain = a_head_gain[j].reshape(1, N_HEADS * DV)
            s_f = s_b = zero_state
            if ctx_read:
                pc = _mlstm_proj(cs, mod[i], gm, wts, tm=n_ctx, row_fn=ctx_row)
                hc_f, s_f = _scan(*pc, zero_state, batch=bn, ts=n_ctx, rev=False)
                hc_b, s_b = _scan(*pc, zero_state, batch=bn, ts=n_ctx, rev=True)
            px = _mlstm_proj(xs, mod[i], gm, wts, tm=tm, row_fn=lat_row)
            hl_f, _ = _scan(*px, s_f, batch=bn, ts=tm, rev=False)
            hl_b, _ = _scan(*px, s_b, batch=bn, ts=tm, rev=True)
            xs = _readout(xs, hl_f, hl_b, mod[i], gm, wts["w_o"], gain, w_out,
                          tm=tm, row_fn=lat_row)
            if ctx_live:
                cs = _readout(cs, hc_f, hc_b, mod[i], gm, wts["w_o"], gain,
                              w_out, tm=n_ctx, row_fn=ctx_row)
        else:
            w_in = b_w_in[j].astype(BF16)
            w_out = b_w_out[j].astype(BF16)
            xs = _shortconv(xs, mod[i], gm, w_in, b_w_conv[j], w_out,
                            tm=tm, seg=GRID_W, row_fn=lat_row)
            if ctx_live:
                cs = _shortconv(cs, mod[i], gm, w_in, b_w_conv[j], w_out,
                                tm=n_ctx, seg=n_ctx, row_fn=ctx_row)
        w_up = f_w_up[i].astype(BF16)
        w_down = f_w_down[i].astype(BF16)
        b_conv = f_b_conv[i].reshape(1, D_FF)
        gfin = g_final.reshape(1, d)
        xs = _ffn(xs, mod[i], gf, w_up, f_w_conv[i], b_conv, w_down, gfin,
                  tm=tm, halo=GRID_W, seg=t, row_fn=lat_row,
                  final=(i == depth - 1))
        if ctx_live:
            cs = _ffn(cs, mod[i], gf, w_up, f_w_conv[i], b_conv, w_down, gfin,
                      tm=n_ctx, halo=0, seg=n_ctx, row_fn=ctx_row, final=False)

    return xs.reshape(bn, t, d)
```

```python
import functools

import jax
import jax.numpy as jnp
from jax import lax
from jax.experimental import pallas as pl
from jax.experimental.pallas import tpu as pltpu

F32 = jnp.float32
BF16 = jnp.bfloat16

D_MODEL = 1024
GRID_W = 64
N_HEADS = 4
DK = 128
DV = 256
D_FF = 2816
EPS = 1e-6
N_GATE = 16
GATE_PAD = 128

SCAN_CHUNK = 128
FF_CHUNK = 256
TOKEN_TILE = 512
FFN_TILE = 1024
VMEM_LIMIT = 56 * 1024 * 1024


def _params(n_axes):
    return pltpu.CompilerParams(
        dimension_semantics=("arbitrary",) * n_axes,
        vmem_limit_bytes=VMEM_LIMIT)


def _const_spec(shape):
    zeros = (0,) * len(shape)
    return pl.BlockSpec(shape, lambda *_: zeros, pipeline_mode=pl.Buffered(1))


def _dot(a, b):
    return jnp.dot(a, b, preferred_element_type=F32)


def _sigmoid(v):
    return 1.0 / (1.0 + jnp.exp(-v))


def _rmsnorm(x, g):
    ms = jnp.mean(x * x, axis=-1, keepdims=True)
    return (x * lax.rsqrt(ms + EPS)) * g


def _modulate(x, g, shift, scale):
    return _rmsnorm(x, g) * (1.0 + scale) + shift


def _shift_conv(u, w_ref, seg):
    n = u.shape[0]
    pos = lax.broadcasted_iota(jnp.int32, (n, 1), 0) & (seg - 1)
    prev = jnp.where(pos >= 1, pltpu.roll(u, 1, axis=0), 0.0)
    nxt = jnp.where(pos < seg - 1, pltpu.roll(u, n - 1, axis=0), 0.0)
    return prev * w_ref[0:1, :] + u * w_ref[1:2, :] + nxt * w_ref[2:3, :]


def _mod_kernel(cc_ref, w_ref, b_ref, o_ref):
    cc = cc_ref[...]
    s = cc * _sigmoid(cc)
    o_ref[...] = _dot(s, w_ref[...]) + b_ref[...]


def _mod_all(cc, w_mod, b_mod):
    depth, d, n = w_mod.shape
    tn = 1536
    return pl.pallas_call(
        _mod_kernel,
        grid=(depth, n // tn),
        in_specs=[
            pl.BlockSpec((8, d), lambda l, j: (0, 0)),
            pl.BlockSpec((None, d, tn), lambda l, j: (l, 0, j)),
            pl.BlockSpec((None, 1, tn), lambda l, j: (l, 0, j)),
        ],
        out_specs=pl.BlockSpec((None, 8, tn), lambda l, j: (l, 0, j)),
        out_shape=jax.ShapeDtypeStruct((depth, 8, n), F32),
        compiler_params=_params(2),
        name="adaln_mod",
    )(cc, w_mod, b_mod.reshape(depth, 1, n))


def _shortconv_kernel(x_ref, mod_ref, g_ref, win_ref, wconv_ref, wout_ref,
                      o_ref, z_ref, *, seg):
    d = D_MODEL
    x = x_ref[...]
    hx = _modulate(x, g_ref[...], mod_ref[0:1, :], mod_ref[1:2, :]).astype(BF16)
    cw = 256
    for j in range(d // cw):
        lo, hi = j * cw, (j + 1) * cw
        bg = _dot(hx, win_ref[:, lo:hi])
        cg = _dot(hx, win_ref[:, d + lo:d + hi])
        xv = _dot(hx, win_ref[:, 2 * d + lo:2 * d + hi])
        cv = _shift_conv(cg * xv, wconv_ref.at[:, lo:hi], seg)
        z_ref[:, lo:hi] = (bg * cv).astype(BF16)
    y = _dot(z_ref[...], wout_ref[...])
    o_ref[...] = x + mod_ref[2:3, :] * y


def _shortconv(x2d, mod_l, g, w_in, w_conv, w_out, *, tm, seg, row_fn):
    n, d = x2d.shape
    return pl.pallas_call(
        functools.partial(_shortconv_kernel, seg=seg),
        grid=(n // tm,),
        in_specs=[
            pl.BlockSpec((tm, d), lambda i: (i, 0)),
            pl.BlockSpec((None, 6, d), lambda i: (row_fn(i), 0, 0)),
            _const_spec((1, d)),
            _const_spec((d, 3 * d)),
            _const_spec((3, d)),
            _const_spec((d, d)),
        ],
        out_specs=pl.BlockSpec((tm, d), lambda i: (i, 0)),
        out_shape=jax.ShapeDtypeStruct((n, d), F32),
        scratch_shapes=[pltpu.VMEM((tm, d), BF16)],
        compiler_params=_params(1),
        name="shortconv_mixer",
    )(x2d, mod_l, g, w_in, w_conv, w_out)


def _ffn_kernel(*refs, halo, seg, tiles_per_seg, final):
    if halo:
        xp_ref, x_ref, xn_ref = refs[:3]
        refs = refs[3:]
    else:
        x_ref = refs[0]
        refs = refs[1:]
    (mod_ref, g_ref, wup_ref, wconv_ref, bconv_ref, wdown_ref, gfin_ref,
     o_ref, act_ref) = refs
    x = x_ref[...]
    tm = x.shape[0]
    g = g_ref[...]
    shift, scale = mod_ref[3:4, :], mod_ref[4:5, :]
    hx = _modulate(x, g, shift, scale).astype(BF16)
    if halo:
        j = pl.program_id(0) % tiles_per_seg
        hp = _modulate(xp_ref[...], g, shift, scale).astype(BF16)
        hn = _modulate(xn_ref[...], g, shift, scale).astype(BF16)
        hext = jnp.concatenate([hp, hx, hn], axis=0)
        row = lax.broadcasted_iota(jnp.int32, (tm, 1), 0)
        keep_prev = jnp.logical_or(j > 0, row >= halo)
        keep_next = jnp.logical_or(j < tiles_per_seg - 1, row < tm - halo)
    for c in range(D_FF // FF_CHUNK):
        lo, hi = c * FF_CHUNK, (c + 1) * FF_CHUNK
        val = _dot(hx, wup_ref[:, D_FF + lo:D_FF + hi])
        if halo:
            ge = _dot(hext, wup_ref[:, lo:hi])
            gp = jnp.where(keep_prev, ge[0:tm], 0.0)
            gm = ge[halo:halo + tm]
            gn = jnp.where(keep_next, ge[2 * halo:2 * halo + tm], 0.0)
            gate = (gp * wconv_ref[0:1, lo:hi] + gm * wconv_ref[1:2, lo:hi]
                    + gn * wconv_ref[2:3, lo:hi])
        else:
            gate = _shift_conv(_dot(hx, wup_ref[:, lo:hi]),
                               wconv_ref.at[:, lo:hi], seg)
        gate = gate + bconv_ref[:, lo:hi]
        act_ref[:, lo:hi] = (gate * _sigmoid(gate) * val).astype(BF16)
    y = _dot(act_ref[...], wdown_ref[...])
    out = x + mod_ref[5:6, :] * y
    if final:
        out = _rmsnorm(out, gfin_ref[...])
    o_ref[...] = out


def _ffn(x2d, mod_l, g, w_up, w_conv, b_conv, w_down, g_final, *, tm, halo,
         seg, row_fn, final):
    n, d = x2d.shape
    tiles_per_seg = seg // tm
    x_specs = [pl.BlockSpec((tm, d), lambda i: (i, 0))]
    x_args = [x2d]
    if halo:
        r = tm // halo
        last = n // halo - 1
        x_specs = [
            pl.BlockSpec((halo, d), lambda i: (jnp.maximum(i * r - 1, 0), 0)),
            x_specs[0],
            pl.BlockSpec((halo, d), lambda i: (jnp.minimum((i + 1) * r, last), 0)),
        ]
        x_args = [x2d, x2d, x2d]
    return pl.pallas_call(
        functools.partial(_ffn_kernel, halo=halo, seg=seg,
                          tiles_per_seg=tiles_per_seg, final=final),
        grid=(n // tm,),
        in_specs=x_specs + [
            pl.BlockSpec((None, 6, d), lambda i: (row_fn(i), 0, 0)),
            _const_spec((1, d)),
            _const_spec((d, 2 * D_FF)),
            _const_spec((3, D_FF)),
            _const_spec((1, D_FF)),
            _const_spec((D_FF, d)),
            _const_spec((1, d)),
        ],
        out_specs=pl.BlockSpec((tm, d), lambda i: (i, 0)),
        out_shape=jax.ShapeDtypeStruct((n, d), F32),
        scratch_shapes=[pltpu.VMEM((tm, D_FF), BF16)],
        compiler_params=_params(1),
        name="convglu_ffn",
    )(*x_args, mod_l, g, w_up, w_conv, b_conv, w_down, g_final)


def _log_sigmoid(v):
    return -(jnp.maximum(-v, 0.0) + jnp.log(1.0 + jnp.exp(-jnp.abs(v))))


def _chunk_scans(v, axis, chunk, op, identity):
    n = v.shape[axis]
    shape = (n, 1) if axis == 0 else (1, n)
    pos = lax.broadcasted_iota(jnp.int32, shape, axis) & (chunk - 1)
    pre, suf = v, v
    step = 1
    while step < chunk:
        pre = op(pre, jnp.where(pos >= step,
                                pltpu.roll(pre, step, axis=axis), identity))
        suf = op(suf, jnp.where(pos < chunk - step,
                                pltpu.roll(suf, n - step, axis=axis), identity))
        step *= 2
    return pre, suf


def _mlstm_proj_kernel(x_ref, mod_ref, g_ref, wqv_ref, wkt_ref, wg_ref,
                       wgt_ref, bg_ref, bgt_ref,
                       qv_ref, kt_ref, gc_ref, gr_ref):
    x = x_ref[...]
    hx = _modulate(x, g_ref[...], mod_ref[0:1, :], mod_ref[1:2, :]).astype(BF16)
    k_scale = DK ** -0.5
    qv_ref[...] = _dot(hx, wqv_ref[...]).astype(BF16)
    nt = (((1,), (1,)), ((), ()))
    kt = lax.dot_general(wkt_ref[...], hx, nt, preferred_element_type=F32)
    kt_ref[...] = (kt * k_scale).astype(BF16)

    half = N_GATE // 2
    gc = _dot(hx, wg_ref[...]) + bg_ref[...]
    lane = lax.broadcasted_iota(jnp.int32, (1, GATE_PAD), 1)
    fwd = (lane & (half - 1)) < N_HEADS
    pre, suf = _chunk_scans(_log_sigmoid(gc), 0, SCAN_CHUNK, jnp.add, 0.0)
    b = jnp.where(fwd, pre, suf)
    u = gc - pltpu.roll(b, GATE_PAD - half, axis=1)
    pre, suf = _chunk_scans(u, 0, SCAN_CHUNK, jnp.maximum, -jnp.inf)
    gc_ref[...] = jnp.where(lane < half, jnp.where(fwd, pre, suf), b)

    gr = lax.dot_general(wgt_ref[...], hx, nt, preferred_element_type=F32)
    gr = gr + bgt_ref[...]
    sub = lax.broadcasted_iota(jnp.int32, (N_GATE, 1), 0)
    pre, suf = _chunk_scans(_log_sigmoid(gr), 1, SCAN_CHUNK, jnp.add, 0.0)
    b = jnp.where((sub & (half - 1)) < N_HEADS, pre, suf)
    gr_ref[0:half, :] = gr[0:half] - b[half:]
    gr_ref[half:, :] = b[half:]


def _mlstm_proj(x2d, mod_l, g, wts, *, tm, row_fn):
    n, d = x2d.shape
    nq = N_HEADS * DK
    nqv = nq + N_HEADS * DV
    return pl.pallas_call(
        _mlstm_proj_kernel,
        grid=(n // tm,),
        in_specs=[
            pl.BlockSpec((tm, d), lambda i: (i, 0)),
            pl.BlockSpec((None, 6, d), lambda i: (row_fn(i), 0, 0)),
            _const_spec((1, d)),
            _const_spec((d, nqv)),
            _const_spec((nq, d)),
            _const_spec((d, GATE_PAD)),
            _const_spec((N_GATE, d)),
            _const_spec((1, GATE_PAD)),
            _const_spec((N_GATE, 1)),
        ],
        out_specs=[
            pl.BlockSpec((tm, nqv), lambda i: (i, 0)),
            pl.BlockSpec((nq, tm), lambda i: (0, i)),
            pl.BlockSpec((tm, GATE_PAD), lambda i: (i, 0)),
            pl.BlockSpec((N_GATE, tm), lambda i: (0, i)),
        ],
        out_shape=[
            jax.ShapeDtypeStruct((n, nqv), BF16),
            jax.ShapeDtypeStruct((nq, n), BF16),
            jax.ShapeDtypeStruct((n, GATE_PAD), F32),
            jax.ShapeDtypeStruct((N_GATE, n), F32),
        ],
        compiler_params=_params(1),
        name="mlstm_proj",
    )(x2d, mod_l, g, wts["w_qv"], wts["w_kt"], wts["w_g"], wts["w_gt"],
      wts["b_g"], wts["b_gt"])


def _scan_kernel(*refs, rev, has_add):
    qv_ref, kt_ref, gc_ref, gr_ref, c0_ref, m0_ref = refs[:6]
    hadd_ref = refs[6] if has_add else None
    (h_ref, cf_ref, mf_ref, c_s, m_s, a_s, x_s, s_s, w_s,
     ktw_s) = refs[7:] if has_add else refs[6:]
    t = pl.program_id(1)
    n_t = pl.num_programs(1)
    L = SCAN_CHUNK
    nq = N_HEADS * DK
    n_chunks = h_ref.shape[0] // L
    half = N_GATE // 2

    @pl.when(t == 0)
    def _():
        c_s[...] = c0_ref[...]
        m_s[...] = m0_ref[...]

    ti = lax.broadcasted_iota(jnp.int32, (L, L), 0)
    si = lax.broadcasted_iota(jnp.int32, (L, L), 1)
    mask = (si >= ti) if rev else (si <= ti)
    gcol = N_HEADS if rev else 0
    ones = jnp.ones((L, DK), BF16)
    order = range(n_chunks - 1, -1, -1) if rev else range(n_chunks)

    units = [(c, h) for c in order for h in range(N_HEADS)]

    for c, h in units:
        r0, r1 = c * L, (c + 1) * L
        q = qv_ref[r0:r1, h * DK:(h + 1) * DK]
        kt = kt_ref[h * DK:(h + 1) * DK, r0:r1]
        s_s[c, h] = _dot(q, kt)
        gi = gcol + h
        u_row = gr_ref[gi:gi + 1, r0:r1]
        cm_end = jnp.max(u_row, axis=1, keepdims=True)
        ktw_s[c, h] = (kt.astype(F32) * jnp.exp(u_row - cm_end)).astype(BF16)
    for c, h in units:
        r0, r1 = c * L, (c + 1) * L
        gi = gcol + h
        u_row = gr_ref[gi:gi + 1, r0:r1]
        cm_bc = jnp.broadcast_to(gc_ref[r0:r1, gi:gi + 1], (L, DK))
        w0 = jnp.exp(jnp.where(mask, u_row - cm_bc, -jnp.inf)) * s_s[c, h]
        w_s[c, h] = w0.astype(BF16)

    for c in order:
        r0, r1 = c * L, (c + 1) * L
        for h in range(N_HEADS):
            v = qv_ref[r0:r1, nq + h * DV:nq + (h + 1) * DV]
            v_aug = jnp.concatenate([v, ones], axis=1)
            a_s[h] = _dot(w_s[c, h], v_aug)
            x_s[h] = _dot(ktw_s[c, h], v_aug)
        for h in range(N_HEADS):
            gi = gcol + h
            q = qv_ref[r0:r1, h * DK:(h + 1) * DK]
            u_row = gr_ref[gi:gi + 1, r0:r1]
            b_row = gr_ref[half + gi:half + gi + 1, r0:r1]
            g = b_row[:, 0:1] if rev else b_row[:, L - 1:L]
            cm_end = jnp.max(u_row, axis=1, keepdims=True)
            m_prev = m_s[h]
            c_prev = c_s[h]
            a_aug = a_s[h]
            b_aug = _dot(q, c_prev.astype(BF16))
            cm_bc = jnp.broadcast_to(gc_ref[r0:r1, gi:gi + 1], (L, DK))
            b_bc = jnp.broadcast_to(gc_ref[r0:r1, half + gi:half + gi + 1],
                                    (L, DK))
            r = jnp.maximum(m_prev, cm_bc)
            rho = jnp.exp(cm_bc - r)
            wi = jnp.exp(m_prev - r)
            den = rho * a_aug[:, DV:] + wi * b_aug[:, DV:]
            inv = 1.0 / jnp.maximum(jnp.abs(den), jnp.exp(-(b_bc + r)))
            alpha = rho * inv
            beta = wi * inv
            for j in range(DV // DK):
                lo, hi = j * DK, (j + 1) * DK
                cols = slice(h * DV + lo, h * DV + hi)
                hh = alpha * a_aug[:, lo:hi] + beta * b_aug[:, lo:hi]
                if has_add:
                    hh = hh + hadd_ref[r0:r1, cols]
                h_ref[r0:r1, cols] = hh
            mx = jnp.maximum(m_prev, cm_end)
            decay = jnp.exp(m_prev - mx)
            gamma = jnp.exp(cm_end - mx)
            for j in range((DV + DK) // DK):
                lo, hi = j * DK, (j + 1) * DK
                c_s[h, :, lo:hi] = (decay * c_prev[:, lo:hi]
                                    + gamma * x_s[h, :, lo:hi])
            m_s[h] = g + mx

    @pl.when(t == n_t - 1)
    def _():
        cf_ref[...] = c_s[...]
        mf_ref[...] = m_s[...]


def _scan(qv, kt, gc, gr, state, *, batch, ts, rev, add=None):
    n = qv.shape[0]
    n_t = n // batch // ts
    n_chunks = ts // SCAN_CHUNK
    nq = N_HEADS * DK
    nqv = nq + N_HEADS * DV

    def blk(b, t):
        return b * n_t + ((n_t - 1 - t) if rev else t)

    st_shapes = [(N_HEADS, DK, DV + DK), (N_HEADS, 1, DK)]
    st_specs = [pl.BlockSpec((None,) + s, lambda b, t: (b, 0, 0, 0))
                for s in st_shapes]
    h_spec = pl.BlockSpec((ts, N_HEADS * DV), lambda b, t: (blk(b, t), 0))
    has_add = add is not None
    h, cf, mf = pl.pallas_call(
        functools.partial(_scan_kernel, rev=rev, has_add=has_add),
        grid=(batch, n_t),
        in_specs=[
            pl.BlockSpec((ts, nqv), lambda b, t: (blk(b, t), 0)),
            pl.BlockSpec((nq, ts), lambda b, t: (0, blk(b, t))),
            pl.BlockSpec((ts, GATE_PAD), lambda b, t: (blk(b, t), 0)),
            pl.BlockSpec((N_GATE, ts), lambda b, t: (0, blk(b, t))),
        ] + st_specs + ([h_spec] if has_add else []),
        out_specs=[h_spec] + st_specs,
        out_shape=[jax.ShapeDtypeStruct((n, N_HEADS * DV), F32)]
        + [jax.ShapeDtypeStruct((batch,) + s, F32) for s in st_shapes],
        scratch_shapes=[pltpu.VMEM(s, F32) for s in st_shapes] + [
            pltpu.VMEM((N_HEADS, SCAN_CHUNK, DV + DK), F32),
            pltpu.VMEM((N_HEADS, DK, DV + DK), F32),
            pltpu.VMEM((n_chunks, N_HEADS, SCAN_CHUNK, SCAN_CHUNK), F32),
            pltpu.VMEM((n_chunks, N_HEADS, SCAN_CHUNK, SCAN_CHUNK), BF16),
            pltpu.VMEM((n_chunks, N_HEADS, DK, SCAN_CHUNK), BF16),
        ],
        compiler_params=_params(2),
        name="mlstm_scan_bwd" if rev else "mlstm_scan_fwd",
    )(qv, kt, gc, gr, *state, *([add] if has_add else []))
    return h, (cf, mf)


def _readout_kernel(x_ref, h_ref, mod_ref, g_ref, wo_ref, gain_ref,
                    wout_ref, o_ref, z_ref):
    x = x_ref[...]
    hx = _modulate(x, g_ref[...], mod_ref[0:1, :], mod_ref[1:2, :]).astype(BF16)
    for h in range(N_HEADS):
        lo, hi = h * DV, (h + 1) * DV
        hh = h_ref[:, lo:hi]
        ms = jnp.mean(hh * hh, axis=-1, keepdims=True)
        hn = hh * lax.rsqrt(ms + EPS) * gain_ref[:, lo:hi]
        o = _dot(hx, wo_ref[:, lo:hi])
        z_ref[:, lo:hi] = (hn * _sigmoid(o)).astype(BF16)
    y = _dot(z_ref[...], wout_ref[...])
    o_ref[...] = x + mod_ref[2:3, :] * y


def _readout(x2d, h_sum, mod_l, g, w_o, gain, w_out, *, tm, row_fn):
    n, d = x2d.shape
    nv = N_HEADS * DV
    return pl.pallas_call(
        _readout_kernel,
        grid=(n // tm,),
        in_specs=[
            pl.BlockSpec((tm, d), lambda i: (i, 0)),
            pl.BlockSpec((tm, nv), lambda i: (i, 0)),
            pl.BlockSpec((None, 6, d), lambda i: (row_fn(i), 0, 0)),
            _const_spec((1, d)),
            _const_spec((d, nv)),
            _const_spec((1, nv)),
            _const_spec((nv, d)),
        ],
        out_specs=pl.BlockSpec((tm, d), lambda i: (i, 0)),
        out_shape=jax.ShapeDtypeStruct((n, d), F32),
        scratch_shapes=[pltpu.VMEM((tm, nv), BF16)],
        compiler_params=_params(1),
        name="mlstm_readout",
    )(x2d, h_sum, mod_l, g, w_o, gain, w_out)


def _mlstm_weights(w_in, b_gate):
    nq = N_HEADS * DK
    o2 = 2 * nq
    o3 = o2 + N_HEADS * DV
    o4 = o3 + D_MODEL
    w_g = w_in[:, o4:]
    return {
        "w_qv": jnp.concatenate([w_in[:, :nq], w_in[:, o2:o3]], axis=1).astype(BF16),
        "w_kt": w_in[:, nq:o2].T.astype(BF16),
        "w_o": w_in[:, o3:o4].astype(BF16),
        "w_g": jnp.pad(w_g, ((0, 0), (0, GATE_PAD - N_GATE))).astype(BF16),
        "w_gt": w_g.T.astype(BF16),
        "b_g": jnp.pad(b_gate, (0, GATE_PAD - N_GATE)).reshape(1, GATE_PAD),
        "b_gt": b_gate.reshape(N_GATE, 1),
    }


def kernel(x, c, ctx, c_ctx, w_mod, b_mod, g_mix, g_ffn, a_w_in, a_b_gate,
           a_head_gain, a_w_out, b_w_in, b_w_conv, b_w_out, f_w_up, f_w_conv,
           f_b_conv, f_w_down, g_final):
    bn, t, d = x.shape
    n_ctx = ctx.shape[1]
    depth = w_mod.shape[0]
    n_mixers = 2
    assert d == D_MODEL and bn <= 7 and t % TOKEN_TILE == 0
    assert n_ctx % SCAN_CHUNK == 0 and TOKEN_TILE % SCAN_CHUNK == 0

    rec_layers = [i for i in range(depth) if i % n_mixers == 0]
    last_rec = max(rec_layers) if rec_layers else -1

    cc = jnp.zeros((8, d), F32).at[:bn].set(c).at[bn].set(c_ctx)
    mod = _mod_all(cc, w_mod, b_mod).reshape(depth, 8, 6, d)

    tm = TOKEN_TILE
    tiles_per_batch = t // tm
    lat_row = lambda i: i // tiles_per_batch
    ctx_row = lambda i: bn

    xs = x.reshape(bn * t, d)
    cs = ctx.reshape(bn * n_ctx, d)
    zero_state = (jnp.zeros((bn, N_HEADS, DK, DV + DK), F32),
                  jnp.zeros((bn, N_HEADS, 1, DK), F32))

    for i in range(depth):
        j = i // n_mixers
        ctx_read = i <= last_rec
        ctx_live = i < last_rec
        gm = g_mix[i].reshape(1, d)
        gf = g_ffn[i].reshape(1, d)
        if i % n_mixers == 0:
            wts = _mlstm_weights(a_w_in[j], a_b_gate[j])
            w_out = a_w_out[j].astype(BF16)
            gain = a_head_gain[j].reshape(1, N_HEADS * DV)
            s_f = s_b = zero_state
            if ctx_read:
                pc = _mlstm_proj(cs, mod[i], gm, wts, tm=n_ctx, row_fn=ctx_row)
                hc, s_b = _scan(*pc, zero_state, batch=bn, ts=n_ctx, rev=True)
                hc, s_f = _scan(*pc, zero_state, batch=bn, ts=n_ctx, rev=False,
                                add=hc)
            px = _mlstm_proj(xs, mod[i], gm, wts, tm=tm, row_fn=lat_row)
            hl, _ = _scan(*px, s_b, batch=bn, ts=tm, rev=True)
            hl, _ = _scan(*px, s_f, batch=bn, ts=tm, rev=False, add=hl)
            xs = _readout(xs, hl, mod[i], gm, wts["w_o"], gain, w_out,
                          tm=tm, row_fn=lat_row)
            if ctx_live:
                cs = _readout(cs, hc, mod[i], gm, wts["w_o"], gain,
                              w_out, tm=n_ctx, row_fn=ctx_row)
        else:
            w_in = b_w_in[j].astype(BF16)
            w_out = b_w_out[j].astype(BF16)
            xs = _shortconv(xs, mod[i], gm, w_in, b_w_conv[j], w_out,
                            tm=tm, seg=GRID_W, row_fn=lat_row)
            if ctx_live:
                cs = _shortconv(cs, mod[i], gm, w_in, b_w_conv[j], w_out,
                                tm=n_ctx, seg=n_ctx, row_fn=ctx_row)
        w_up = f_w_up[i].astype(BF16)
        w_down = f_w_down[i].astype(BF16)
        b_conv = f_b_conv[i].reshape(1, D_FF)
        gfin = g_final.reshape(1, d)
        xs = _ffn(xs, mod[i], gf, w_up, f_w_conv[i], b_conv, w_down, gfin,
                  tm=FFN_TILE, halo=GRID_W, seg=t,
                  row_fn=lambda i: i // (t // FFN_TILE),
                  final=(i == depth - 1))
        if ctx_live:
            cs = _ffn(cs, mod[i], gf, w_up, f_w_conv[i], b_conv, w_down, gfin,
                      tm=n_ctx, halo=0, seg=n_ctx, row_fn=ctx_row, final=False)

    return xs.reshape(bn, t, d)
```

```python
import functools

import jax
import jax.numpy as jnp
from jax import lax
from jax.experimental import pallas as pl
from jax.experimental.pallas import tpu as pltpu

F32 = jnp.float32
BF16 = jnp.bfloat16

D_MODEL = 1024
GRID_W = 64
N_HEADS = 4
DK = 128
DV = 256
D_FF = 2816
EPS = 1e-6
N_GATE = 16
GATE_PAD = 128

SCAN_CHUNK = 128
FF_CHUNK = 256
TOKEN_TILE = 512
FFN_TILE = 1024
VMEM_LIMIT = 56 * 1024 * 1024


def _params(n_axes):
    return pltpu.CompilerParams(
        dimension_semantics=("arbitrary",) * n_axes,
        vmem_limit_bytes=VMEM_LIMIT)


def _const_spec(shape):
    zeros = (0,) * len(shape)
    return pl.BlockSpec(shape, lambda *_: zeros, pipeline_mode=pl.Buffered(1))


def _dot(a, b):
    return jnp.dot(a, b, preferred_element_type=F32)


def _sigmoid(v):
    return 1.0 / (1.0 + jnp.exp(-v))


def _rmsnorm(x, g):
    ms = jnp.mean(x * x, axis=-1, keepdims=True)
    return (x * lax.rsqrt(ms + EPS)) * g


def _modulate(x, g, shift, scale):
    return _rmsnorm(x, g) * (1.0 + scale) + shift


def _shift_conv(u, w_ref, seg):
    n = u.shape[0]
    pos = lax.broadcasted_iota(jnp.int32, (n, 1), 0) & (seg - 1)
    prev = jnp.where(pos >= 1, pltpu.roll(u, 1, axis=0), 0.0)
    nxt = jnp.where(pos < seg - 1, pltpu.roll(u, n - 1, axis=0), 0.0)
    return prev * w_ref[0:1, :] + u * w_ref[1:2, :] + nxt * w_ref[2:3, :]


def _mod_kernel(cc_ref, w_ref, b_ref, o_ref):
    cc = cc_ref[...]
    s = cc * _sigmoid(cc)
    o_ref[...] = _dot(s, w_ref[...]) + b_ref[...]


def _mod_all(cc, w_mod, b_mod):
    depth, d, n = w_mod.shape
    tn = 1536
    return pl.pallas_call(
        _mod_kernel,
        grid=(depth, n // tn),
        in_specs=[
            pl.BlockSpec((8, d), lambda l, j: (0, 0)),
            pl.BlockSpec((None, d, tn), lambda l, j: (l, 0, j)),
            pl.BlockSpec((None, 1, tn), lambda l, j: (l, 0, j)),
        ],
        out_specs=pl.BlockSpec((None, 8, tn), lambda l, j: (l, 0, j)),
        out_shape=jax.ShapeDtypeStruct((depth, 8, n), F32),
        compiler_params=_params(2),
        name="adaln_mod",
    )(cc, w_mod, b_mod.reshape(depth, 1, n))


def _shortconv_kernel(x_ref, mod_ref, g_ref, win_ref, wconv_ref, wout_ref,
                      o_ref, z_ref, *, seg):
    d = D_MODEL
    x = x_ref[...]
    hx = _modulate(x, g_ref[...], mod_ref[0:1, :], mod_ref[1:2, :]).astype(BF16)
    cw = 256
    for j in range(d // cw):
        lo, hi = j * cw, (j + 1) * cw
        bg = _dot(hx, win_ref[:, lo:hi])
        cg = _dot(hx, win_ref[:, d + lo:d + hi])
        xv = _dot(hx, win_ref[:, 2 * d + lo:2 * d + hi])
        cv = _shift_conv(cg * xv, wconv_ref.at[:, lo:hi], seg)
        z_ref[:, lo:hi] = (bg * cv).astype(BF16)
    y = _dot(z_ref[...], wout_ref[...])
    o_ref[...] = x + mod_ref[2:3, :] * y


def _shortconv(x2d, mod_l, g, w_in, w_conv, w_out, *, tm, seg, row_fn):
    n, d = x2d.shape
    return pl.pallas_call(
        functools.partial(_shortconv_kernel, seg=seg),
        grid=(n // tm,),
        in_specs=[
            pl.BlockSpec((tm, d), lambda i: (i, 0)),
            pl.BlockSpec((None, 6, d), lambda i: (row_fn(i), 0, 0)),
            _const_spec((1, d)),
            _const_spec((d, 3 * d)),
            _const_spec((3, d)),
            _const_spec((d, d)),
        ],
        out_specs=pl.BlockSpec((tm, d), lambda i: (i, 0)),
        out_shape=jax.ShapeDtypeStruct((n, d), F32),
        scratch_shapes=[pltpu.VMEM((tm, d), BF16)],
        compiler_params=_params(1),
        name="shortconv_mixer",
    )(x2d, mod_l, g, w_in, w_conv, w_out)


def _ffn_kernel(*refs, halo, seg, tiles_per_seg, final):
    if halo:
        xp_ref, x_ref, xn_ref = refs[:3]
        refs = refs[3:]
    else:
        x_ref = refs[0]
        refs = refs[1:]
    (mod_ref, g_ref, wup_ref, wconv_ref, bconv_ref, wdown_ref, gfin_ref,
     o_ref, act_ref) = refs
    x = x_ref[...]
    tm = x.shape[0]
    g = g_ref[...]
    shift, scale = mod_ref[3:4, :], mod_ref[4:5, :]
    hx = _modulate(x, g, shift, scale).astype(BF16)
    if halo:
        j = pl.program_id(0) % tiles_per_seg
        hp = _modulate(xp_ref[...], g, shift, scale).astype(BF16)
        hn = _modulate(xn_ref[...], g, shift, scale).astype(BF16)
        hext = jnp.concatenate([hp, hx, hn], axis=0)
        row = lax.broadcasted_iota(jnp.int32, (tm, 1), 0)
        keep_prev = jnp.logical_or(j > 0, row >= halo)
        keep_next = jnp.logical_or(j < tiles_per_seg - 1, row < tm - halo)
    for c in range(D_FF // FF_CHUNK):
        lo, hi = c * FF_CHUNK, (c + 1) * FF_CHUNK
        val = _dot(hx, wup_ref[:, D_FF + lo:D_FF + hi])
        if halo:
            ge = _dot(hext, wup_ref[:, lo:hi])
            gp = jnp.where(keep_prev, ge[0:tm], 0.0)
            gm = ge[halo:halo + tm]
            gn = jnp.where(keep_next, ge[2 * halo:2 * halo + tm], 0.0)
            gate = (gp * wconv_ref[0:1, lo:hi] + gm * wconv_ref[1:2, lo:hi]
                    + gn * wconv_ref[2:3, lo:hi])
        else:
            gate = _shift_conv(_dot(hx, wup_ref[:, lo:hi]),
                               wconv_ref.at[:, lo:hi], seg)
        gate = gate + bconv_ref[:, lo:hi]
        act_ref[:, lo:hi] = (gate * _sigmoid(gate) * val).astype(BF16)
    y = _dot(act_ref[...], wdown_ref[...])
    out = x + mod_ref[5:6, :] * y
    if final:
        out = _rmsnorm(out, gfin_ref[...])
    o_ref[...] = out


def _ffn(x2d, mod_l, g, w_up, w_conv, b_conv, w_down, g_final, *, tm, halo,
         seg, row_fn, final):
    n, d = x2d.shape
    tiles_per_seg = seg // tm
    x_specs = [pl.BlockSpec((tm, d), lambda i: (i, 0))]
    x_args = [x2d]
    if halo:
        r = tm // halo
        last = n // halo - 1
        x_specs = [
            pl.BlockSpec((halo, d), lambda i: (jnp.maximum(i * r - 1, 0), 0)),
            x_specs[0],
            pl.BlockSpec((halo, d), lambda i: (jnp.minimum((i + 1) * r, last), 0)),
        ]
        x_args = [x2d, x2d, x2d]
    return pl.pallas_call(
        functools.partial(_ffn_kernel, halo=halo, seg=seg,
                          tiles_per_seg=tiles_per_seg, final=final),
        grid=(n // tm,),
        in_specs=x_specs + [
            pl.BlockSpec((None, 6, d), lambda i: (row_fn(i), 0, 0)),
            _const_spec((1, d)),
            _const_spec((d, 2 * D_FF)),
            _const_spec((3, D_FF)),
            _const_spec((1, D_FF)),
            _const_spec((D_FF, d)),
            _const_spec((1, d)),
        ],
        out_specs=pl.BlockSpec((tm, d), lambda i: (i, 0)),
        out_shape=jax.ShapeDtypeStruct((n, d), F32),
        scratch_shapes=[pltpu.VMEM((tm, D_FF), BF16)],
        compiler_params=_params(1),
        name="convglu_ffn",
    )(*x_args, mod_l, g, w_up, w_conv, b_conv, w_down, g_final)


def _log_sigmoid(v):
    return -(jnp.maximum(-v, 0.0) + jnp.log(1.0 + jnp.exp(-jnp.abs(v))))


def _chunk_scans(v, axis, chunk, op, identity):
    n = v.shape[axis]
    shape = (n, 1) if axis == 0 else (1, n)
    pos = lax.broadcasted_iota(jnp.int32, shape, axis) & (chunk - 1)
    pre, suf = v, v
    step = 1
    while step < chunk:
        pre = op(pre, jnp.where(pos >= step,
                                pltpu.roll(pre, step, axis=axis), identity))
        suf = op(suf, jnp.where(pos < chunk - step,
                                pltpu.roll(suf, n - step, axis=axis), identity))
        step *= 2
    return pre, suf


def _mlstm_proj_kernel(x_ref, mod_ref, g_ref, wqv_ref, wkt_ref, wg_ref,
                       wgt_ref, bg_ref, bgt_ref,
                       qv_ref, kt_ref, gc_ref, gr_ref):
    x = x_ref[...]
    hx = _modulate(x, g_ref[...], mod_ref[0:1, :], mod_ref[1:2, :]).astype(BF16)
    nt = (((1,), (1,)), ((), ()))

    half = N_GATE // 2
    gc = _dot(hx, wg_ref[...]) + bg_ref[...]
    lane = lax.broadcasted_iota(jnp.int32, (1, GATE_PAD), 1)
    fwd = (lane & (half - 1)) < N_HEADS
    pre, suf = _chunk_scans(_log_sigmoid(gc), 0, SCAN_CHUNK, jnp.add, 0.0)
    b = jnp.where(fwd, pre, suf)
    u = gc - pltpu.roll(b, GATE_PAD - half, axis=1)
    pre, suf = _chunk_scans(u, 0, SCAN_CHUNK, jnp.maximum, -jnp.inf)
    gc_ref[...] = jnp.where(lane < half, jnp.where(fwd, pre, suf), b)

    gr = lax.dot_general(wgt_ref[...], hx, nt, preferred_element_type=F32)
    gr = gr + bgt_ref[...]
    sub = lax.broadcasted_iota(jnp.int32, (N_GATE, 1), 0)
    pre, suf = _chunk_scans(_log_sigmoid(gr), 1, SCAN_CHUNK, jnp.add, 0.0)
    b = jnp.where((sub & (half - 1)) < N_HEADS, pre, suf)
    gr_ref[0:half, :] = gr[0:half] - b[half:]
    gr_ref[half:, :] = b[half:]

    qv_ref[...] = _dot(hx, wqv_ref[...]).astype(BF16)
    kt = lax.dot_general(wkt_ref[...], hx, nt, preferred_element_type=F32)
    kt_ref[...] = (kt * (DK ** -0.5)).astype(BF16)


def _mlstm_proj(x2d, mod_l, g, wts, *, tm, row_fn):
    n, d = x2d.shape
    nq = N_HEADS * DK
    nqv = nq + N_HEADS * DV
    return pl.pallas_call(
        _mlstm_proj_kernel,
        grid=(n // tm,),
        in_specs=[
            pl.BlockSpec((tm, d), lambda i: (i, 0)),
            pl.BlockSpec((None, 6, d), lambda i: (row_fn(i), 0, 0)),
            _const_spec((1, d)),
            _const_spec((d, nqv)),
            _const_spec((nq, d)),
            _const_spec((d, GATE_PAD)),
            _const_spec((N_GATE, d)),
            _const_spec((1, GATE_PAD)),
            _const_spec((N_GATE, 1)),
        ],
        out_specs=[
            pl.BlockSpec((tm, nqv), lambda i: (i, 0)),
            pl.BlockSpec((nq, tm), lambda i: (0, i)),
            pl.BlockSpec((tm, GATE_PAD), lambda i: (i, 0)),
            pl.BlockSpec((N_GATE, tm), lambda i: (0, i)),
        ],
        out_shape=[
            jax.ShapeDtypeStruct((n, nqv), BF16),
            jax.ShapeDtypeStruct((nq, n), BF16),
            jax.ShapeDtypeStruct((n, GATE_PAD), F32),
            jax.ShapeDtypeStruct((N_GATE, n), F32),
        ],
        compiler_params=_params(1),
        name="mlstm_proj",
    )(x2d, mod_l, g, wts["w_qv"], wts["w_kt"], wts["w_g"], wts["w_gt"],
      wts["b_g"], wts["b_gt"])


def _scan_kernel(*refs, rev, has_add, readout):
    it = iter(refs)
    qv_ref, kt_ref, gc_ref, gr_ref, c0_ref, m0_ref = (next(it) for _ in range(6))
    hadd_ref = next(it) if has_add else None
    if readout:
        x_ref, mod_ref, g_ref, wo_ref, gain_ref, wout_ref = (
            next(it) for _ in range(6))
    out_ref, cf_ref, mf_ref = (next(it) for _ in range(3))
    c_s, m_s, a_s, x_s, s_s, w_s, ktw_s = (next(it) for _ in range(7))
    if readout:
        h_s, z_s = next(it), next(it)
    h_ref = h_s if readout else out_ref
    t = pl.program_id(1)
    n_t = pl.num_programs(1)
    L = SCAN_CHUNK
    nq = N_HEADS * DK
    n_chunks = qv_ref.shape[0] // L
    half = N_GATE // 2

    @pl.when(t == 0)
    def _():
        c_s[...] = c0_ref[...]
        m_s[...] = m0_ref[...]

    ti = lax.broadcasted_iota(jnp.int32, (L, L), 0)
    si = lax.broadcasted_iota(jnp.int32, (L, L), 1)
    mask = (si >= ti) if rev else (si <= ti)
    gcol = N_HEADS if rev else 0
    ones = jnp.ones((L, DK), BF16)
    order = range(n_chunks - 1, -1, -1) if rev else range(n_chunks)

    units = [(c, h) for c in order for h in range(N_HEADS)]

    for c, h in units:
        r0, r1 = c * L, (c + 1) * L
        q = qv_ref[r0:r1, h * DK:(h + 1) * DK]
        kt = kt_ref[h * DK:(h + 1) * DK, r0:r1]
        s_s[c, h] = _dot(q, kt)
        gi = gcol + h
        u_row = gr_ref[gi:gi + 1, r0:r1]
        cm_end = jnp.max(u_row, axis=1, keepdims=True)
        ktw_s[c, h] = (kt.astype(F32) * jnp.exp(u_row - cm_end)).astype(BF16)
    for c, h in units:
        r0, r1 = c * L, (c + 1) * L
        gi = gcol + h
        u_row = gr_ref[gi:gi + 1, r0:r1]
        cm_bc = jnp.broadcast_to(gc_ref[r0:r1, gi:gi + 1], (L, DK))
        w0 = jnp.exp(jnp.where(mask, u_row - cm_bc, -jnp.inf)) * s_s[c, h]
        w_s[c, h] = w0.astype(BF16)

    for c in order:
        r0, r1 = c * L, (c + 1) * L
        for h in range(N_HEADS):
            v = qv_ref[r0:r1, nq + h * DV:nq + (h + 1) * DV]
            v_aug = jnp.concatenate([v, ones], axis=1)
            a_s[h] = _dot(w_s[c, h], v_aug)
            x_s[h] = _dot(ktw_s[c, h], v_aug)
        for h in range(N_HEADS):
            gi = gcol + h
            q = qv_ref[r0:r1, h * DK:(h + 1) * DK]
            u_row = gr_ref[gi:gi + 1, r0:r1]
            b_row = gr_ref[half + gi:half + gi + 1, r0:r1]
            g = b_row[:, 0:1] if rev else b_row[:, L - 1:L]
            cm_end = jnp.max(u_row, axis=1, keepdims=True)
            m_prev = m_s[h]
            c_prev = c_s[h]
            a_aug = a_s[h]
            b_aug = _dot(q, c_prev.astype(BF16))
            cm_bc = jnp.broadcast_to(gc_ref[r0:r1, gi:gi + 1], (L, DK))
            b_bc = jnp.broadcast_to(gc_ref[r0:r1, half + gi:half + gi + 1],
                                    (L, DK))
            r = jnp.maximum(m_prev, cm_bc)
            rho = jnp.exp(cm_bc - r)
            wi = jnp.exp(m_prev - r)
            den = rho * a_aug[:, DV:] + wi * b_aug[:, DV:]
            inv = 1.0 / jnp.maximum(jnp.abs(den), jnp.exp(-(b_bc + r)))
            alpha = rho * inv
            beta = wi * inv
            for j in range(DV // DK):
                lo, hi = j * DK, (j + 1) * DK
                cols = slice(h * DV + lo, h * DV + hi)
                hh = alpha * a_aug[:, lo:hi] + beta * b_aug[:, lo:hi]
                if has_add:
                    hh = hh + hadd_ref[r0:r1, cols]
                h_ref[r0:r1, cols] = hh
            mx = jnp.maximum(m_prev, cm_end)
            decay = jnp.exp(m_prev - mx)
            gamma = jnp.exp(cm_end - mx)
            for j in range((DV + DK) // DK):
                lo, hi = j * DK, (j + 1) * DK
                c_s[h, :, lo:hi] = (decay * c_prev[:, lo:hi]
                                    + gamma * x_s[h, :, lo:hi])
            m_s[h] = g + mx

    @pl.when(t == n_t - 1)
    def _():
        cf_ref[...] = c_s[...]
        mf_ref[...] = m_s[...]

    if readout:
        x = x_ref[...]
        hx = _modulate(x, g_ref[...], mod_ref[0:1, :], mod_ref[1:2, :]).astype(BF16)
        for h in range(N_HEADS):
            lo, hi = h * DV, (h + 1) * DV
            hh = h_s[:, lo:hi]
            ms = jnp.mean(hh * hh, axis=-1, keepdims=True)
            hn = hh * lax.rsqrt(ms + EPS) * gain_ref[:, lo:hi]
            o = _dot(hx, wo_ref[:, lo:hi])
            z_s[:, lo:hi] = (hn * _sigmoid(o)).astype(BF16)
        y = _dot(z_s[...], wout_ref[...])
        out_ref[...] = x + mod_ref[2:3, :] * y


def _scan(qv, kt, gc, gr, state, *, batch, ts, rev, add=None, readout=None):
    n = qv.shape[0]
    n_t = n // batch // ts
    n_chunks = ts // SCAN_CHUNK
    nq = N_HEADS * DK
    nv = N_HEADS * DV
    nqv = nq + nv

    def blk(b, t):
        return b * n_t + ((n_t - 1 - t) if rev else t)

    st_shapes = [(N_HEADS, DK, DV + DK), (N_HEADS, 1, DK)]
    st_specs = [pl.BlockSpec((None,) + s, lambda b, t: (b, 0, 0, 0))
                for s in st_shapes]
    h_spec = pl.BlockSpec((ts, nv), lambda b, t: (blk(b, t), 0))
    has_add = add is not None
    in_specs = [
        pl.BlockSpec((ts, nqv), lambda b, t: (blk(b, t), 0)),
        pl.BlockSpec((nq, ts), lambda b, t: (0, blk(b, t))),
        pl.BlockSpec((ts, GATE_PAD), lambda b, t: (blk(b, t), 0)),
        pl.BlockSpec((N_GATE, ts), lambda b, t: (0, blk(b, t))),
    ] + st_specs
    args = [qv, kt, gc, gr, *state]
    scratch = [pltpu.VMEM(s, F32) for s in st_shapes] + [
        pltpu.VMEM((N_HEADS, SCAN_CHUNK, DV + DK), F32),
        pltpu.VMEM((N_HEADS, DK, DV + DK), F32),
        pltpu.VMEM((n_chunks, N_HEADS, SCAN_CHUNK, SCAN_CHUNK), F32),
        pltpu.VMEM((n_chunks, N_HEADS, SCAN_CHUNK, SCAN_CHUNK), BF16),
        pltpu.VMEM((n_chunks, N_HEADS, DK, SCAN_CHUNK), BF16),
    ]
    if has_add:
        in_specs.append(h_spec)
        args.append(add)
    out_cols = nv
    if readout is not None:
        x2d, mod_l, mod_row_fn, g, w_o, gain, w_out = readout
        d = x2d.shape[1]
        out_cols = d
        in_specs += [
            pl.BlockSpec((ts, d), lambda b, t: (blk(b, t), 0)),
            pl.BlockSpec((None, 6, d), lambda b, t: (mod_row_fn(b), 0, 0)),
            _const_spec((1, d)),
            _const_spec((d, nv)),
            _const_spec((1, nv)),
            _const_spec((nv, d)),
        ]
        args += [x2d, mod_l, g, w_o, gain, w_out]
        scratch += [pltpu.VMEM((ts, nv), F32), pltpu.VMEM((ts, nv), BF16)]
    out, cf, mf = pl.pallas_call(
        functools.partial(_scan_kernel, rev=rev, has_add=has_add,
                          readout=readout is not None),
        grid=(batch, n_t),
        in_specs=in_specs,
        out_specs=[pl.BlockSpec((ts, out_cols), lambda b, t: (blk(b, t), 0))]
        + st_specs,
        out_shape=[jax.ShapeDtypeStruct((n, out_cols), F32)]
        + [jax.ShapeDtypeStruct((batch,) + s, F32) for s in st_shapes],
        scratch_shapes=scratch,
        compiler_params=_params(2),
        name="mlstm_scan_bwd" if rev else "mlstm_scan_fwd",
    )(*args)
    return out, (cf, mf)


def _mlstm_weights(w_in, b_gate):
    nq = N_HEADS * DK
    o2 = 2 * nq
    o3 = o2 + N_HEADS * DV
    o4 = o3 + D_MODEL
    w_g = w_in[:, o4:]
    return {
        "w_qv": jnp.concatenate([w_in[:, :nq], w_in[:, o2:o3]], axis=1).astype(BF16),
        "w_kt": w_in[:, nq:o2].T.astype(BF16),
        "w_o": w_in[:, o3:o4].astype(BF16),
        "w_g": jnp.pad(w_g, ((0, 0), (0, GATE_PAD - N_GATE))).astype(BF16),
        "w_gt": w_g.T.astype(BF16),
        "b_g": jnp.pad(b_gate, (0, GATE_PAD - N_GATE)).reshape(1, GATE_PAD),
        "b_gt": b_gate.reshape(N_GATE, 1),
    }


def kernel(x, c, ctx, c_ctx, w_mod, b_mod, g_mix, g_ffn, a_w_in, a_b_gate,
           a_head_gain, a_w_out, b_w_in, b_w_conv, b_w_out, f_w_up, f_w_conv,
           f_b_conv, f_w_down, g_final):
    bn, t, d = x.shape
    n_ctx = ctx.shape[1]
    depth = w_mod.shape[0]
    n_mixers = 2
    assert d == D_MODEL and bn <= 7 and t % TOKEN_TILE == 0
    assert n_ctx % SCAN_CHUNK == 0 and TOKEN_TILE % SCAN_CHUNK == 0

    rec_layers = [i for i in range(depth) if i % n_mixers == 0]
    last_rec = max(rec_layers) if rec_layers else -1

    cc = jnp.zeros((8, d), F32).at[:bn].set(c).at[bn].set(c_ctx)
    mod = _mod_all(cc, w_mod, b_mod).reshape(depth, 8, 6, d)

    tm = TOKEN_TILE
    tiles_per_batch = t // tm
    lat_row = lambda i: i // tiles_per_batch
    ctx_row = lambda i: bn

    xs = x.reshape(bn * t, d)
    cs = ctx.reshape(bn * n_ctx, d)
    zero_state = (jnp.zeros((bn, N_HEADS, DK, DV + DK), F32),
                  jnp.zeros((bn, N_HEADS, 1, DK), F32))

    for i in range(depth):
        j = i // n_mixers
        ctx_read = i <= last_rec
        ctx_live = i < last_rec
        gm = g_mix[i].reshape(1, d)
        gf = g_ffn[i].reshape(1, d)
        if i % n_mixers == 0:
            wts = _mlstm_weights(a_w_in[j], a_b_gate[j])
            w_out = a_w_out[j].astype(BF16)
            gain = a_head_gain[j].reshape(1, N_HEADS * DV)
            s_f = s_b = zero_state
            if ctx_read:
                pc = _mlstm_proj(cs, mod[i], gm, wts, tm=n_ctx, row_fn=ctx_row)
                hc, s_b = _scan(*pc, zero_state, batch=bn, ts=n_ctx, rev=True)
                ro = ((cs, mod[i], lambda b: bn, gm, wts["w_o"], gain, w_out)
                      if ctx_live else None)
                cs_new, s_f = _scan(*pc, zero_state, batch=bn, ts=n_ctx,
                                    rev=False, add=hc, readout=ro)
            px = _mlstm_proj(xs, mod[i], gm, wts, tm=tm, row_fn=lat_row)
            hl, _ = _scan(*px, s_b, batch=bn, ts=tm, rev=True)
            ro = (xs, mod[i], lambda b: b, gm, wts["w_o"], gain, w_out)
            xs, _ = _scan(*px, s_f, batch=bn, ts=tm, rev=False, add=hl,
                          readout=ro)
            if ctx_live:
                cs = cs_new
        else:
            w_in = b_w_in[j].astype(BF16)
            w_out = b_w_out[j].astype(BF16)
            xs = _shortconv(xs, mod[i], gm, w_in, b_w_conv[j], w_out,
                            tm=tm, seg=GRID_W, row_fn=lat_row)
            if ctx_live:
                cs = _shortconv(cs, mod[i], gm, w_in, b_w_conv[j], w_out,
                                tm=n_ctx, seg=n_ctx, row_fn=ctx_row)
        w_up = f_w_up[i].astype(BF16)
        w_down = f_w_down[i].astype(BF16)
        b_conv = f_b_conv[i].reshape(1, D_FF)
        gfin = g_final.reshape(1, d)
        xs = _ffn(xs, mod[i], gf, w_up, f_w_conv[i], b_conv, w_down, gfin,
                  tm=FFN_TILE, halo=GRID_W, seg=t,
                  row_fn=lambda i: i // (t // FFN_TILE),
                  final=(i == depth - 1))
        if ctx_live:
            cs = _ffn(cs, mod[i], gf, w_up, f_w_conv[i], b_conv, w_down, gfin,
                      tm=n_ctx, halo=0, seg=n_ctx, row_fn=ctx_row, final=False)

    return xs.reshape(bn, t, d)
```

```python
import functools

import jax
import jax.numpy as jnp
from jax import lax
from jax.experimental import pallas as pl
from jax.experimental.pallas import tpu as pltpu

F32 = jnp.float32
BF16 = jnp.bfloat16

D_MODEL = 1024
GRID_W = 64
N_HEADS = 4
DK = 128
DV = 256
D_FF = 2816
EPS = 1e-6
N_GATE = 16
GATE_PAD = 128

SCAN_CHUNK = 128
FF_CHUNK = 256
TOKEN_TILE = 512
FFN_STRIP = 8
VMEM_LIMIT = 56 * 1024 * 1024


def _params(n_axes):
    return pltpu.CompilerParams(
        dimension_semantics=("arbitrary",) * n_axes,
        vmem_limit_bytes=VMEM_LIMIT)


def _const_spec(shape):
    zeros = (0,) * len(shape)
    return pl.BlockSpec(shape, lambda *_: zeros, pipeline_mode=pl.Buffered(1))


def _dot(a, b):
    return jnp.dot(a, b, preferred_element_type=F32)


def _sigmoid(v):
    return 1.0 / (1.0 + jnp.exp(-v))


def _rmsnorm(x, g):
    ms = jnp.mean(x * x, axis=-1, keepdims=True)
    return (x * lax.rsqrt(ms + EPS)) * g


def _modulate(x, g, shift, scale):
    return _rmsnorm(x, g) * (1.0 + scale) + shift


def _shift_conv(u, w_ref, seg, shift=1):
    n = u.shape[0]
    if seg == n and shift % 8 == 0:
        pad = jnp.zeros((shift, u.shape[1]), u.dtype)
        prev = jnp.concatenate([pad, u[:n - shift]], axis=0)
        nxt = jnp.concatenate([u[shift:], pad], axis=0)
    else:
        pos = lax.broadcasted_iota(jnp.int32, (n, 1), 0) & (seg - 1)
        prev = jnp.where(pos >= shift, pltpu.roll(u, shift, axis=0), 0.0)
        nxt = jnp.where(pos < seg - shift, pltpu.roll(u, n - shift, axis=0), 0.0)
    return prev * w_ref[0:1, :] + u * w_ref[1:2, :] + nxt * w_ref[2:3, :]


def _mod_kernel(cc_ref, w_ref, b_ref, o_ref):
    cc = cc_ref[...]
    s = cc * _sigmoid(cc)
    o_ref[...] = _dot(s, w_ref[...]) + b_ref[...]


def _mod_all(cc, w_mod, b_mod):
    depth, d, n = w_mod.shape
    tn = 1536
    return pl.pallas_call(
        _mod_kernel,
        grid=(depth, n // tn),
        in_specs=[
            pl.BlockSpec((8, d), lambda l, j: (0, 0)),
            pl.BlockSpec((None, d, tn), lambda l, j: (l, 0, j)),
            pl.BlockSpec((None, 1, tn), lambda l, j: (l, 0, j)),
        ],
        out_specs=pl.BlockSpec((None, 8, tn), lambda l, j: (l, 0, j)),
        out_shape=jax.ShapeDtypeStruct((depth, 8, n), F32),
        compiler_params=_params(2),
        name="adaln_mod",
    )(cc, w_mod, b_mod.reshape(depth, 1, n))


def _shortconv_kernel(x_ref, mod_ref, g_ref, win_ref, wconv_ref, wout_ref,
                      o_ref, z_ref, *, seg):
    d = D_MODEL
    x = x_ref[...]
    hx = _modulate(x, g_ref[...], mod_ref[0:1, :], mod_ref[1:2, :]).astype(BF16)
    cw = 256
    for j in range(d // cw):
        lo, hi = j * cw, (j + 1) * cw
        bg = _dot(hx, win_ref[:, lo:hi])
        cg = _dot(hx, win_ref[:, d + lo:d + hi])
        xv = _dot(hx, win_ref[:, 2 * d + lo:2 * d + hi])
        cv = _shift_conv(cg * xv, wconv_ref.at[:, lo:hi], seg)
        z_ref[:, lo:hi] = (bg * cv).astype(BF16)
    y = _dot(z_ref[...], wout_ref[...])
    o_ref[...] = x + mod_ref[2:3, :] * y


def _shortconv(x2d, mod_l, g, w_in, w_conv, w_out, *, tm, seg, row_fn):
    n, d = x2d.shape
    return pl.pallas_call(
        functools.partial(_shortconv_kernel, seg=seg),
        grid=(n // tm,),
        in_specs=[
            pl.BlockSpec((tm, d), lambda i: (i, 0)),
            pl.BlockSpec((None, 6, d), lambda i: (row_fn(i), 0, 0)),
            _const_spec((1, d)),
            _const_spec((d, 3 * d)),
            _const_spec((3, d)),
            _const_spec((d, d)),
        ],
        out_specs=pl.BlockSpec((tm, d), lambda i: (i, 0)),
        out_shape=jax.ShapeDtypeStruct((n, d), F32),
        scratch_shapes=[pltpu.VMEM((tm, d), BF16)],
        compiler_params=_params(1),
        name="shortconv_mixer",
    )(x2d, mod_l, g, w_in, w_conv, w_out)


def _ffn_kernel(x_ref, mod_ref, g_ref, wup_ref, wconv_ref, bconv_ref, wdown_ref,
                gfin_ref, o_ref, act_ref, *, shift, seg, final):
    d = x_ref.shape[-1]
    x = x_ref[...].reshape(-1, d)
    hx = _modulate(x, g_ref[...], mod_ref[3:4, :], mod_ref[4:5, :]).astype(BF16)
    for c in range(D_FF // FF_CHUNK):
        lo, hi = c * FF_CHUNK, (c + 1) * FF_CHUNK
        val = _dot(hx, wup_ref[:, D_FF + lo:D_FF + hi])
        gate = _shift_conv(_dot(hx, wup_ref[:, lo:hi]), wconv_ref.at[:, lo:hi],
                           seg, shift)
        gate = gate + bconv_ref[:, lo:hi]
        act_ref[:, lo:hi] = (gate * _sigmoid(gate) * val).astype(BF16)
    y = _dot(act_ref[...], wdown_ref[...])
    out = x + mod_ref[5:6, :] * y
    if final:
        out = _rmsnorm(out, gfin_ref[...])
    o_ref[...] = out.reshape(o_ref.shape)


def _ffn(x2d, mod_l, g, w_up, w_conv, b_conv, w_down, g_final, *, batch,
         strip, final):
    n, d = x2d.shape
    per = n // batch
    if strip:
        rows = per // GRID_W
        strips = GRID_W // strip
        tm = rows * strip
        xin = x2d.reshape(batch * rows, GRID_W, d)
        x_spec = pl.BlockSpec((rows, strip, d),
                              lambda i: (i // strips, i % strips, 0))
        grid = (batch * strips,)
        row_fn = lambda i: i // strips
        shift = strip
    else:
        tm = per
        xin = x2d
        x_spec = pl.BlockSpec((tm, d), lambda i: (i, 0))
        grid = (batch,)
        row_fn = lambda i: batch
        shift = 1
    out = pl.pallas_call(
        functools.partial(_ffn_kernel, shift=shift, seg=tm, final=final),
        grid=grid,
        in_specs=[
            x_spec,
            pl.BlockSpec((None, 6, d), lambda i: (row_fn(i), 0, 0)),
            _const_spec((1, d)),
            _const_spec((d, 2 * D_FF)),
            _const_spec((3, D_FF)),
            _const_spec((1, D_FF)),
            _const_spec((D_FF, d)),
            _const_spec((1, d)),
        ],
        out_specs=x_spec,
        out_shape=jax.ShapeDtypeStruct(xin.shape, F32),
        scratch_shapes=[pltpu.VMEM((tm, D_FF), BF16)],
        compiler_params=_params(1),
        name="convglu_ffn",
    )(xin, mod_l, g, w_up, w_conv, b_conv, w_down, g_final)
    return out.reshape(n, d)


def _log_sigmoid(v):
    return -(jnp.maximum(-v, 0.0) + jnp.log(1.0 + jnp.exp(-jnp.abs(v))))


def _chunk_scans(v, axis, chunk, op, identity):
    n = v.shape[axis]
    shape = (n, 1) if axis == 0 else (1, n)
    pos = lax.broadcasted_iota(jnp.int32, shape, axis) & (chunk - 1)
    pre, suf = v, v
    step = 1
    while step < chunk:
        pre = op(pre, jnp.where(pos >= step,
                                pltpu.roll(pre, step, axis=axis), identity))
        suf = op(suf, jnp.where(pos < chunk - step,
                                pltpu.roll(suf, n - step, axis=axis), identity))
        step *= 2
    return pre, suf


def _mlstm_proj_kernel(x_ref, mod_ref, g_ref, wqv_ref, wkt_ref, wg_ref,
                       wgt_ref, bg_ref, bgt_ref,
                       qv_ref, kt_ref, gc_ref, gr_ref):
    x = x_ref[...]
    hx = _modulate(x, g_ref[...], mod_ref[0:1, :], mod_ref[1:2, :]).astype(BF16)
    nt = (((1,), (1,)), ((), ()))

    half = N_GATE // 2
    gc = _dot(hx, wg_ref[...]) + bg_ref[...]
    lane = lax.broadcasted_iota(jnp.int32, (1, GATE_PAD), 1)
    fwd = (lane & (half - 1)) < N_HEADS
    pre, suf = _chunk_scans(_log_sigmoid(gc), 0, SCAN_CHUNK, jnp.add, 0.0)
    b = jnp.where(fwd, pre, suf)
    u = gc - pltpu.roll(b, GATE_PAD - half, axis=1)
    pre, suf = _chunk_scans(u, 0, SCAN_CHUNK, jnp.maximum, -jnp.inf)
    gc_ref[...] = jnp.where(lane < half, jnp.where(fwd, pre, suf), b)

    gr = lax.dot_general(wgt_ref[...], hx, nt, preferred_element_type=F32)
    gr = gr + bgt_ref[...]
    sub = lax.broadcasted_iota(jnp.int32, (N_GATE, 1), 0)
    pre, suf = _chunk_scans(_log_sigmoid(gr), 1, SCAN_CHUNK, jnp.add, 0.0)
    b = jnp.where((sub & (half - 1)) < N_HEADS, pre, suf)
    gr_ref[0:half, :] = gr[0:half] - b[half:]
    gr_ref[half:, :] = b[half:]

    qv_ref[...] = _dot(hx, wqv_ref[...]).astype(BF16)
    kt = lax.dot_general(wkt_ref[...], hx, nt, preferred_element_type=F32)
    kt_ref[...] = (kt * (DK ** -0.5)).astype(BF16)


def _mlstm_proj(x2d, mod_l, g, wts, *, tm, row_fn):
    n, d = x2d.shape
    nq = N_HEADS * DK
    nqv = nq + N_HEADS * DV
    return pl.pallas_call(
        _mlstm_proj_kernel,
        grid=(n // tm,),
        in_specs=[
            pl.BlockSpec((tm, d), lambda i: (i, 0)),
            pl.BlockSpec((None, 6, d), lambda i: (row_fn(i), 0, 0)),
            _const_spec((1, d)),
            _const_spec((d, nqv)),
            _const_spec((nq, d)),
            _const_spec((d, GATE_PAD)),
            _const_spec((N_GATE, d)),
            _const_spec((1, GATE_PAD)),
            _const_spec((N_GATE, 1)),
        ],
        out_specs=[
            pl.BlockSpec((tm, nqv), lambda i: (i, 0)),
            pl.BlockSpec((nq, tm), lambda i: (0, i)),
            pl.BlockSpec((tm, GATE_PAD), lambda i: (i, 0)),
            pl.BlockSpec((N_GATE, tm), lambda i: (0, i)),
        ],
        out_shape=[
            jax.ShapeDtypeStruct((n, nqv), BF16),
            jax.ShapeDtypeStruct((nq, n), BF16),
            jax.ShapeDtypeStruct((n, GATE_PAD), F32),
            jax.ShapeDtypeStruct((N_GATE, n), F32),
        ],
        compiler_params=_params(1),
        name="mlstm_proj",
    )(x2d, mod_l, g, wts["w_qv"], wts["w_kt"], wts["w_g"], wts["w_gt"],
      wts["b_g"], wts["b_gt"])


def _scan_kernel(*refs, rev, has_add, readout):
    it = iter(refs)
    qv_ref, kt_ref, gc_ref, gr_ref, c0_ref, m0_ref = (next(it) for _ in range(6))
    hadd_ref = next(it) if has_add else None
    if readout:
        x_ref, mod_ref, g_ref, wo_ref, gain_ref, wout_ref = (
            next(it) for _ in range(6))
    out_ref, cf_ref, mf_ref = (next(it) for _ in range(3))
    c_s, m_s, a_s, x_s, s_s, w_s, ktw_s = (next(it) for _ in range(7))
    if readout:
        h_s, z_s = next(it), next(it)
    h_ref = h_s if readout else out_ref
    t = pl.program_id(1)
    n_t = pl.num_programs(1)
    L = SCAN_CHUNK
    nq = N_HEADS * DK
    n_chunks = qv_ref.shape[0] // L
    half = N_GATE // 2

    @pl.when(t == 0)
    def _():
        c_s[...] = c0_ref[...]
        m_s[...] = m0_ref[...]

    ti = lax.broadcasted_iota(jnp.int32, (L, L), 0)
    si = lax.broadcasted_iota(jnp.int32, (L, L), 1)
    mask = (si >= ti) if rev else (si <= ti)
    gcol = N_HEADS if rev else 0
    ones = jnp.ones((L, DK), BF16)
    order = range(n_chunks - 1, -1, -1) if rev else range(n_chunks)

    units = [(c, h) for c in order for h in range(N_HEADS)]

    for c, h in units:
        r0, r1 = c * L, (c + 1) * L
        q = qv_ref[r0:r1, h * DK:(h + 1) * DK]
        kt = kt_ref[h * DK:(h + 1) * DK, r0:r1]
        s_s[c, h] = _dot(q, kt)
        gi = gcol + h
        u_row = gr_ref[gi:gi + 1, r0:r1]
        cm_end = jnp.max(u_row, axis=1, keepdims=True)
        ktw_s[c, h] = (kt.astype(F32) * jnp.exp(u_row - cm_end)).astype(BF16)
    for c, h in units:
        r0, r1 = c * L, (c + 1) * L
        gi = gcol + h
        u_row = gr_ref[gi:gi + 1, r0:r1]
        cm_bc = jnp.broadcast_to(gc_ref[r0:r1, gi:gi + 1], (L, DK))
        w0 = jnp.exp(jnp.where(mask, u_row - cm_bc, -jnp.inf)) * s_s[c, h]
        w_s[c, h] = w0.astype(BF16)

    for c in order:
        r0, r1 = c * L, (c + 1) * L
        for h in range(N_HEADS):
            v = qv_ref[r0:r1, nq + h * DV:nq + (h + 1) * DV]
            v_aug = jnp.concatenate([v, ones], axis=1)
            a_s[h] = _dot(w_s[c, h], v_aug)
            x_s[h] = _dot(ktw_s[c, h], v_aug)
        for h in range(N_HEADS):
            gi = gcol + h
            q = qv_ref[r0:r1, h * DK:(h + 1) * DK]
            u_row = gr_ref[gi:gi + 1, r0:r1]
            b_row = gr_ref[half + gi:half + gi + 1, r0:r1]
            g = b_row[:, 0:1] if rev else b_row[:, L - 1:L]
            cm_end = jnp.max(u_row, axis=1, keepdims=True)
            m_prev = m_s[h]
            c_prev = c_s[h]
            a_aug = a_s[h]
            b_aug = _dot(q, c_prev.astype(BF16))
            cm_bc = jnp.broadcast_to(gc_ref[r0:r1, gi:gi + 1], (L, DK))
            b_bc = jnp.broadcast_to(gc_ref[r0:r1, half + gi:half + gi + 1],
                                    (L, DK))
            r = jnp.maximum(m_prev, cm_bc)
            rho = jnp.exp(cm_bc - r)
            wi = jnp.exp(m_prev - r)
            den = rho * a_aug[:, DV:] + wi * b_aug[:, DV:]
            inv = 1.0 / jnp.maximum(jnp.abs(den), jnp.exp(-(b_bc + r)))
            alpha = rho * inv
            beta = wi * inv
            for j in range(DV // DK):
                lo, hi = j * DK, (j + 1) * DK
                cols = slice(h * DV + lo, h * DV + hi)
                hh = alpha * a_aug[:, lo:hi] + beta * b_aug[:, lo:hi]
                if has_add:
                    hh = hh + hadd_ref[r0:r1, cols]
                h_ref[r0:r1, cols] = hh
            mx = jnp.maximum(m_prev, cm_end)
            decay = jnp.exp(m_prev - mx)
            gamma = jnp.exp(cm_end - mx)
            for j in range((DV + DK) // DK):
                lo, hi = j * DK, (j + 1) * DK
                c_s[h, :, lo:hi] = (decay * c_prev[:, lo:hi]
                                    + gamma * x_s[h, :, lo:hi])
            m_s[h] = g + mx

    @pl.when(t == n_t - 1)
    def _():
        cf_ref[...] = c_s[...]
        mf_ref[...] = m_s[...]

    if readout:
        x = x_ref[...]
        hx = _modulate(x, g_ref[...], mod_ref[0:1, :], mod_ref[1:2, :]).astype(BF16)
        for h in range(N_HEADS):
            lo, hi = h * DV, (h + 1) * DV
            hh = h_s[:, lo:hi]
            ms = jnp.mean(hh * hh, axis=-1, keepdims=True)
            hn = hh * lax.rsqrt(ms + EPS) * gain_ref[:, lo:hi]
            o = _dot(hx, wo_ref[:, lo:hi])
            z_s[:, lo:hi] = (hn * _sigmoid(o)).astype(BF16)
        y = _dot(z_s[...], wout_ref[...])
        out_ref[...] = x + mod_ref[2:3, :] * y


def _scan(qv, kt, gc, gr, state, *, batch, ts, rev, add=None, readout=None):
    n = qv.shape[0]
    n_t = n // batch // ts
    n_chunks = ts // SCAN_CHUNK
    nq = N_HEADS * DK
    nv = N_HEADS * DV
    nqv = nq + nv

    def blk(b, t):
        return b * n_t + ((n_t - 1 - t) if rev else t)

    st_shapes = [(N_HEADS, DK, DV + DK), (N_HEADS, 1, DK)]
    st_specs = [pl.BlockSpec((None,) + s, lambda b, t: (b, 0, 0, 0))
                for s in st_shapes]
    h_spec = pl.BlockSpec((ts, nv), lambda b, t: (blk(b, t), 0))
    has_add = add is not None
    in_specs = [
        pl.BlockSpec((ts, nqv), lambda b, t: (blk(b, t), 0)),
        pl.BlockSpec((nq, ts), lambda b, t: (0, blk(b, t))),
        pl.BlockSpec((ts, GATE_PAD), lambda b, t: (blk(b, t), 0)),
        pl.BlockSpec((N_GATE, ts), lambda b, t: (0, blk(b, t))),
    ] + st_specs
    args = [qv, kt, gc, gr, *state]
    scratch = [pltpu.VMEM(s, F32) for s in st_shapes] + [
        pltpu.VMEM((N_HEADS, SCAN_CHUNK, DV + DK), F32),
        pltpu.VMEM((N_HEADS, DK, DV + DK), F32),
        pltpu.VMEM((n_chunks, N_HEADS, SCAN_CHUNK, SCAN_CHUNK), F32),
        pltpu.VMEM((n_chunks, N_HEADS, SCAN_CHUNK, SCAN_CHUNK), BF16),
        pltpu.VMEM((n_chunks, N_HEADS, DK, SCAN_CHUNK), BF16),
    ]
    if has_add:
        in_specs.append(h_spec)
        args.append(add)
    out_cols = nv
    if readout is not None:
        x2d, mod_l, mod_row_fn, g, w_o, gain, w_out = readout
        d = x2d.shape[1]
        out_cols = d
        in_specs += [
            pl.BlockSpec((ts, d), lambda b, t: (blk(b, t), 0)),
            pl.BlockSpec((None, 6, d), lambda b, t: (mod_row_fn(b), 0, 0)),
            _const_spec((1, d)),
            _const_spec((d, nv)),
            _const_spec((1, nv)),
            _const_spec((nv, d)),
        ]
        args += [x2d, mod_l, g, w_o, gain, w_out]
        scratch += [pltpu.VMEM((ts, nv), F32), pltpu.VMEM((ts, nv), BF16)]
    out, cf, mf = pl.pallas_call(
        functools.partial(_scan_kernel, rev=rev, has_add=has_add,
                          readout=readout is not None),
        grid=(batch, n_t),
        in_specs=in_specs,
        out_specs=[pl.BlockSpec((ts, out_cols), lambda b, t: (blk(b, t), 0))]
        + st_specs,
        out_shape=[jax.ShapeDtypeStruct((n, out_cols), F32)]
        + [jax.ShapeDtypeStruct((batch,) + s, F32) for s in st_shapes],
        scratch_shapes=scratch,
        compiler_params=_params(2),
        name="mlstm_scan_bwd" if rev else "mlstm_scan_fwd",
    )(*args)
    return out, (cf, mf)


def _mlstm_weights(w_in, b_gate):
    nq = N_HEADS * DK
    o2 = 2 * nq
    o3 = o2 + N_HEADS * DV
    o4 = o3 + D_MODEL
    w_g = w_in[:, o4:]
    return {
        "w_qv": jnp.concatenate([w_in[:, :nq], w_in[:, o2:o3]], axis=1).astype(BF16),
        "w_kt": w_in[:, nq:o2].T.astype(BF16),
        "w_o": w_in[:, o3:o4].astype(BF16),
        "w_g": jnp.pad(w_g, ((0, 0), (0, GATE_PAD - N_GATE))).astype(BF16),
        "w_gt": w_g.T.astype(BF16),
        "b_g": jnp.pad(b_gate, (0, GATE_PAD - N_GATE)).reshape(1, GATE_PAD),
        "b_gt": b_gate.reshape(N_GATE, 1),
    }


def kernel(x, c, ctx, c_ctx, w_mod, b_mod, g_mix, g_ffn, a_w_in, a_b_gate,
           a_head_gain, a_w_out, b_w_in, b_w_conv, b_w_out, f_w_up, f_w_conv,
           f_b_conv, f_w_down, g_final):
    bn, t, d = x.shape
    n_ctx = ctx.shape[1]
    depth = w_mod.shape[0]
    n_mixers = 2
    assert d == D_MODEL and bn <= 7 and t % TOKEN_TILE == 0
    assert n_ctx % SCAN_CHUNK == 0 and TOKEN_TILE % SCAN_CHUNK == 0

    rec_layers = [i for i in range(depth) if i % n_mixers == 0]
    last_rec = max(rec_layers) if rec_layers else -1

    cc = jnp.zeros((8, d), F32).at[:bn].set(c).at[bn].set(c_ctx)
    mod = _mod_all(cc, w_mod, b_mod).reshape(depth, 8, 6, d)

    tm = TOKEN_TILE
    tiles_per_batch = t // tm
    lat_row = lambda i: i // tiles_per_batch
    ctx_row = lambda i: bn

    xs = x.reshape(bn * t, d)
    cs = ctx.reshape(bn * n_ctx, d)
    zero_state = (jnp.zeros((bn, N_HEADS, DK, DV + DK), F32),
                  jnp.zeros((bn, N_HEADS, 1, DK), F32))

    for i in range(depth):
        j = i // n_mixers
        ctx_read = i <= last_rec
        ctx_live = i < last_rec
        gm = g_mix[i].reshape(1, d)
        gf = g_ffn[i].reshape(1, d)
        if i % n_mixers == 0:
            wts = _mlstm_weights(a_w_in[j], a_b_gate[j])
            w_out = a_w_out[j].astype(BF16)
            gain = a_head_gain[j].reshape(1, N_HEADS * DV)
            s_f = s_b = zero_state
            if ctx_read:
                pc = _mlstm_proj(cs, mod[i], gm, wts, tm=n_ctx, row_fn=ctx_row)
                hc, s_b = _scan(*pc, zero_state, batch=bn, ts=n_ctx, rev=True)
                ro = ((cs, mod[i], lambda b: bn, gm, wts["w_o"], gain, w_out)
                      if ctx_live else None)
                cs_new, s_f = _scan(*pc, zero_state, batch=bn, ts=n_ctx,
                                    rev=False, add=hc, readout=ro)
            px = _mlstm_proj(xs, mod[i], gm, wts, tm=tm, row_fn=lat_row)
            hl, _ = _scan(*px, s_b, batch=bn, ts=tm, rev=True)
            ro = (xs, mod[i], lambda b: b, gm, wts["w_o"], gain, w_out)
            xs, _ = _scan(*px, s_f, batch=bn, ts=tm, rev=False, add=hl,
                          readout=ro)
            if ctx_live:
                cs = cs_new
        else:
            w_in = b_w_in[j].astype(BF16)
            w_out = b_w_out[j].astype(BF16)
            xs = _shortconv(xs, mod[i], gm, w_in, b_w_conv[j], w_out,
                            tm=tm, seg=GRID_W, row_fn=lat_row)
            if ctx_live:
                cs = _shortconv(cs, mod[i], gm, w_in, b_w_conv[j], w_out,
                                tm=n_ctx, seg=n_ctx, row_fn=ctx_row)
        w_up = f_w_up[i].astype(BF16)
        w_down = f_w_down[i].astype(BF16)
        b_conv = f_b_conv[i].reshape(1, D_FF)
        gfin = g_final.reshape(1, d)
        xs = _ffn(xs, mod[i], gf, w_up, f_w_conv[i], b_conv, w_down, gfin,
                  batch=bn, strip=FFN_STRIP, final=(i == depth - 1))
        if ctx_live:
            cs = _ffn(cs, mod[i], gf, w_up, f_w_conv[i], b_conv, w_down, gfin,
                      batch=bn, strip=0, final=False)

    return xs.reshape(bn, t, d)
```

```python
import functools

import jax
import jax.numpy as jnp
from jax import lax
from jax.experimental import pallas as pl
from jax.experimental.pallas import tpu as pltpu

F32 = jnp.float32
BF16 = jnp.bfloat16

D_MODEL = 1024
GRID_W = 64
N_HEADS = 4
DK = 128
DV = 256
D_FF = 2816
EPS = 1e-6
N_GATE = 16
GATE_PAD = 128

SCAN_CHUNK = 128
FF_CHUNK = 256
DOWN_PIECES = 2
TOKEN_TILE = 1024
FFN_STRIP = 8
VMEM_LIMIT = 56 * 1024 * 1024


def _params(n_axes):
    return pltpu.CompilerParams(
        dimension_semantics=("arbitrary",) * n_axes,
        vmem_limit_bytes=VMEM_LIMIT)


def _const_spec(shape):
    zeros = (0,) * len(shape)
    return pl.BlockSpec(shape, lambda *_: zeros, pipeline_mode=pl.Buffered(1))


def _dot(a, b):
    return jnp.dot(a, b, preferred_element_type=F32)


def _sigmoid(v):
    return 1.0 / (1.0 + jnp.exp(-v))


def _rmsnorm(x, g):
    ms = jnp.mean(x * x, axis=-1, keepdims=True)
    return (x * lax.rsqrt(ms + EPS)) * g


def _modulate(x, g, shift, scale):
    return _rmsnorm(x, g) * (1.0 + scale) + shift


def _shift_conv(u, w_ref, seg, shift=1):
    n = u.shape[0]
    if seg == n and shift % 8 == 0:
        pad = jnp.zeros((shift, u.shape[1]), u.dtype)
        prev = jnp.concatenate([pad, u[:n - shift]], axis=0)
        nxt = jnp.concatenate([u[shift:], pad], axis=0)
    else:
        pos = lax.broadcasted_iota(jnp.int32, (n, 1), 0) & (seg - 1)
        prev = jnp.where(pos >= shift, pltpu.roll(u, shift, axis=0), 0.0)
        nxt = jnp.where(pos < seg - shift, pltpu.roll(u, n - shift, axis=0), 0.0)
    return prev * w_ref[0:1, :] + u * w_ref[1:2, :] + nxt * w_ref[2:3, :]


def _mod_kernel(cc_ref, w_ref, b_ref, o_ref):
    cc = cc_ref[...]
    s = cc * _sigmoid(cc)
    o_ref[...] = _dot(s, w_ref[...]) + b_ref[...]


def _mod_all(cc, w_mod, b_mod):
    depth, d, n = w_mod.shape
    tn = 1536
    return pl.pallas_call(
        _mod_kernel,
        grid=(depth, n // tn),
        in_specs=[
            pl.BlockSpec((8, d), lambda l, j: (0, 0)),
            pl.BlockSpec((None, d, tn), lambda l, j: (l, 0, j)),
            pl.BlockSpec((None, 1, tn), lambda l, j: (l, 0, j)),
        ],
        out_specs=pl.BlockSpec((None, 8, tn), lambda l, j: (l, 0, j)),
        out_shape=jax.ShapeDtypeStruct((depth, 8, n), F32),
        compiler_params=_params(2),
        name="adaln_mod",
    )(cc, w_mod, b_mod.reshape(depth, 1, n))


def _shortconv_kernel(x_ref, mod_ref, g_ref, win_ref, wconv_ref, wout_ref,
                      o_ref, z_ref, *, seg):
    d = D_MODEL
    x = x_ref[...]
    hx = _modulate(x, g_ref[...], mod_ref[0:1, :], mod_ref[1:2, :]).astype(BF16)
    cw = 256
    for j in range(d // cw):
        lo, hi = j * cw, (j + 1) * cw
        bg = _dot(hx, win_ref[:, lo:hi])
        cg = _dot(hx, win_ref[:, d + lo:d + hi])
        xv = _dot(hx, win_ref[:, 2 * d + lo:2 * d + hi])
        cv = _shift_conv(cg * xv, wconv_ref.at[:, lo:hi], seg)
        z_ref[:, lo:hi] = (bg * cv).astype(BF16)
    y = _dot(z_ref[...], wout_ref[...])
    o_ref[...] = x + mod_ref[2:3, :] * y


def _shortconv(x2d, mod_l, g, w_in, w_conv, w_out, *, tm, seg, row_fn):
    n, d = x2d.shape
    return pl.pallas_call(
        functools.partial(_shortconv_kernel, seg=seg),
        grid=(n // tm,),
        in_specs=[
            pl.BlockSpec((tm, d), lambda i: (i, 0)),
            pl.BlockSpec((None, 6, d), lambda i: (row_fn(i), 0, 0)),
            _const_spec((1, d)),
            _const_spec((d, 3 * d)),
            _const_spec((3, d)),
            _const_spec((d, d)),
        ],
        out_specs=pl.BlockSpec((tm, d), lambda i: (i, 0)),
        out_shape=jax.ShapeDtypeStruct((n, d), F32),
        scratch_shapes=[pltpu.VMEM((tm, d), BF16)],
        compiler_params=_params(1),
        name="shortconv_mixer",
    )(x2d, mod_l, g, w_in, w_conv, w_out)


def _ffn_kernel(x_ref, mod_ref, g_ref, wup_ref, wconv_ref, bconv_ref, wdown_ref,
                gfin_ref, o_ref, act_ref, *, shift, seg, final):
    d = x_ref.shape[-1]
    x = x_ref[...].reshape(-1, d)
    hx = _modulate(x, g_ref[...], mod_ref[3:4, :], mod_ref[4:5, :]).astype(BF16)
    for c in range(D_FF // FF_CHUNK):
        lo, hi = c * FF_CHUNK, (c + 1) * FF_CHUNK
        val = _dot(hx, wup_ref[:, D_FF + lo:D_FF + hi])
        gate = _shift_conv(_dot(hx, wup_ref[:, lo:hi]), wconv_ref.at[:, lo:hi],
                           seg, shift)
        gate = gate + bconv_ref[:, lo:hi]
        act_ref[:, lo:hi] = (gate * _sigmoid(gate) * val).astype(BF16)
    tm = x.shape[0]
    pr = tm // DOWN_PIECES
    lead = o_ref.shape[0] // DOWN_PIECES
    for p in range(DOWN_PIECES):
        rows = slice(p * pr, (p + 1) * pr)
        y = _dot(act_ref[rows, :], wdown_ref[...])
        out = x[rows] + mod_ref[5:6, :] * y
        if final:
            out = _rmsnorm(out, gfin_ref[...])
        o_ref[p * lead:(p + 1) * lead] = out.reshape((lead,) + o_ref.shape[1:])


def _ffn(x2d, mod_l, g, w_up, w_conv, b_conv, w_down, g_final, *, batch,
         strip, final):
    n, d = x2d.shape
    per = n // batch
    if strip:
        rows = per // GRID_W
        strips = GRID_W // strip
        tm = rows * strip
        xin = x2d.reshape(batch * rows, GRID_W, d)
        x_spec = pl.BlockSpec((rows, strip, d),
                              lambda i: (i // strips, i % strips, 0))
        grid = (batch * strips,)
        row_fn = lambda i: i // strips
        shift = strip
    else:
        tm = per
        xin = x2d
        x_spec = pl.BlockSpec((tm, d), lambda i: (i, 0))
        grid = (batch,)
        row_fn = lambda i: batch
        shift = 1
    out = pl.pallas_call(
        functools.partial(_ffn_kernel, shift=shift, seg=tm, final=final),
        grid=grid,
        in_specs=[
            x_spec,
            pl.BlockSpec((None, 6, d), lambda i: (row_fn(i), 0, 0)),
            _const_spec((1, d)),
            _const_spec((d, 2 * D_FF)),
            _const_spec((3, D_FF)),
            _const_spec((1, D_FF)),
            _const_spec((D_FF, d)),
            _const_spec((1, d)),
        ],
        out_specs=x_spec,
        out_shape=jax.ShapeDtypeStruct(xin.shape, F32),
        scratch_shapes=[pltpu.VMEM((tm, D_FF), BF16)],
        compiler_params=_params(1),
        name="convglu_ffn",
    )(xin, mod_l, g, w_up, w_conv, b_conv, w_down, g_final)
    return out.reshape(n, d)


def _log_sigmoid(v):
    return -(jnp.maximum(-v, 0.0) + jnp.log(1.0 + jnp.exp(-jnp.abs(v))))


def _chunk_scans(v, axis, chunk, op, identity):
    n = v.shape[axis]
    shape = (n, 1) if axis == 0 else (1, n)
    pos = lax.broadcasted_iota(jnp.int32, shape, axis) & (chunk - 1)
    pre, suf = v, v
    step = 1
    while step < chunk:
        pre = op(pre, jnp.where(pos >= step,
                                pltpu.roll(pre, step, axis=axis), identity))
        suf = op(suf, jnp.where(pos < chunk - step,
                                pltpu.roll(suf, n - step, axis=axis), identity))
        step *= 2
    return pre, suf


def _mlstm_proj_kernel(x_ref, mod_ref, g_ref, wqv_ref, wkt_ref, wg_ref,
                       wgt_ref, bg_ref, bgt_ref,
                       qv_ref, kt_ref, gc_ref, gr_ref):
    x = x_ref[...]
    hx = _modulate(x, g_ref[...], mod_ref[0:1, :], mod_ref[1:2, :]).astype(BF16)
    nt = (((1,), (1,)), ((), ()))

    half = N_GATE // 2
    gc = _dot(hx, wg_ref[...]) + bg_ref[...]
    lane = lax.broadcasted_iota(jnp.int32, (1, GATE_PAD), 1)
    fwd = (lane & (half - 1)) < N_HEADS
    pre, suf = _chunk_scans(_log_sigmoid(gc), 0, SCAN_CHUNK, jnp.add, 0.0)
    b = jnp.where(fwd, pre, suf)
    u = gc - pltpu.roll(b, GATE_PAD - half, axis=1)
    pre, suf = _chunk_scans(u, 0, SCAN_CHUNK, jnp.maximum, -jnp.inf)
    gc_ref[...] = jnp.where(lane < half, jnp.where(fwd, pre, suf), b)

    gr = lax.dot_general(wgt_ref[...], hx, nt, preferred_element_type=F32)
    gr = gr + bgt_ref[...]
    sub = lax.broadcasted_iota(jnp.int32, (N_GATE, 1), 0)
    pre, suf = _chunk_scans(_log_sigmoid(gr), 1, SCAN_CHUNK, jnp.add, 0.0)
    b = jnp.where((sub & (half - 1)) < N_HEADS, pre, suf)
    gr_ref[0:half, :] = gr[0:half] - b[half:]
    gr_ref[half:, :] = b[half:]

    qv_ref[...] = _dot(hx, wqv_ref[...]).astype(BF16)
    kt = lax.dot_general(wkt_ref[...], hx, nt, preferred_element_type=F32)
    kt_ref[...] = (kt * (DK ** -0.5)).astype(BF16)


def _mlstm_proj(x2d, mod_l, g, wts, *, tm, row_fn):
    n, d = x2d.shape
    nq = N_HEADS * DK
    nqv = nq + N_HEADS * DV
    return pl.pallas_call(
        _mlstm_proj_kernel,
        grid=(n // tm,),
        in_specs=[
            pl.BlockSpec((tm, d), lambda i: (i, 0)),
            pl.BlockSpec((None, 6, d), lambda i: (row_fn(i), 0, 0)),
            _const_spec((1, d)),
            _const_spec((d, nqv)),
            _const_spec((nq, d)),
            _const_spec((d, GATE_PAD)),
            _const_spec((N_GATE, d)),
            _const_spec((1, GATE_PAD)),
            _const_spec((N_GATE, 1)),
        ],
        out_specs=[
            pl.BlockSpec((tm, nqv), lambda i: (i, 0)),
            pl.BlockSpec((nq, tm), lambda i: (0, i)),
            pl.BlockSpec((tm, GATE_PAD), lambda i: (i, 0)),
            pl.BlockSpec((N_GATE, tm), lambda i: (0, i)),
        ],
        out_shape=[
            jax.ShapeDtypeStruct((n, nqv), BF16),
            jax.ShapeDtypeStruct((nq, n), BF16),
            jax.ShapeDtypeStruct((n, GATE_PAD), F32),
            jax.ShapeDtypeStruct((N_GATE, n), F32),
        ],
        compiler_params=_params(1),
        name="mlstm_proj",
    )(x2d, mod_l, g, wts["w_qv"], wts["w_kt"], wts["w_g"], wts["w_gt"],
      wts["b_g"], wts["b_gt"])


def _scan_kernel(*refs, rev, has_add, readout):
    it = iter(refs)
    qv_ref, kt_ref, gc_ref, gr_ref, c0_ref, m0_ref = (next(it) for _ in range(6))
    hadd_ref = next(it) if has_add else None
    if readout:
        x_ref, mod_ref, g_ref, wo_ref, gain_ref, wout_ref = (
            next(it) for _ in range(6))
    out_ref, cf_ref, mf_ref = (next(it) for _ in range(3))
    c_s, m_s, a_s, x_s, s_s, w_s, ktw_s = (next(it) for _ in range(7))
    if readout:
        h_s, z_s = next(it), next(it)
    h_ref = h_s if readout else out_ref
    t = pl.program_id(1)
    n_t = pl.num_programs(1)
    L = SCAN_CHUNK
    nq = N_HEADS * DK
    n_chunks = qv_ref.shape[0] // L
    half = N_GATE // 2

    @pl.when(t == 0)
    def _():
        c_s[...] = c0_ref[...]
        m_s[...] = m0_ref[...]

    ti = lax.broadcasted_iota(jnp.int32, (L, L), 0)
    si = lax.broadcasted_iota(jnp.int32, (L, L), 1)
    mask = (si >= ti) if rev else (si <= ti)
    gcol = N_HEADS if rev else 0
    ones = jnp.ones((L, DK), BF16)
    order = range(n_chunks - 1, -1, -1) if rev else range(n_chunks)

    units = [(c, h) for c in order for h in range(N_HEADS)]

    for c, h in units:
        r0, r1 = c * L, (c + 1) * L
        q = qv_ref[r0:r1, h * DK:(h + 1) * DK]
        kt = kt_ref[h * DK:(h + 1) * DK, r0:r1]
        s_s[c, h] = _dot(q, kt)
        gi = gcol + h
        u_row = gr_ref[gi:gi + 1, r0:r1]
        cm_end = jnp.max(u_row, axis=1, keepdims=True)
        ktw_s[c, h] = (kt.astype(F32) * jnp.exp(u_row - cm_end)).astype(BF16)
    for c, h in units:
        r0, r1 = c * L, (c + 1) * L
        gi = gcol + h
        u_row = gr_ref[gi:gi + 1, r0:r1]
        cm_bc = jnp.broadcast_to(gc_ref[r0:r1, gi:gi + 1], (L, DK))
        w0 = jnp.exp(jnp.where(mask, u_row - cm_bc, -jnp.inf)) * s_s[c, h]
        w_s[c, h] = w0.astype(BF16)

    for c in order:
        r0, r1 = c * L, (c + 1) * L
        for h in range(N_HEADS):
            v = qv_ref[r0:r1, nq + h * DV:nq + (h + 1) * DV]
            v_aug = jnp.concatenate([v, ones], axis=1)
            a_s[h] = _dot(w_s[c, h], v_aug)
            x_s[h] = _dot(ktw_s[c, h], v_aug)
        for h in range(N_HEADS):
            gi = gcol + h
            q = qv_ref[r0:r1, h * DK:(h + 1) * DK]
            u_row = gr_ref[gi:gi + 1, r0:r1]
            b_row = gr_ref[half + gi:half + gi + 1, r0:r1]
            g = b_row[:, 0:1] if rev else b_row[:, L - 1:L]
            cm_end = jnp.max(u_row, axis=1, keepdims=True)
            m_prev = m_s[h]
            c_prev = c_s[h]
            a_aug = a_s[h]
            b_aug = _dot(q, c_prev.astype(BF16))
            cm_bc = jnp.broadcast_to(gc_ref[r0:r1, gi:gi + 1], (L, DK))
            b_bc = jnp.broadcast_to(gc_ref[r0:r1, half + gi:half + gi + 1],
                                    (L, DK))
            r = jnp.maximum(m_prev, cm_bc)
            rho = jnp.exp(cm_bc - r)
            wi = jnp.exp(m_prev - r)
            den = rho * a_aug[:, DV:] + wi * b_aug[:, DV:]
            inv = 1.0 / jnp.maximum(jnp.abs(den), jnp.exp(-(b_bc + r)))
            alpha = rho * inv
            beta = wi * inv
            for j in range(DV // DK):
                lo, hi = j * DK, (j + 1) * DK
                cols = slice(h * DV + lo, h * DV + hi)
                hh = alpha * a_aug[:, lo:hi] + beta * b_aug[:, lo:hi]
                if has_add:
                    hh = hh + hadd_ref[r0:r1, cols]
                h_ref[r0:r1, cols] = hh
            mx = jnp.maximum(m_prev, cm_end)
            decay = jnp.exp(m_prev - mx)
            gamma = jnp.exp(cm_end - mx)
            for j in range((DV + DK) // DK):
                lo, hi = j * DK, (j + 1) * DK
                c_s[h, :, lo:hi] = (decay * c_prev[:, lo:hi]
                                    + gamma * x_s[h, :, lo:hi])
            m_s[h] = g + mx

    @pl.when(t == n_t - 1)
    def _():
        cf_ref[...] = c_s[...]
        mf_ref[...] = m_s[...]

    if readout:
        x = x_ref[...]
        hx = _modulate(x, g_ref[...], mod_ref[0:1, :], mod_ref[1:2, :]).astype(BF16)
        for h in range(N_HEADS):
            lo, hi = h * DV, (h + 1) * DV
            hh = h_s[:, lo:hi]
            ms = jnp.mean(hh * hh, axis=-1, keepdims=True)
            hn = hh * lax.rsqrt(ms + EPS) * gain_ref[:, lo:hi]
            o = _dot(hx, wo_ref[:, lo:hi])
            z_s[:, lo:hi] = (hn * _sigmoid(o)).astype(BF16)
        y = _dot(z_s[...], wout_ref[...])
        out_ref[...] = x + mod_ref[2:3, :] * y


def _scan(qv, kt, gc, gr, state, *, batch, ts, rev, add=None, readout=None):
    n = qv.shape[0]
    n_t = n // batch // ts
    n_chunks = ts // SCAN_CHUNK
    nq = N_HEADS * DK
    nv = N_HEADS * DV
    nqv = nq + nv

    def blk(b, t):
        return b * n_t + ((n_t - 1 - t) if rev else t)

    st_shapes = [(N_HEADS, DK, DV + DK), (N_HEADS, 1, DK)]
    st_specs = [pl.BlockSpec((None,) + s, lambda b, t: (b, 0, 0, 0))
                for s in st_shapes]
    h_spec = pl.BlockSpec((ts, nv), lambda b, t: (blk(b, t), 0))
    has_add = add is not None
    in_specs = [
        pl.BlockSpec((ts, nqv), lambda b, t: (blk(b, t), 0)),
        pl.BlockSpec((nq, ts), lambda b, t: (0, blk(b, t))),
        pl.BlockSpec((ts, GATE_PAD), lambda b, t: (blk(b, t), 0)),
        pl.BlockSpec((N_GATE, ts), lambda b, t: (0, blk(b, t))),
    ] + st_specs
    args = [qv, kt, gc, gr, *state]
    scratch = [pltpu.VMEM(s, F32) for s in st_shapes] + [
        pltpu.VMEM((N_HEADS, SCAN_CHUNK, DV + DK), F32),
        pltpu.VMEM((N_HEADS, DK, DV + DK), F32),
        pltpu.VMEM((n_chunks, N_HEADS, SCAN_CHUNK, SCAN_CHUNK), F32),
        pltpu.VMEM((n_chunks, N_HEADS, SCAN_CHUNK, SCAN_CHUNK), BF16),
        pltpu.VMEM((n_chunks, N_HEADS, DK, SCAN_CHUNK), BF16),
    ]
    if has_add:
        in_specs.append(h_spec)
        args.append(add)
    out_cols = nv
    if readout is not None:
        x2d, mod_l, mod_row_fn, g, w_o, gain, w_out = readout
        d = x2d.shape[1]
        out_cols = d
        in_specs += [
            pl.BlockSpec((ts, d), lambda b, t: (blk(b, t), 0)),
            pl.BlockSpec((None, 6, d), lambda b, t: (mod_row_fn(b), 0, 0)),
            _const_spec((1, d)),
            _const_spec((d, nv)),
            _const_spec((1, nv)),
            _const_spec((nv, d)),
        ]
        args += [x2d, mod_l, g, w_o, gain, w_out]
        scratch += [pltpu.VMEM((ts, nv), F32), pltpu.VMEM((ts, nv), BF16)]
    out, cf, mf = pl.pallas_call(
        functools.partial(_scan_kernel, rev=rev, has_add=has_add,
                          readout=readout is not None),
        grid=(batch, n_t),
        in_specs=in_specs,
        out_specs=[pl.BlockSpec((ts, out_cols), lambda b, t: (blk(b, t), 0))]
        + st_specs,
        out_shape=[jax.ShapeDtypeStruct((n, out_cols), F32)]
        + [jax.ShapeDtypeStruct((batch,) + s, F32) for s in st_shapes],
        scratch_shapes=scratch,
        compiler_params=_params(2),
        name="mlstm_scan_bwd" if rev else "mlstm_scan_fwd",
    )(*args)
    return out, (cf, mf)


def _mlstm_weights(w_in, b_gate):
    nq = N_HEADS * DK
    o2 = 2 * nq
    o3 = o2 + N_HEADS * DV
    o4 = o3 + D_MODEL
    w_g = w_in[:, o4:]
    return {
        "w_qv": jnp.concatenate([w_in[:, :nq], w_in[:, o2:o3]], axis=1).astype(BF16),
        "w_kt": w_in[:, nq:o2].T.astype(BF16),
        "w_o": w_in[:, o3:o4].astype(BF16),
        "w_g": jnp.pad(w_g, ((0, 0), (0, GATE_PAD - N_GATE))).astype(BF16),
        "w_gt": w_g.T.astype(BF16),
        "b_g": jnp.pad(b_gate, (0, GATE_PAD - N_GATE)).reshape(1, GATE_PAD),
        "b_gt": b_gate.reshape(N_GATE, 1),
    }


def kernel(x, c, ctx, c_ctx, w_mod, b_mod, g_mix, g_ffn, a_w_in, a_b_gate,
           a_head_gain, a_w_out, b_w_in, b_w_conv, b_w_out, f_w_up, f_w_conv,
           f_b_conv, f_w_down, g_final):
    bn, t, d = x.shape
    n_ctx = ctx.shape[1]
    depth = w_mod.shape[0]
    n_mixers = 2
    assert d == D_MODEL and bn <= 7 and t % TOKEN_TILE == 0
    assert n_ctx % SCAN_CHUNK == 0 and TOKEN_TILE % SCAN_CHUNK == 0

    rec_layers = [i for i in range(depth) if i % n_mixers == 0]
    last_rec = max(rec_layers) if rec_layers else -1

    cc = jnp.zeros((8, d), F32).at[:bn].set(c).at[bn].set(c_ctx)
    mod = _mod_all(cc, w_mod, b_mod).reshape(depth, 8, 6, d)

    tm = TOKEN_TILE
    tiles_per_batch = t // tm
    lat_row = lambda i: i // tiles_per_batch
    ctx_row = lambda i: bn

    xs = x.reshape(bn * t, d)
    cs = ctx.reshape(bn * n_ctx, d)
    zero_state = (jnp.zeros((bn, N_HEADS, DK, DV + DK), F32),
                  jnp.zeros((bn, N_HEADS, 1, DK), F32))

    for i in range(depth):
        j = i // n_mixers
        ctx_read = i <= last_rec
        ctx_live = i < last_rec
        gm = g_mix[i].reshape(1, d)
        gf = g_ffn[i].reshape(1, d)
        if i % n_mixers == 0:
            wts = _mlstm_weights(a_w_in[j], a_b_gate[j])
            w_out = a_w_out[j].astype(BF16)
            gain = a_head_gain[j].reshape(1, N_HEADS * DV)
            s_f = s_b = zero_state
            if ctx_read:
                pc = _mlstm_proj(cs, mod[i], gm, wts, tm=n_ctx, row_fn=ctx_row)
                hc, s_b = _scan(*pc, zero_state, batch=bn, ts=n_ctx, rev=True)
                ro = ((cs, mod[i], lambda b: bn, gm, wts["w_o"], gain, w_out)
                      if ctx_live else None)
                cs_new, s_f = _scan(*pc, zero_state, batch=bn, ts=n_ctx,
                                    rev=False, add=hc, readout=ro)
            px = _mlstm_proj(xs, mod[i], gm, wts, tm=tm, row_fn=lat_row)
            hl, _ = _scan(*px, s_b, batch=bn, ts=tm, rev=True)
            ro = (xs, mod[i], lambda b: b, gm, wts["w_o"], gain, w_out)
            xs, _ = _scan(*px, s_f, batch=bn, ts=tm, rev=False, add=hl,
                          readout=ro)
            if ctx_live:
                cs = cs_new
        else:
            w_in = b_w_in[j].astype(BF16)
            w_out = b_w_out[j].astype(BF16)
            xs = _shortconv(xs, mod[i], gm, w_in, b_w_conv[j], w_out,
                            tm=tm, seg=GRID_W, row_fn=lat_row)
            if ctx_live:
                cs = _shortconv(cs, mod[i], gm, w_in, b_w_conv[j], w_out,
                                tm=n_ctx, seg=n_ctx, row_fn=ctx_row)
        w_up = f_w_up[i].astype(BF16)
        w_down = f_w_down[i].astype(BF16)
        b_conv = f_b_conv[i].reshape(1, D_FF)
        gfin = g_final.reshape(1, d)
        xs = _ffn(xs, mod[i], gf, w_up, f_w_conv[i], b_conv, w_down, gfin,
                  batch=bn, strip=FFN_STRIP, final=(i == depth - 1))
        if ctx_live:
            cs = _ffn(cs, mod[i], gf, w_up, f_w_conv[i], b_conv, w_down, gfin,
                      batch=bn, strip=0, final=False)

    return xs.reshape(bn, t, d)
```

```python
import functools

import jax
import jax.numpy as jnp
from jax import lax
from jax.experimental import pallas as pl
from jax.experimental.pallas import tpu as pltpu

F32 = jnp.float32
BF16 = jnp.bfloat16

D_MODEL = 1024
GRID_W = 64
N_HEADS = 4
DK = 128
DV = 256
D_FF = 2816
EPS = 1e-6
N_GATE = 16
GATE_PAD = 128

SCAN_CHUNK = 128
FF_CHUNK = 256
DOWN_PIECES = 2
ROW_PIECES = 2
TOKEN_TILE = 1024
FFN_STRIP = 8
VMEM_LIMIT = 56 * 1024 * 1024


def _params(n_axes):
    return pltpu.CompilerParams(
        dimension_semantics=("arbitrary",) * n_axes,
        vmem_limit_bytes=VMEM_LIMIT)


def _const_spec(shape):
    zeros = (0,) * len(shape)
    return pl.BlockSpec(shape, lambda *_: zeros, pipeline_mode=pl.Buffered(1))


def _weight(w):
    if isinstance(w, tuple):
        arr, layer = w
        idx = (layer,) + (0,) * (arr.ndim - 1)
        spec = pl.BlockSpec((None,) + arr.shape[1:], lambda *_: idx,
                            pipeline_mode=pl.Buffered(1))
        return spec, arr
    return _const_spec(w.shape), w


def _dot(a, b):
    return jnp.dot(a, b, preferred_element_type=F32)


def _sigmoid(v):
    return 1.0 / (1.0 + jnp.exp(-v))


def _rmsnorm(x, g):
    ms = jnp.mean(x * x, axis=-1, keepdims=True)
    return (x * lax.rsqrt(ms + EPS)) * g


def _modulate(x, g, shift, scale):
    return _rmsnorm(x, g) * (1.0 + scale) + shift


def _shift_conv(u, w_ref, seg, shift=1):
    n = u.shape[0]
    if seg == n and shift % 8 == 0:
        pad = jnp.zeros((shift, u.shape[1]), u.dtype)
        prev = jnp.concatenate([pad, u[:n - shift]], axis=0)
        nxt = jnp.concatenate([u[shift:], pad], axis=0)
    else:
        pos = lax.broadcasted_iota(jnp.int32, (n, 1), 0) & (seg - 1)
        prev = jnp.where(pos >= shift, pltpu.roll(u, shift, axis=0), 0.0)
        nxt = jnp.where(pos < seg - shift, pltpu.roll(u, n - shift, axis=0), 0.0)
    return prev * w_ref[0:1, :] + u * w_ref[1:2, :] + nxt * w_ref[2:3, :]


def _mod_kernel(cc_ref, w_ref, b_ref, o_ref):
    cc = cc_ref[...]
    s = cc * _sigmoid(cc)
    o_ref[...] = _dot(s, w_ref[...]) + b_ref[...]


def _mod_all(cc, w_mod, b_mod):
    depth, d, n = w_mod.shape
    tn = 1536
    return pl.pallas_call(
        _mod_kernel,
        grid=(depth, n // tn),
        in_specs=[
            pl.BlockSpec((8, d), lambda l, j: (0, 0)),
            pl.BlockSpec((None, d, tn), lambda l, j: (l, 0, j)),
            pl.BlockSpec((None, 1, tn), lambda l, j: (l, 0, j)),
        ],
        out_specs=pl.BlockSpec((None, 8, tn), lambda l, j: (l, 0, j)),
        out_shape=jax.ShapeDtypeStruct((depth, 8, n), F32),
        compiler_params=_params(2),
        name="adaln_mod",
    )(cc, w_mod, b_mod.reshape(depth, 1, n))


def _shortconv_kernel(x_ref, mod_ref, g_ref, win_ref, wconv_ref, wout_ref,
                      o_ref, z_ref, *, seg):
    d = D_MODEL
    cw = 256
    tm = x_ref.shape[0]
    pieces = ROW_PIECES if (tm // ROW_PIECES) % seg == 0 else 1
    pr = tm // pieces
    for p in range(pieces):
        rows = slice(p * pr, (p + 1) * pr)
        x = x_ref[rows, :]
        hx = _modulate(x, g_ref[...], mod_ref[0:1, :], mod_ref[1:2, :]).astype(BF16)
        for j in range(d // cw):
            lo, hi = j * cw, (j + 1) * cw
            bg = _dot(hx, win_ref[:, lo:hi])
            cg = _dot(hx, win_ref[:, d + lo:d + hi])
            xv = _dot(hx, win_ref[:, 2 * d + lo:2 * d + hi])
            cv = _shift_conv(cg * xv, wconv_ref.at[:, lo:hi], seg)
            z_ref[rows, lo:hi] = (bg * cv).astype(BF16)
        y = _dot(z_ref[rows, :], wout_ref[...])
        o_ref[rows, :] = x + mod_ref[2:3, :] * y


def _shortconv(x2d, mod_l, g, w_in, w_conv, w_out, *, tm, seg, row_fn):
    n, d = x2d.shape
    win_spec, w_in = _weight(w_in)
    wout_spec, w_out = _weight(w_out)
    return pl.pallas_call(
        functools.partial(_shortconv_kernel, seg=seg),
        grid=(n // tm,),
        in_specs=[
            pl.BlockSpec((tm, d), lambda i: (i, 0)),
            pl.BlockSpec((None, 6, d), lambda i: (row_fn(i), 0, 0)),
            _const_spec((1, d)),
            win_spec,
            _const_spec((3, d)),
            wout_spec,
        ],
        out_specs=pl.BlockSpec((tm, d), lambda i: (i, 0)),
        out_shape=jax.ShapeDtypeStruct((n, d), F32),
        scratch_shapes=[pltpu.VMEM((tm, d), BF16)],
        compiler_params=_params(1),
        name="shortconv_mixer",
    )(x2d, mod_l, g, w_in, w_conv, w_out)


def _ffn_kernel(x_ref, mod_ref, g_ref, wup_ref, wconv_ref, bconv_ref, wdown_ref,
                gfin_ref, o_ref, act_ref, *, shift, seg, final):
    d = x_ref.shape[-1]
    x = x_ref[...].reshape(-1, d)
    hx = _modulate(x, g_ref[...], mod_ref[3:4, :], mod_ref[4:5, :]).astype(BF16)
    for c in range(D_FF // FF_CHUNK):
        lo, hi = c * FF_CHUNK, (c + 1) * FF_CHUNK
        val = _dot(hx, wup_ref[:, D_FF + lo:D_FF + hi])
        gate = _shift_conv(_dot(hx, wup_ref[:, lo:hi]), wconv_ref.at[:, lo:hi],
                           seg, shift)
        gate = gate + bconv_ref[:, lo:hi]
        act_ref[:, lo:hi] = (gate * _sigmoid(gate) * val).astype(BF16)
    tm = x.shape[0]
    pr = tm // DOWN_PIECES
    lead = o_ref.shape[0] // DOWN_PIECES
    for p in range(DOWN_PIECES):
        rows = slice(p * pr, (p + 1) * pr)
        y = _dot(act_ref[rows, :], wdown_ref[...])
        out = x[rows] + mod_ref[5:6, :] * y
        if final:
            out = _rmsnorm(out, gfin_ref[...])
        o_ref[p * lead:(p + 1) * lead] = out.reshape((lead,) + o_ref.shape[1:])


def _ffn(x2d, mod_l, g, w_up, w_conv, b_conv, w_down, g_final, *, batch,
         strip, final):
    n, d = x2d.shape
    wup_spec, w_up = _weight(w_up)
    wdown_spec, w_down = _weight(w_down)
    per = n // batch
    if strip:
        rows = per // GRID_W
        strips = GRID_W // strip
        tm = rows * strip
        xin = x2d.reshape(batch * rows, GRID_W, d)
        x_spec = pl.BlockSpec((rows, strip, d),
                              lambda i: (i // strips, i % strips, 0))
        grid = (batch * strips,)
        row_fn = lambda i: i // strips
        shift = strip
    else:
        tm = per
        xin = x2d
        x_spec = pl.BlockSpec((tm, d), lambda i: (i, 0))
        grid = (batch,)
        row_fn = lambda i: batch
        shift = 1
    out = pl.pallas_call(
        functools.partial(_ffn_kernel, shift=shift, seg=tm, final=final),
        grid=grid,
        in_specs=[
            x_spec,
            pl.BlockSpec((None, 6, d), lambda i: (row_fn(i), 0, 0)),
            _const_spec((1, d)),
            wup_spec,
            _const_spec((3, D_FF)),
            _const_spec((1, D_FF)),
            wdown_spec,
            _const_spec((1, d)),
        ],
        out_specs=x_spec,
        out_shape=jax.ShapeDtypeStruct(xin.shape, F32),
        scratch_shapes=[pltpu.VMEM((tm, D_FF), BF16)],
        compiler_params=_params(1),
        name="convglu_ffn",
    )(xin, mod_l, g, w_up, w_conv, b_conv, w_down, g_final)
    return out.reshape(n, d)


def _log_sigmoid(v):
    return -(jnp.maximum(-v, 0.0) + jnp.log(1.0 + jnp.exp(-jnp.abs(v))))


def _chunk_scans(v, axis, chunk, op, identity):
    n = v.shape[axis]
    shape = (n, 1) if axis == 0 else (1, n)
    pos = lax.broadcasted_iota(jnp.int32, shape, axis) & (chunk - 1)
    pre, suf = v, v
    step = 1
    while step < chunk:
        pre = op(pre, jnp.where(pos >= step,
                                pltpu.roll(pre, step, axis=axis), identity))
        suf = op(suf, jnp.where(pos < chunk - step,
                                pltpu.roll(suf, n - step, axis=axis), identity))
        step *= 2
    return pre, suf


def _mlstm_proj_kernel(x_ref, mod_ref, g_ref, wqv_ref, wkt_ref, wg_ref,
                       wgt_ref, bg_ref, bgt_ref,
                       qv_ref, kt_ref, gc_ref, gr_ref):
    nt = (((1,), (1,)), ((), ()))
    half = N_GATE // 2
    lane = lax.broadcasted_iota(jnp.int32, (1, GATE_PAD), 1)
    fwd = (lane & (half - 1)) < N_HEADS
    sub = lax.broadcasted_iota(jnp.int32, (N_GATE, 1), 0)
    tm = x_ref.shape[0]
    pieces = ROW_PIECES if (tm // ROW_PIECES) % SCAN_CHUNK == 0 else 1
    pr = tm // pieces
    for p in range(pieces):
        rows = slice(p * pr, (p + 1) * pr)
        hx = _modulate(x_ref[rows, :], g_ref[...], mod_ref[0:1, :],
                       mod_ref[1:2, :]).astype(BF16)

        gc = _dot(hx, wg_ref[...]) + bg_ref[...]
        pre, suf = _chunk_scans(_log_sigmoid(gc), 0, SCAN_CHUNK, jnp.add, 0.0)
        b = jnp.where(fwd, pre, suf)
        u = gc - pltpu.roll(b, GATE_PAD - half, axis=1)
        pre, suf = _chunk_scans(u, 0, SCAN_CHUNK, jnp.maximum, -jnp.inf)
        gc_ref[rows, :] = jnp.where(lane < half, jnp.where(fwd, pre, suf), b)

        gr = lax.dot_general(wgt_ref[...], hx, nt, preferred_element_type=F32)
        gr = gr + bgt_ref[...]
        pre, suf = _chunk_scans(_log_sigmoid(gr), 1, SCAN_CHUNK, jnp.add, 0.0)
        b = jnp.where((sub & (half - 1)) < N_HEADS, pre, suf)
        gr_ref[0:half, rows] = gr[0:half] - b[half:]
        gr_ref[half:, rows] = b[half:]

        qv_ref[rows, :] = _dot(hx, wqv_ref[...]).astype(BF16)
        kt = lax.dot_general(wkt_ref[...], hx, nt, preferred_element_type=F32)
        kt_ref[:, rows] = (kt * (DK ** -0.5)).astype(BF16)


def _mlstm_proj(x2d, mod_l, g, wts, *, tm, row_fn):
    n, d = x2d.shape
    nq = N_HEADS * DK
    nqv = nq + N_HEADS * DV
    return pl.pallas_call(
        _mlstm_proj_kernel,
        grid=(n // tm,),
        in_specs=[
            pl.BlockSpec((tm, d), lambda i: (i, 0)),
            pl.BlockSpec((None, 6, d), lambda i: (row_fn(i), 0, 0)),
            _const_spec((1, d)),
            _const_spec((d, nqv)),
            _const_spec((nq, d)),
            _const_spec((d, GATE_PAD)),
            _const_spec((N_GATE, d)),
            _const_spec((1, GATE_PAD)),
            _const_spec((N_GATE, 1)),
        ],
        out_specs=[
            pl.BlockSpec((tm, nqv), lambda i: (i, 0)),
            pl.BlockSpec((nq, tm), lambda i: (0, i)),
            pl.BlockSpec((tm, GATE_PAD), lambda i: (i, 0)),
            pl.BlockSpec((N_GATE, tm), lambda i: (0, i)),
        ],
        out_shape=[
            jax.ShapeDtypeStruct((n, nqv), BF16),
            jax.ShapeDtypeStruct((nq, n), BF16),
            jax.ShapeDtypeStruct((n, GATE_PAD), F32),
            jax.ShapeDtypeStruct((N_GATE, n), F32),
        ],
        compiler_params=_params(1),
        name="mlstm_proj",
    )(x2d, mod_l, g, wts["w_qv"], wts["w_kt"], wts["w_g"], wts["w_gt"],
      wts["b_g"], wts["b_gt"])


def _scan_kernel(*refs, rev, has_add, readout):
    it = iter(refs)
    qv_ref, kt_ref, gc_ref, gr_ref, c0_ref, m0_ref = (next(it) for _ in range(6))
    hadd_ref = next(it) if has_add else None
    if readout:
        x_ref, mod_ref, g_ref, wo_ref, gain_ref, wout_ref = (
            next(it) for _ in range(6))
    out_ref, cf_ref, mf_ref = (next(it) for _ in range(3))
    c_s, m_s, a_s, x_s, s_s, w_s, ktw_s = (next(it) for _ in range(7))
    if readout:
        h_s, z_s = next(it), next(it)
    h_ref = h_s if readout else out_ref
    t = pl.program_id(1)
    n_t = pl.num_programs(1)
    L = SCAN_CHUNK
    nq = N_HEADS * DK
    n_chunks = qv_ref.shape[0] // L
    half = N_GATE // 2

    @pl.when(t == 0)
    def _():
        c_s[...] = c0_ref[...]
        m_s[...] = m0_ref[...]

    ti = lax.broadcasted_iota(jnp.int32, (L, L), 0)
    si = lax.broadcasted_iota(jnp.int32, (L, L), 1)
    mask = (si >= ti) if rev else (si <= ti)
    gcol = N_HEADS if rev else 0
    ones = jnp.ones((L, DK), BF16)
    order = range(n_chunks - 1, -1, -1) if rev else range(n_chunks)

    units = [(c, h) for c in order for h in range(N_HEADS)]

    for c, h in units:
        r0, r1 = c * L, (c + 1) * L
        q = qv_ref[r0:r1, h * DK:(h + 1) * DK]
        kt = kt_ref[h * DK:(h + 1) * DK, r0:r1]
        s_s[c, h] = _dot(q, kt)
        gi = gcol + h
        u_row = gr_ref[gi:gi + 1, r0:r1]
        cm_end = jnp.max(u_row, axis=1, keepdims=True)
        ktw_s[c, h] = (kt.astype(F32) * jnp.exp(u_row - cm_end)).astype(BF16)
    for c, h in units:
        r0, r1 = c * L, (c + 1) * L
        gi = gcol + h
        u_row = gr_ref[gi:gi + 1, r0:r1]
        cm_bc = jnp.broadcast_to(gc_ref[r0:r1, gi:gi + 1], (L, DK))
        w0 = jnp.exp(jnp.where(mask, u_row - cm_bc, -jnp.inf)) * s_s[c, h]
        w_s[c, h] = w0.astype(BF16)

    for c in order:
        r0, r1 = c * L, (c + 1) * L
        for h in range(N_HEADS):
            v = qv_ref[r0:r1, nq + h * DV:nq + (h + 1) * DV]
            v_aug = jnp.concatenate([v, ones], axis=1)
            a_s[h] = _dot(w_s[c, h], v_aug)
            x_s[h] = _dot(ktw_s[c, h], v_aug)
        for h in range(N_HEADS):
            gi = gcol + h
            q = qv_ref[r0:r1, h * DK:(h + 1) * DK]
            u_row = gr_ref[gi:gi + 1, r0:r1]
            b_row = gr_ref[half + gi:half + gi + 1, r0:r1]
            g = b_row[:, 0:1] if rev else b_row[:, L - 1:L]
            cm_end = jnp.max(u_row, axis=1, keepdims=True)
            m_prev = m_s[h]
            c_prev = c_s[h]
            a_aug = a_s[h]
            b_aug = _dot(q, c_prev.astype(BF16))
            cm_bc = jnp.broadcast_to(gc_ref[r0:r1, gi:gi + 1], (L, DK))
            b_bc = jnp.broadcast_to(gc_ref[r0:r1, half + gi:half + gi + 1],
                                    (L, DK))
            r = jnp.maximum(m_prev, cm_bc)
            rho = jnp.exp(cm_bc - r)
            wi = jnp.exp(m_prev - r)
            den = rho * a_aug[:, DV:] + wi * b_aug[:, DV:]
            inv = 1.0 / jnp.maximum(jnp.abs(den), jnp.exp(-(b_bc + r)))
            alpha = rho * inv
            beta = wi * inv
            for j in range(DV // DK):
                lo, hi = j * DK, (j + 1) * DK
                cols = slice(h * DV + lo, h * DV + hi)
                hh = alpha * a_aug[:, lo:hi] + beta * b_aug[:, lo:hi]
                if has_add:
                    hh = hh + hadd_ref[r0:r1, cols]
                h_ref[r0:r1, cols] = hh
            mx = jnp.maximum(m_prev, cm_end)
            decay = jnp.exp(m_prev - mx)
            gamma = jnp.exp(cm_end - mx)
            for j in range((DV + DK) // DK):
                lo, hi = j * DK, (j + 1) * DK
                c_s[h, :, lo:hi] = (decay * c_prev[:, lo:hi]
                                    + gamma * x_s[h, :, lo:hi])
            m_s[h] = g + mx

    @pl.when(t == n_t - 1)
    def _():
        cf_ref[...] = c_s[...]
        mf_ref[...] = m_s[...]

    if readout:
        x = x_ref[...]
        hx = _modulate(x, g_ref[...], mod_ref[0:1, :], mod_ref[1:2, :]).astype(BF16)
        for h in range(N_HEADS):
            lo, hi = h * DV, (h + 1) * DV
            hh = h_s[:, lo:hi]
            ms = jnp.mean(hh * hh, axis=-1, keepdims=True)
            hn = hh * lax.rsqrt(ms + EPS) * gain_ref[:, lo:hi]
            o = _dot(hx, wo_ref[:, lo:hi])
            z_s[:, lo:hi] = (hn * _sigmoid(o)).astype(BF16)
        y = _dot(z_s[...], wout_ref[...])
        out_ref[...] = x + mod_ref[2:3, :] * y


def _scan(qv, kt, gc, gr, state, *, batch, ts, rev, add=None, readout=None):
    n = qv.shape[0]
    n_t = n // batch // ts
    n_chunks = ts // SCAN_CHUNK
    nq = N_HEADS * DK
    nv = N_HEADS * DV
    nqv = nq + nv

    def blk(b, t):
        return b * n_t + ((n_t - 1 - t) if rev else t)

    st_shapes = [(N_HEADS, DK, DV + DK), (N_HEADS, 1, DK)]
    st_specs = [pl.BlockSpec((None,) + s, lambda b, t: (b, 0, 0, 0))
                for s in st_shapes]
    h_spec = pl.BlockSpec((ts, nv), lambda b, t: (blk(b, t), 0))
    has_add = add is not None
    in_specs = [
        pl.BlockSpec((ts, nqv), lambda b, t: (blk(b, t), 0)),
        pl.BlockSpec((nq, ts), lambda b, t: (0, blk(b, t))),
        pl.BlockSpec((ts, GATE_PAD), lambda b, t: (blk(b, t), 0)),
        pl.BlockSpec((N_GATE, ts), lambda b, t: (0, blk(b, t))),
    ] + st_specs
    args = [qv, kt, gc, gr, *state]
    scratch = [pltpu.VMEM(s, F32) for s in st_shapes] + [
        pltpu.VMEM((N_HEADS, SCAN_CHUNK, DV + DK), F32),
        pltpu.VMEM((N_HEADS, DK, DV + DK), F32),
        pltpu.VMEM((n_chunks, N_HEADS, SCAN_CHUNK, SCAN_CHUNK), F32),
        pltpu.VMEM((n_chunks, N_HEADS, SCAN_CHUNK, SCAN_CHUNK), BF16),
        pltpu.VMEM((n_chunks, N_HEADS, DK, SCAN_CHUNK), BF16),
    ]
    if has_add:
        in_specs.append(h_spec)
        args.append(add)
    out_cols = nv
    if readout is not None:
        x2d, mod_l, mod_row_fn, g, w_o, gain, w_out = readout
        wout_spec, w_out = _weight(w_out)
        d = x2d.shape[1]
        out_cols = d
        in_specs += [
            pl.BlockSpec((ts, d), lambda b, t: (blk(b, t), 0)),
            pl.BlockSpec((None, 6, d), lambda b, t: (mod_row_fn(b), 0, 0)),
            _const_spec((1, d)),
            _const_spec((d, nv)),
            _const_spec((1, nv)),
            wout_spec,
        ]
        args += [x2d, mod_l, g, w_o, gain, w_out]
        scratch += [pltpu.VMEM((ts, nv), F32), pltpu.VMEM((ts, nv), BF16)]
    out, cf, mf = pl.pallas_call(
        functools.partial(_scan_kernel, rev=rev, has_add=has_add,
                          readout=readout is not None),
        grid=(batch, n_t),
        in_specs=in_specs,
        out_specs=[pl.BlockSpec((ts, out_cols), lambda b, t: (blk(b, t), 0))]
        + st_specs,
        out_shape=[jax.ShapeDtypeStruct((n, out_cols), F32)]
        + [jax.ShapeDtypeStruct((batch,) + s, F32) for s in st_shapes],
        scratch_shapes=scratch,
        compiler_params=_params(2),
        name="mlstm_scan_bwd" if rev else "mlstm_scan_fwd",
    )(*args)
    return out, (cf, mf)


def _mlstm_weights(w_in, b_gate):
    nq = N_HEADS * DK
    o2 = 2 * nq
    o3 = o2 + N_HEADS * DV
    o4 = o3 + D_MODEL
    w_g = w_in[:, o4:]
    return {
        "w_qv": jnp.concatenate([w_in[:, :nq], w_in[:, o2:o3]], axis=1).astype(BF16),
        "w_kt": w_in[:, nq:o2].T.astype(BF16),
        "w_o": w_in[:, o3:o4].astype(BF16),
        "w_g": jnp.pad(w_g, ((0, 0), (0, GATE_PAD - N_GATE))).astype(BF16),
        "w_gt": w_g.T.astype(BF16),
        "b_g": jnp.pad(b_gate, (0, GATE_PAD - N_GATE)).reshape(1, GATE_PAD),
        "b_gt": b_gate.reshape(N_GATE, 1),
    }


def kernel(x, c, ctx, c_ctx, w_mod, b_mod, g_mix, g_ffn, a_w_in, a_b_gate,
           a_head_gain, a_w_out, b_w_in, b_w_conv, b_w_out, f_w_up, f_w_conv,
           f_b_conv, f_w_down, g_final):
    bn, t, d = x.shape
    n_ctx = ctx.shape[1]
    depth = w_mod.shape[0]
    n_mixers = 2
    assert d == D_MODEL and bn <= 7 and t % TOKEN_TILE == 0
    assert n_ctx % SCAN_CHUNK == 0 and TOKEN_TILE % SCAN_CHUNK == 0

    rec_layers = [i for i in range(depth) if i % n_mixers == 0]
    last_rec = max(rec_layers) if rec_layers else -1

    cc = jnp.zeros((8, d), F32).at[:bn].set(c).at[bn].set(c_ctx)
    mod = _mod_all(cc, w_mod, b_mod).reshape(depth, 8, 6, d)

    tm = TOKEN_TILE
    tiles_per_batch = t // tm
    lat_row = lambda i: i // tiles_per_batch
    ctx_row = lambda i: bn

    xs = x.reshape(bn * t, d)
    cs = ctx.reshape(bn * n_ctx, d)
    zero_state = (jnp.zeros((bn, N_HEADS, DK, DV + DK), F32),
                  jnp.zeros((bn, N_HEADS, 1, DK), F32))

    a_w_out_b = a_w_out.astype(BF16)
    b_w_in_b = b_w_in.astype(BF16)
    b_w_out_b = b_w_out.astype(BF16)
    f_w_up_b = f_w_up.astype(BF16)
    f_w_down_b = f_w_down.astype(BF16)

    for i in range(depth):
        j = i // n_mixers
        ctx_read = i <= last_rec
        ctx_live = i < last_rec
        gm = g_mix[i].reshape(1, d)
        gf = g_ffn[i].reshape(1, d)
        if i % n_mixers == 0:
            wts = _mlstm_weights(a_w_in[j], a_b_gate[j])
            w_out = (a_w_out_b, j)
            gain = a_head_gain[j].reshape(1, N_HEADS * DV)
            s_f = s_b = zero_state
            if ctx_read:
                pc = _mlstm_proj(cs, mod[i], gm, wts, tm=n_ctx, row_fn=ctx_row)
                hc, s_b = _scan(*pc, zero_state, batch=bn, ts=n_ctx, rev=True)
                ro = ((cs, mod[i], lambda b: bn, gm, wts["w_o"], gain, w_out)
                      if ctx_live else None)
                cs_new, s_f = _scan(*pc, zero_state, batch=bn, ts=n_ctx,
                                    rev=False, add=hc, readout=ro)
            px = _mlstm_proj(xs, mod[i], gm, wts, tm=tm, row_fn=lat_row)
            hl, _ = _scan(*px, s_b, batch=bn, ts=tm, rev=True)
            ro = (xs, mod[i], lambda b: b, gm, wts["w_o"], gain, w_out)
            xs, _ = _scan(*px, s_f, batch=bn, ts=tm, rev=False, add=hl,
                          readout=ro)
            if ctx_live:
                cs = cs_new
        else:
            w_in = (b_w_in_b, j)
            w_out = (b_w_out_b, j)
            xs = _shortconv(xs, mod[i], gm, w_in, b_w_conv[j], w_out,
                            tm=tm, seg=GRID_W, row_fn=lat_row)
            if ctx_live:
                cs = _shortconv(cs, mod[i], gm, w_in, b_w_conv[j], w_out,
                                tm=n_ctx, seg=n_ctx, row_fn=ctx_row)
        w_up = (f_w_up_b, i)
        w_down = (f_w_down_b, i)
        b_conv = f_b_conv[i].reshape(1, D_FF)
        gfin = g_final.reshape(1, d)
        xs = _ffn(xs, mod[i], gf, w_up, f_w_conv[i], b_conv, w_down, gfin,
                  batch=bn, strip=FFN_STRIP, final=(i == depth - 1))
        if ctx_live:
            cs = _ffn(cs, mod[i], gf, w_up, f_w_conv[i], b_conv, w_down, gfin,
                      batch=bn, strip=0, final=False)

    return xs.reshape(bn, t, d)
```

```python
import functools

import jax
import jax.numpy as jnp
from jax import lax
from jax.experimental import pallas as pl
from jax.experimental.pallas import tpu as pltpu

F32 = jnp.float32
BF16 = jnp.bfloat16

D_MODEL = 1024
GRID_W = 64
N_HEADS = 4
DK = 128
DV = 256
D_FF = 2816
EPS = 1e-6
N_GATE = 16
GATE_PAD = 128

SCAN_CHUNK = 128
FF_CHUNK = 256
DOWN_PIECES = 2
ROW_PIECES = 2
TOKEN_TILE = 1024
FFN_STRIP = 8
VMEM_LIMIT = 56 * 1024 * 1024


def _params(n_axes):
    return pltpu.CompilerParams(
        dimension_semantics=("arbitrary",) * n_axes,
        vmem_limit_bytes=VMEM_LIMIT)


def _const_spec(shape):
    zeros = (0,) * len(shape)
    return pl.BlockSpec(shape, lambda *_: zeros, pipeline_mode=pl.Buffered(1))


def _weight(w):
    if isinstance(w, tuple):
        arr, layer = w
        idx = (layer,) + (0,) * (arr.ndim - 1)
        spec = pl.BlockSpec((None,) + arr.shape[1:], lambda *_: idx,
                            pipeline_mode=pl.Buffered(1))
        return spec, arr
    return _const_spec(w.shape), w


def _dot(a, b):
    return jnp.dot(a, b, preferred_element_type=F32)


def _sigmoid(v):
    return 1.0 / (1.0 + jnp.exp(-v))


def _rmsnorm(x, g):
    ms = jnp.mean(x * x, axis=-1, keepdims=True)
    return (x * lax.rsqrt(ms + EPS)) * g


def _modulate(x, g, shift, scale):
    return _rmsnorm(x, g) * (1.0 + scale) + shift


def _shift_conv(u, w_ref, seg, shift=1):
    n = u.shape[0]
    if seg == n and shift % 8 == 0:
        pad = jnp.zeros((shift, u.shape[1]), u.dtype)
        prev = jnp.concatenate([pad, u[:n - shift]], axis=0)
        nxt = jnp.concatenate([u[shift:], pad], axis=0)
    else:
        pos = lax.broadcasted_iota(jnp.int32, (n, 1), 0) & (seg - 1)
        prev = jnp.where(pos >= shift, pltpu.roll(u, shift, axis=0), 0.0)
        nxt = jnp.where(pos < seg - shift, pltpu.roll(u, n - shift, axis=0), 0.0)
    return prev * w_ref[0:1, :] + u * w_ref[1:2, :] + nxt * w_ref[2:3, :]


def _mod_kernel(cc_ref, w_ref, b_ref, o_ref):
    cc = cc_ref[...]
    s = cc * _sigmoid(cc)
    o_ref[...] = _dot(s, w_ref[...]) + b_ref[...]


def _mod_all(cc, w_mod, b_mod):
    depth, d, n = w_mod.shape
    tn = 1536
    return pl.pallas_call(
        _mod_kernel,
        grid=(depth, n // tn),
        in_specs=[
            pl.BlockSpec((8, d), lambda l, j: (0, 0)),
            pl.BlockSpec((None, d, tn), lambda l, j: (l, 0, j)),
            pl.BlockSpec((None, 1, tn), lambda l, j: (l, 0, j)),
        ],
        out_specs=pl.BlockSpec((None, 8, tn), lambda l, j: (l, 0, j)),
        out_shape=jax.ShapeDtypeStruct((depth, 8, n), F32),
        compiler_params=_params(2),
        name="adaln_mod",
    )(cc, w_mod, b_mod.reshape(depth, 1, n))


def _shortconv_kernel(x_ref, mod_ref, g_ref, win_ref, wconv_ref, wout_ref,
                      o_ref, z_ref, *, seg):
    d = D_MODEL
    cw = 256
    tm = x_ref.shape[0]
    pieces = ROW_PIECES if (tm // ROW_PIECES) % seg == 0 else 1
    pr = tm // pieces
    for p in range(pieces):
        rows = slice(p * pr, (p + 1) * pr)
        x = x_ref[rows, :]
        hx = _modulate(x, g_ref[...], mod_ref[0:1, :], mod_ref[1:2, :]).astype(BF16)
        for j in range(d // cw):
            lo, hi = j * cw, (j + 1) * cw
            bg = _dot(hx, win_ref[:, lo:hi])
            cg = _dot(hx, win_ref[:, d + lo:d + hi])
            xv = _dot(hx, win_ref[:, 2 * d + lo:2 * d + hi])
            cv = _shift_conv(cg * xv, wconv_ref.at[:, lo:hi], seg)
            z_ref[rows, lo:hi] = (bg * cv).astype(BF16)
        y = _dot(z_ref[rows, :], wout_ref[...])
        o_ref[rows, :] = x + mod_ref[2:3, :] * y


def _shortconv(x2d, mod_l, g, w_in, w_conv, w_out, *, tm, seg, row_fn):
    n, d = x2d.shape
    win_spec, w_in = _weight(w_in)
    wout_spec, w_out = _weight(w_out)
    return pl.pallas_call(
        functools.partial(_shortconv_kernel, seg=seg),
        grid=(n // tm,),
        in_specs=[
            pl.BlockSpec((tm, d), lambda i: (i, 0)),
            pl.BlockSpec((None, 6, d), lambda i: (row_fn(i), 0, 0)),
            _const_spec((1, d)),
            win_spec,
            _const_spec((3, d)),
            wout_spec,
        ],
        out_specs=pl.BlockSpec((tm, d), lambda i: (i, 0)),
        out_shape=jax.ShapeDtypeStruct((n, d), F32),
        scratch_shapes=[pltpu.VMEM((tm, d), BF16)],
        compiler_params=_params(1),
        name="shortconv_mixer",
    )(x2d, mod_l, g, w_in, w_conv, w_out)


def _ffn_kernel(x_ref, mod_ref, g_ref, wup_ref, wconv_ref, bconv_ref, wdown_ref,
                gfin_ref, o_ref, act_ref, *, shift, seg, final):
    d = x_ref.shape[-1]
    x = x_ref[...].reshape(-1, d)
    hx = _modulate(x, g_ref[...], mod_ref[3:4, :], mod_ref[4:5, :]).astype(BF16)
    for c in range(D_FF // FF_CHUNK):
        lo, hi = c * FF_CHUNK, (c + 1) * FF_CHUNK
        val = _dot(hx, wup_ref[:, D_FF + lo:D_FF + hi])
        gate = _shift_conv(_dot(hx, wup_ref[:, lo:hi]), wconv_ref.at[:, lo:hi],
                           seg, shift)
        gate = gate + bconv_ref[:, lo:hi]
        act_ref[:, lo:hi] = (gate * _sigmoid(gate) * val).astype(BF16)
    tm = x.shape[0]
    pr = tm // DOWN_PIECES
    lead = o_ref.shape[0] // DOWN_PIECES
    for p in range(DOWN_PIECES):
        rows = slice(p * pr, (p + 1) * pr)
        y = _dot(act_ref[rows, :], wdown_ref[...])
        out = x[rows] + mod_ref[5:6, :] * y
        if final:
            out = _rmsnorm(out, gfin_ref[...])
        o_ref[p * lead:(p + 1) * lead] = out.reshape((lead,) + o_ref.shape[1:])


def _ffn(x2d, mod_l, g, w_up, w_conv, b_conv, w_down, g_final, *, batch,
         strip, final):
    n, d = x2d.shape
    wup_spec, w_up = _weight(w_up)
    wdown_spec, w_down = _weight(w_down)
    per = n // batch
    if strip:
        rows = per // GRID_W
        strips = GRID_W // strip
        tm = rows * strip
        xin = x2d.reshape(batch * rows, GRID_W, d)
        x_spec = pl.BlockSpec((rows, strip, d),
                              lambda i: (i // strips, i % strips, 0))
        grid = (batch * strips,)
        row_fn = lambda i: i // strips
        shift = strip
    else:
        tm = per
        xin = x2d
        x_spec = pl.BlockSpec((tm, d), lambda i: (i, 0))
        grid = (batch,)
        row_fn = lambda i: batch
        shift = 1
    out = pl.pallas_call(
        functools.partial(_ffn_kernel, shift=shift, seg=tm, final=final),
        grid=grid,
        in_specs=[
            x_spec,
            pl.BlockSpec((None, 6, d), lambda i: (row_fn(i), 0, 0)),
            _const_spec((1, d)),
            wup_spec,
            _const_spec((3, D_FF)),
            _const_spec((1, D_FF)),
            wdown_spec,
            _const_spec((1, d)),
        ],
        out_specs=x_spec,
        out_shape=jax.ShapeDtypeStruct(xin.shape, F32),
        scratch_shapes=[pltpu.VMEM((tm, D_FF), BF16)],
        compiler_params=_params(1),
        name="convglu_ffn",
    )(xin, mod_l, g, w_up, w_conv, b_conv, w_down, g_final)
    return out.reshape(n, d)


def _log_sigmoid(v):
    return -(jnp.maximum(-v, 0.0) + jnp.log(1.0 + jnp.exp(-jnp.abs(v))))


def _chunk_scans(v, axis, chunk, op, identity):
    n = v.shape[axis]
    shape = (n, 1) if axis == 0 else (1, n)
    pos = lax.broadcasted_iota(jnp.int32, shape, axis) & (chunk - 1)
    pre, suf = v, v
    step = 1
    while step < chunk:
        pre = op(pre, jnp.where(pos >= step,
                                pltpu.roll(pre, step, axis=axis), identity))
        suf = op(suf, jnp.where(pos < chunk - step,
                                pltpu.roll(suf, n - step, axis=axis), identity))
        step *= 2
    return pre, suf


def _mlstm_proj_kernel(x_ref, mod_ref, g_ref, wqv_ref, wkt_ref, wg_ref,
                       wgt_ref, bg_ref, bgt_ref,
                       qv_ref, kt_ref, gc_ref, gr_ref):
    nt = (((1,), (1,)), ((), ()))
    half = N_GATE // 2
    lane = lax.broadcasted_iota(jnp.int32, (1, GATE_PAD), 1)
    fwd = (lane & (half - 1)) < N_HEADS
    sub = lax.broadcasted_iota(jnp.int32, (N_GATE, 1), 0)
    tm = x_ref.shape[0]
    pieces = ROW_PIECES if (tm // ROW_PIECES) % SCAN_CHUNK == 0 else 1
    pr = tm // pieces
    for p in range(pieces):
        rows = slice(p * pr, (p + 1) * pr)
        hx = _modulate(x_ref[rows, :], g_ref[...], mod_ref[0:1, :],
                       mod_ref[1:2, :]).astype(BF16)

        gc = _dot(hx, wg_ref[...]) + bg_ref[...]
        pre, suf = _chunk_scans(_log_sigmoid(gc), 0, SCAN_CHUNK, jnp.add, 0.0)
        b = jnp.where(fwd, pre, suf)
        u = gc - pltpu.roll(b, GATE_PAD - half, axis=1)
        pre, suf = _chunk_scans(u, 0, SCAN_CHUNK, jnp.maximum, -jnp.inf)
        gc_ref[rows, :] = jnp.where(lane < half, jnp.where(fwd, pre, suf), b)

        gr = lax.dot_general(wgt_ref[...], hx, nt, preferred_element_type=F32)
        gr = gr + bgt_ref[...]
        pre, suf = _chunk_scans(_log_sigmoid(gr), 1, SCAN_CHUNK, jnp.add, 0.0)
        b = jnp.where((sub & (half - 1)) < N_HEADS, pre, suf)
        gr_ref[0:half, rows] = gr[0:half] - b[half:]
        gr_ref[half:, rows] = b[half:]

        qv_ref[rows, :] = _dot(hx, wqv_ref[...]).astype(BF16)
        kt = lax.dot_general(wkt_ref[...], hx, nt, preferred_element_type=F32)
        kt_ref[:, rows] = (kt * (DK ** -0.5)).astype(BF16)


def _mlstm_proj(x2d, mod_l, g, wts, *, tm, row_fn):
    n, d = x2d.shape
    nq = N_HEADS * DK
    nqv = nq + N_HEADS * DV
    return pl.pallas_call(
        _mlstm_proj_kernel,
        grid=(n // tm,),
        in_specs=[
            pl.BlockSpec((tm, d), lambda i: (i, 0)),
            pl.BlockSpec((None, 6, d), lambda i: (row_fn(i), 0, 0)),
            _const_spec((1, d)),
            _const_spec((d, nqv)),
            _const_spec((nq, d)),
            _const_spec((d, GATE_PAD)),
            _const_spec((N_GATE, d)),
            _const_spec((1, GATE_PAD)),
            _const_spec((N_GATE, 1)),
        ],
        out_specs=[
            pl.BlockSpec((tm, nqv), lambda i: (i, 0)),
            pl.BlockSpec((nq, tm), lambda i: (0, i)),
            pl.BlockSpec((tm, GATE_PAD), lambda i: (i, 0)),
            pl.BlockSpec((N_GATE, tm), lambda i: (0, i)),
        ],
        out_shape=[
            jax.ShapeDtypeStruct((n, nqv), BF16),
            jax.ShapeDtypeStruct((nq, n), BF16),
            jax.ShapeDtypeStruct((n, GATE_PAD), F32),
            jax.ShapeDtypeStruct((N_GATE, n), F32),
        ],
        compiler_params=_params(1),
        name="mlstm_proj",
    )(x2d, mod_l, g, wts["w_qv"], wts["w_kt"], wts["w_g"], wts["w_gt"],
      wts["b_g"], wts["b_gt"])


def _scan_kernel(*refs, rev, has_add, readout):
    it = iter(refs)
    qv_ref, kt_ref, gc_ref, gr_ref, c0_ref, m0_ref = (next(it) for _ in range(6))
    hadd_ref = next(it) if has_add else None
    if readout:
        x_ref, mod_ref, g_ref, wo_ref, gain_ref, wout_ref = (
            next(it) for _ in range(6))
    out_ref, cf_ref, mf_ref = (next(it) for _ in range(3))
    c_s, m_s, a_s, x_s, s_s, w_s, ktw_s = (next(it) for _ in range(7))
    if readout:
        h_s, z_s = next(it), next(it)
    h_ref = h_s if readout else out_ref
    assert not (rev and readout)
    t = pl.program_id(1)
    n_t = pl.num_programs(1)
    L = SCAN_CHUNK
    nq = N_HEADS * DK
    n_chunks = qv_ref.shape[0] // L
    half = N_GATE // 2

    @pl.when(t == 0)
    def _():
        c_s[...] = c0_ref[...]
        m_s[...] = m0_ref[...]

    ti = lax.broadcasted_iota(jnp.int32, (L, L), 0)
    si = lax.broadcasted_iota(jnp.int32, (L, L), 1)
    mask = (si >= ti) if rev else (si <= ti)
    gcol = N_HEADS if rev else 0
    ones = jnp.ones((L, DK), BF16)
    order = range(n_chunks - 1, -1, -1) if rev else range(n_chunks)

    units = [(c, h) for c in order for h in range(N_HEADS)]

    m_in, mx_of, decay_of = {}, {}, {}
    for h in range(N_HEADS):
        gi = gcol + h
        m_prev = m_s[h]
        for c in order:
            r0, r1 = c * L, (c + 1) * L
            u_row = gr_ref[gi:gi + 1, r0:r1]
            b_row = gr_ref[half + gi:half + gi + 1, r0:r1]
            g = b_row[:, 0:1] if rev else b_row[:, L - 1:L]
            mx = jnp.maximum(m_prev, jnp.max(u_row, axis=1, keepdims=True))
            m_in[c, h], mx_of[c, h] = m_prev, mx
            decay_of[c, h] = jnp.exp(m_prev - mx)
            m_prev = g + mx
        m_s[h] = m_prev

    for c, h in units:
        r0, r1 = c * L, (c + 1) * L
        q = qv_ref[r0:r1, h * DK:(h + 1) * DK]
        kt = kt_ref[h * DK:(h + 1) * DK, r0:r1]
        s_s[c, h] = _dot(q, kt)
        u_row = gr_ref[gcol + h:gcol + h + 1, r0:r1]
        ktw_s[c, h] = (kt.astype(F32) * jnp.exp(u_row - mx_of[c, h])).astype(BF16)
    o_piece = 2 if n_chunks % 2 == 0 else 1
    for i, (c, h) in enumerate(units):
        r0, r1 = c * L, (c + 1) * L
        gi = gcol + h
        u_row = gr_ref[gi:gi + 1, r0:r1]
        cm_bc = jnp.broadcast_to(gc_ref[r0:r1, gi:gi + 1], (L, DK))
        r = jnp.maximum(m_in[c, h], cm_bc)
        w = jnp.exp(jnp.where(mask, u_row - r, -jnp.inf)) * s_s[c, h]
        w_s[c, h] = w.astype(BF16)
        if readout and i % (o_piece * N_HEADS) == 0:
            p0 = (i // (o_piece * N_HEADS)) * o_piece * L
            rows = slice(p0, p0 + o_piece * L)
            hx = _modulate(x_ref[rows, :], g_ref[...], mod_ref[0:1, :],
                           mod_ref[1:2, :]).astype(BF16)
            out_ref[rows, :] = _dot(hx, wo_ref[...])

    for c in order:
        r0, r1 = c * L, (c + 1) * L
        for h in range(N_HEADS):
            v = qv_ref[r0:r1, nq + h * DV:nq + (h + 1) * DV]
            v_aug = jnp.concatenate([v, ones], axis=1)
            a_s[h] = _dot(w_s[c, h], v_aug)
            x_s[h] = _dot(ktw_s[c, h], v_aug)
        for h in range(N_HEADS):
            gi = gcol + h
            q = qv_ref[r0:r1, h * DK:(h + 1) * DK]
            m_prev = m_in[c, h]
            c_prev = c_s[h]
            a_aug = a_s[h]
            b_aug = _dot(q, c_prev.astype(BF16))
            cm_bc = jnp.broadcast_to(gc_ref[r0:r1, gi:gi + 1], (L, DK))
            b_bc = jnp.broadcast_to(gc_ref[r0:r1, half + gi:half + gi + 1],
                                    (L, DK))
            r = jnp.maximum(m_prev, cm_bc)
            wi = jnp.exp(m_prev - r)
            den = a_aug[:, DV:] + wi * b_aug[:, DV:]
            inv = 1.0 / jnp.maximum(jnp.abs(den), jnp.exp(-(b_bc + r)))
            beta = wi * inv
            for j in range(DV // DK):
                lo, hi = j * DK, (j + 1) * DK
                cols = slice(h * DV + lo, h * DV + hi)
                hh = inv * a_aug[:, lo:hi] + beta * b_aug[:, lo:hi]
                if has_add:
                    hh = hh + hadd_ref[r0:r1, cols]
                h_ref[r0:r1, cols] = hh
            for j in range((DV + DK) // DK):
                lo, hi = j * DK, (j + 1) * DK
                c_s[h, :, lo:hi] = (decay_of[c, h] * c_prev[:, lo:hi]
                                    + x_s[h, :, lo:hi])

    @pl.when(t == n_t - 1)
    def _():
        cf_ref[...] = c_s[...]
        mf_ref[...] = m_s[...]

    if readout:
        pr = o_piece * L
        for p in range(n_chunks // o_piece):
            rows = slice(p * pr, (p + 1) * pr)
            for h in range(N_HEADS):
                lo, hi = h * DV, (h + 1) * DV
                hh = h_s[rows, lo:hi]
                ms = jnp.mean(hh * hh, axis=-1, keepdims=True)
                hn = hh * lax.rsqrt(ms + EPS) * gain_ref[:, lo:hi]
                z_s[rows, lo:hi] = (hn * _sigmoid(out_ref[rows, lo:hi])
                                    ).astype(BF16)
            y = _dot(z_s[rows, :], wout_ref[...])
            out_ref[rows, :] = x_ref[rows, :] + mod_ref[2:3, :] * y


def _scan(qv, kt, gc, gr, state, *, batch, ts, rev, add=None, readout=None):
    n = qv.shape[0]
    n_t = n // batch // ts
    n_chunks = ts // SCAN_CHUNK
    nq = N_HEADS * DK
    nv = N_HEADS * DV
    nqv = nq + nv

    def blk(b, t):
        return b * n_t + ((n_t - 1 - t) if rev else t)

    st_shapes = [(N_HEADS, DK, DV + DK), (N_HEADS, 1, DK)]
    st_specs = [pl.BlockSpec((None,) + s, lambda b, t: (b, 0, 0, 0))
                for s in st_shapes]
    h_spec = pl.BlockSpec((ts, nv), lambda b, t: (blk(b, t), 0))
    has_add = add is not None
    in_specs = [
        pl.BlockSpec((ts, nqv), lambda b, t: (blk(b, t), 0)),
        pl.BlockSpec((nq, ts), lambda b, t: (0, blk(b, t))),
        pl.BlockSpec((ts, GATE_PAD), lambda b, t: (blk(b, t), 0)),
        pl.BlockSpec((N_GATE, ts), lambda b, t: (0, blk(b, t))),
    ] + st_specs
    args = [qv, kt, gc, gr, *state]
    scratch = [pltpu.VMEM(s, F32) for s in st_shapes] + [
        pltpu.VMEM((N_HEADS, SCAN_CHUNK, DV + DK), F32),
        pltpu.VMEM((N_HEADS, DK, DV + DK), F32),
        pltpu.VMEM((n_chunks, N_HEADS, SCAN_CHUNK, SCAN_CHUNK), F32),
        pltpu.VMEM((n_chunks, N_HEADS, SCAN_CHUNK, SCAN_CHUNK), BF16),
        pltpu.VMEM((n_chunks, N_HEADS, DK, SCAN_CHUNK), BF16),
    ]
    if has_add:
        in_specs.append(h_spec)
        args.append(add)
    out_cols = nv
    if readout is not None:
        x2d, mod_l, mod_row_fn, g, w_o, gain, w_out = readout
        wout_spec, w_out = _weight(w_out)
        d = x2d.shape[1]
        out_cols = d
        in_specs += [
            pl.BlockSpec((ts, d), lambda b, t: (blk(b, t), 0)),
            pl.BlockSpec((None, 6, d), lambda b, t: (mod_row_fn(b), 0, 0)),
            _const_spec((1, d)),
            _const_spec((d, nv)),
            _const_spec((1, nv)),
            wout_spec,
        ]
        args += [x2d, mod_l, g, w_o, gain, w_out]
        scratch += [pltpu.VMEM((ts, nv), F32), pltpu.VMEM((ts, nv), BF16)]
    out, cf, mf = pl.pallas_call(
        functools.partial(_scan_kernel, rev=rev, has_add=has_add,
                          readout=readout is not None),
        grid=(batch, n_t),
        in_specs=in_specs,
        out_specs=[pl.BlockSpec((ts, out_cols), lambda b, t: (blk(b, t), 0))]
        + st_specs,
        out_shape=[jax.ShapeDtypeStruct((n, out_cols), F32)]
        + [jax.ShapeDtypeStruct((batch,) + s, F32) for s in st_shapes],
        scratch_shapes=scratch,
        compiler_params=_params(2),
        name="mlstm_scan_bwd" if rev else "mlstm_scan_fwd",
    )(*args)
    return out, (cf, mf)


def _mlstm_weights(w_in, b_gate):
    nq = N_HEADS * DK
    o2 = 2 * nq
    o3 = o2 + N_HEADS * DV
    o4 = o3 + D_MODEL
    w_g = w_in[:, o4:]
    return {
        "w_qv": jnp.concatenate([w_in[:, :nq], w_in[:, o2:o3]], axis=1).astype(BF16),
        "w_kt": w_in[:, nq:o2].T.astype(BF16),
        "w_o": w_in[:, o3:o4].astype(BF16),
        "w_g": jnp.pad(w_g, ((0, 0), (0, GATE_PAD - N_GATE))).astype(BF16),
        "w_gt": w_g.T.astype(BF16),
        "b_g": jnp.pad(b_gate, (0, GATE_PAD - N_GATE)).reshape(1, GATE_PAD),
        "b_gt": b_gate.reshape(N_GATE, 1),
    }


def kernel(x, c, ctx, c_ctx, w_mod, b_mod, g_mix, g_ffn, a_w_in, a_b_gate,
           a_head_gain, a_w_out, b_w_in, b_w_conv, b_w_out, f_w_up, f_w_conv,
           f_b_conv, f_w_down, g_final):
    bn, t, d = x.shape
    n_ctx = ctx.shape[1]
    depth = w_mod.shape[0]
    n_mixers = 2
    assert d == D_MODEL and bn <= 7 and t % TOKEN_TILE == 0
    assert n_ctx % SCAN_CHUNK == 0 and TOKEN_TILE % SCAN_CHUNK == 0

    rec_layers = [i for i in range(depth) if i % n_mixers == 0]
    last_rec = max(rec_layers) if rec_layers else -1

    cc = jnp.zeros((8, d), F32).at[:bn].set(c).at[bn].set(c_ctx)
    mod = _mod_all(cc, w_mod, b_mod).reshape(depth, 8, 6, d)

    tm = TOKEN_TILE
    tiles_per_batch = t // tm
    lat_row = lambda i: i // tiles_per_batch
    ctx_row = lambda i: bn

    xs = x.reshape(bn * t, d)
    cs = ctx.reshape(bn * n_ctx, d)
    zero_state = (jnp.zeros((bn, N_HEADS, DK, DV + DK), F32),
                  jnp.zeros((bn, N_HEADS, 1, DK), F32))

    a_w_out_b = a_w_out.astype(BF16)
    b_w_in_b = b_w_in.astype(BF16)
    b_w_out_b = b_w_out.astype(BF16)
    f_w_up_b = f_w_up.astype(BF16)
    f_w_down_b = f_w_down.astype(BF16)

    for i in range(depth):
        j = i // n_mixers
        ctx_read = i <= last_rec
        ctx_live = i < last_rec
        gm = g_mix[i].reshape(1, d)
        gf = g_ffn[i].reshape(1, d)
        if i % n_mixers == 0:
            wts = _mlstm_weights(a_w_in[j], a_b_gate[j])
            w_out = (a_w_out_b, j)
            gain = a_head_gain[j].reshape(1, N_HEADS * DV)
            s_f = s_b = zero_state
            if ctx_read:
                pc = _mlstm_proj(cs, mod[i], gm, wts, tm=n_ctx, row_fn=ctx_row)
                hc, s_b = _scan(*pc, zero_state, batch=bn, ts=n_ctx, rev=True)
                ro = ((cs, mod[i], lambda b: bn, gm, wts["w_o"], gain, w_out)
                      if ctx_live else None)
                cs_new, s_f = _scan(*pc, zero_state, batch=bn, ts=n_ctx,
                                    rev=False, add=hc, readout=ro)
            px = _mlstm_proj(xs, mod[i], gm, wts, tm=tm, row_fn=lat_row)
            hl, _ = _scan(*px, s_b, batch=bn, ts=tm, rev=True)
            ro = (xs, mod[i], lambda b: b, gm, wts["w_o"], gain, w_out)
            xs, _ = _scan(*px, s_f, batch=bn, ts=tm, rev=False, add=hl,
                          readout=ro)
            if ctx_live:
                cs = cs_new
        else:
            w_in = (b_w_in_b, j)
            w_out = (b_w_out_b, j)
            xs = _shortconv(xs, mod[i], gm, w_in, b_w_conv[j], w_out,
                            tm=tm, seg=GRID_W, row_fn=lat_row)
            if ctx_live:
                cs = _shortconv(cs, mod[i], gm, w_in, b_w_conv[j], w_out,
                                tm=n_ctx, seg=n_ctx, row_fn=ctx_row)
        w_up = (f_w_up_b, i)
        w_down = (f_w_down_b, i)
        b_conv = f_b_conv[i].reshape(1, D_FF)
        gfin = g_final.reshape(1, d)
        xs = _ffn(xs, mod[i], gf, w_up, f_w_conv[i], b_conv, w_down, gfin,
                  batch=bn, strip=FFN_STRIP, final=(i == depth - 1))
        if ctx_live:
            cs = _ffn(cs, mod[i], gf, w_up, f_w_conv[i], b_conv, w_down, gfin,
                      batch=bn, strip=0, final=False)

    return xs.reshape(bn, t, d)
```

```python
import functools

import jax
import jax.numpy as jnp
from jax import lax
from jax.experimental import pallas as pl
from jax.experimental.pallas import tpu as pltpu

F32 = jnp.float32
BF16 = jnp.bfloat16

D_MODEL = 1024
GRID_W = 64
N_HEADS = 4
DK = 128
DV = 256
D_FF = 2816
EPS = 1e-6
N_GATE = 16
GATE_PAD = 128

SCAN_CHUNK = 128
FF_CHUNK = 256
DOWN_PIECES = 2
ROW_PIECES = 2
TOKEN_TILE = 1024
FFN_STRIP = 8
VMEM_LIMIT = 56 * 1024 * 1024


def _params(n_axes):
    return pltpu.CompilerParams(
        dimension_semantics=("arbitrary",) * n_axes,
        vmem_limit_bytes=VMEM_LIMIT)


def _const_spec(shape):
    zeros = (0,) * len(shape)
    return pl.BlockSpec(shape, lambda *_: zeros, pipeline_mode=pl.Buffered(1))


def _weight(w):
    if isinstance(w, tuple):
        arr, layer = w
        idx = (layer,) + (0,) * (arr.ndim - 1)
        spec = pl.BlockSpec((None,) + arr.shape[1:], lambda *_: idx,
                            pipeline_mode=pl.Buffered(1))
        return spec, arr
    return _const_spec(w.shape), w


def _dot(a, b):
    return jnp.dot(a, b, preferred_element_type=F32)


def _sigmoid(v):
    return 1.0 / (1.0 + jnp.exp(-v))


def _rmsnorm(x, g):
    ms = jnp.mean(x * x, axis=-1, keepdims=True)
    return (x * lax.rsqrt(ms + EPS)) * g


def _modulate(x, g, shift, scale):
    return _rmsnorm(x, g) * (1.0 + scale) + shift


def _shift_conv(u, w_ref, seg, shift=1):
    n = u.shape[0]
    if seg == n and shift % 8 == 0:
        pad = jnp.zeros((shift, u.shape[1]), u.dtype)
        prev = jnp.concatenate([pad, u[:n - shift]], axis=0)
        nxt = jnp.concatenate([u[shift:], pad], axis=0)
    else:
        pos = lax.broadcasted_iota(jnp.int32, (n, 1), 0) & (seg - 1)
        prev = jnp.where(pos >= shift, pltpu.roll(u, shift, axis=0), 0.0)
        nxt = jnp.where(pos < seg - shift, pltpu.roll(u, n - shift, axis=0), 0.0)
    return prev * w_ref[0:1, :] + u * w_ref[1:2, :] + nxt * w_ref[2:3, :]


def _mod_kernel(cc_ref, w_ref, b_ref, o_ref):
    cc = cc_ref[...]
    s = cc * _sigmoid(cc)
    o_ref[...] = _dot(s, w_ref[...]) + b_ref[...]


def _mod_all(cc, w_mod, b_mod):
    depth, d, n = w_mod.shape
    tn = 1536
    return pl.pallas_call(
        _mod_kernel,
        grid=(depth, n // tn),
        in_specs=[
            pl.BlockSpec((8, d), lambda l, j: (0, 0)),
            pl.BlockSpec((None, d, tn), lambda l, j: (l, 0, j)),
            pl.BlockSpec((None, 1, tn), lambda l, j: (l, 0, j)),
        ],
        out_specs=pl.BlockSpec((None, 8, tn), lambda l, j: (l, 0, j)),
        out_shape=jax.ShapeDtypeStruct((depth, 8, n), F32),
        compiler_params=_params(2),
        name="adaln_mod",
    )(cc, w_mod, b_mod.reshape(depth, 1, n))


def _shortconv_kernel(x_ref, mod_ref, g_ref, win_ref, wconv_ref, wout_ref,
                      o_ref, z_ref, *, seg):
    d = D_MODEL
    cw = 256
    tm = x_ref.shape[0]
    pieces = ROW_PIECES if (tm // ROW_PIECES) % seg == 0 else 1
    pr = tm // pieces
    for p in range(pieces):
        rows = slice(p * pr, (p + 1) * pr)
        x = x_ref[rows, :]
        hx = _modulate(x, g_ref[...], mod_ref[0:1, :], mod_ref[1:2, :]).astype(BF16)
        for j in range(d // cw):
            lo, hi = j * cw, (j + 1) * cw
            bg = _dot(hx, win_ref[:, lo:hi])
            cg = _dot(hx, win_ref[:, d + lo:d + hi])
            xv = _dot(hx, win_ref[:, 2 * d + lo:2 * d + hi])
            cv = _shift_conv(cg * xv, wconv_ref.at[:, lo:hi], seg)
            z_ref[rows, lo:hi] = (bg * cv).astype(BF16)
        y = _dot(z_ref[rows, :], wout_ref[...])
        o_ref[rows, :] = x + mod_ref[2:3, :] * y


def _shortconv(x2d, mod_l, g, w_in, w_conv, w_out, *, tm, seg, row_fn):
    n, d = x2d.shape
    win_spec, w_in = _weight(w_in)
    wout_spec, w_out = _weight(w_out)
    return pl.pallas_call(
        functools.partial(_shortconv_kernel, seg=seg),
        grid=(n // tm,),
        in_specs=[
            pl.BlockSpec((tm, d), lambda i: (i, 0)),
            pl.BlockSpec((None, 6, d), lambda i: (row_fn(i), 0, 0)),
            _const_spec((1, d)),
            win_spec,
            _const_spec((3, d)),
            wout_spec,
        ],
        out_specs=pl.BlockSpec((tm, d), lambda i: (i, 0)),
        out_shape=jax.ShapeDtypeStruct((n, d), F32),
        scratch_shapes=[pltpu.VMEM((tm, d), BF16)],
        compiler_params=_params(1),
        name="shortconv_mixer",
    )(x2d, mod_l, g, w_in, w_conv, w_out)


def _ffn_kernel(x_ref, mod_ref, g_ref, wup_ref, wconv_ref, bconv_ref, wdown_ref,
                gfin_ref, o_ref, act_ref, *, shift, seg, final):
    d = x_ref.shape[-1]
    x = x_ref[...].reshape(-1, d)
    hx = _modulate(x, g_ref[...], mod_ref[3:4, :], mod_ref[4:5, :]).astype(BF16)
    for c in range(D_FF // FF_CHUNK):
        lo, hi = c * FF_CHUNK, (c + 1) * FF_CHUNK
        val = _dot(hx, wup_ref[:, D_FF + lo:D_FF + hi])
        gate = _shift_conv(_dot(hx, wup_ref[:, lo:hi]), wconv_ref.at[:, lo:hi],
                           seg, shift)
        gate = gate + bconv_ref[:, lo:hi]
        act_ref[:, lo:hi] = (gate * _sigmoid(gate) * val).astype(BF16)
    tm = x.shape[0]
    pr = tm // DOWN_PIECES
    lead = o_ref.shape[0] // DOWN_PIECES
    for p in range(DOWN_PIECES):
        rows = slice(p * pr, (p + 1) * pr)
        y = _dot(act_ref[rows, :], wdown_ref[...])
        out = x[rows] + mod_ref[5:6, :] * y
        if final:
            out = _rmsnorm(out, gfin_ref[...])
        o_ref[p * lead:(p + 1) * lead] = out.reshape((lead,) + o_ref.shape[1:])


def _ffn(x2d, mod_l, g, w_up, w_conv, b_conv, w_down, g_final, *, batch,
         strip, final):
    n, d = x2d.shape
    wup_spec, w_up = _weight(w_up)
    wdown_spec, w_down = _weight(w_down)
    per = n // batch
    if strip:
        rows = per // GRID_W
        strips = GRID_W // strip
        tm = rows * strip
        xin = x2d.reshape(batch * rows, GRID_W, d)
        x_spec = pl.BlockSpec((rows, strip, d),
                              lambda i: (i // strips, i % strips, 0))
        grid = (batch * strips,)
        row_fn = lambda i: i // strips
        shift = strip
    else:
        tm = per
        xin = x2d
        x_spec = pl.BlockSpec((tm, d), lambda i: (i, 0))
        grid = (batch,)
        row_fn = lambda i: batch
        shift = 1
    out = pl.pallas_call(
        functools.partial(_ffn_kernel, shift=shift, seg=tm, final=final),
        grid=grid,
        in_specs=[
            x_spec,
            pl.BlockSpec((None, 6, d), lambda i: (row_fn(i), 0, 0)),
            _const_spec((1, d)),
            wup_spec,
            _const_spec((3, D_FF)),
            _const_spec((1, D_FF)),
            wdown_spec,
            _const_spec((1, d)),
        ],
        out_specs=x_spec,
        out_shape=jax.ShapeDtypeStruct(xin.shape, F32),
        scratch_shapes=[pltpu.VMEM((tm, D_FF), BF16)],
        compiler_params=_params(1),
        name="convglu_ffn",
    )(xin, mod_l, g, w_up, w_conv, b_conv, w_down, g_final)
    return out.reshape(n, d)


def _log_sigmoid(v):
    return -(jnp.maximum(-v, 0.0) + jnp.log(1.0 + jnp.exp(-jnp.abs(v))))


def _chunk_scans(v, axis, chunk, op, identity):
    n = v.shape[axis]
    shape = (n, 1) if axis == 0 else (1, n)
    pos = lax.broadcasted_iota(jnp.int32, shape, axis) & (chunk - 1)
    pre, suf = v, v
    step = 1
    while step < chunk:
        pre = op(pre, jnp.where(pos >= step,
                                pltpu.roll(pre, step, axis=axis), identity))
        suf = op(suf, jnp.where(pos < chunk - step,
                                pltpu.roll(suf, n - step, axis=axis), identity))
        step *= 2
    return pre, suf


def _mlstm_proj_kernel(x_ref, mod_ref, g_ref, wqv_ref, wkt_ref, wg_ref,
                       wgt_ref, bg_ref, bgt_ref,
                       qv_ref, kt_ref, gc_ref, gr_ref):
    nt = (((1,), (1,)), ((), ()))
    half = N_GATE // 2
    lane = lax.broadcasted_iota(jnp.int32, (1, GATE_PAD), 1)
    fwd = (lane & (half - 1)) < N_HEADS
    sub = lax.broadcasted_iota(jnp.int32, (N_GATE, 1), 0)
    tm = x_ref.shape[0]
    pieces = ROW_PIECES if (tm // ROW_PIECES) % SCAN_CHUNK == 0 else 1
    pr = tm // pieces
    for p in range(pieces):
        rows = slice(p * pr, (p + 1) * pr)
        hx = _modulate(x_ref[rows, :], g_ref[...], mod_ref[0:1, :],
                       mod_ref[1:2, :]).astype(BF16)

        gc = _dot(hx, wg_ref[...]) + bg_ref[...]
        pre, suf = _chunk_scans(_log_sigmoid(gc), 0, SCAN_CHUNK, jnp.add, 0.0)
        b = jnp.where(fwd, pre, suf)
        u = gc - pltpu.roll(b, GATE_PAD - half, axis=1)
        pre, suf = _chunk_scans(u, 0, SCAN_CHUNK, jnp.maximum, -jnp.inf)
        gc_ref[rows, :] = jnp.where(lane < half, jnp.where(fwd, pre, suf), b)

        gr = lax.dot_general(wgt_ref[...], hx, nt, preferred_element_type=F32)
        gr = gr + bgt_ref[...]
        pre, suf = _chunk_scans(_log_sigmoid(gr), 1, SCAN_CHUNK, jnp.add, 0.0)
        b = jnp.where((sub & (half - 1)) < N_HEADS, pre, suf)
        gr_ref[0:half, rows] = gr[0:half] - b[half:]
        gr_ref[half:, rows] = b[half:]

        qv_ref[rows, :] = _dot(hx, wqv_ref[...]).astype(BF16)
        kt = lax.dot_general(wkt_ref[...], hx, nt, preferred_element_type=F32)
        kt_ref[:, rows] = (kt * (DK ** -0.5)).astype(BF16)


def _scan_kernel(*refs, rev, has_add, readout):
    it = iter(refs)
    qv_ref, kt_ref, gc_ref, gr_ref, c0_ref, m0_ref = (next(it) for _ in range(6))
    hadd_ref = next(it) if has_add else None
    if readout:
        x_ref, mod_ref, g_ref, wo_ref, gain_ref, wout_ref = (
            next(it) for _ in range(6))
    out_ref, cf_ref, mf_ref = (next(it) for _ in range(3))
    c_s, m_s, a_s, x_s, s_s, w_s, ktw_s = (next(it) for _ in range(7))
    if readout:
        h_s, z_s = next(it), next(it)
    h_ref = h_s if readout else out_ref
    assert not (rev and readout)
    t = pl.program_id(1)
    n_t = pl.num_programs(1)
    L = SCAN_CHUNK
    nq = N_HEADS * DK
    n_chunks = qv_ref.shape[0] // L
    half = N_GATE // 2

    @pl.when(t == 0)
    def _():
        c_s[...] = c0_ref[...]
        m_s[...] = m0_ref[...]

    ti = lax.broadcasted_iota(jnp.int32, (L, L), 0)
    si = lax.broadcasted_iota(jnp.int32, (L, L), 1)
    mask = (si >= ti) if rev else (si <= ti)
    gcol = N_HEADS if rev else 0
    ones = jnp.ones((L, DK), BF16)
    order = range(n_chunks - 1, -1, -1) if rev else range(n_chunks)

    units = [(c, h) for c in order for h in range(N_HEADS)]

    m_in, mx_of, decay_of = {}, {}, {}
    for h in range(N_HEADS):
        gi = gcol + h
        m_prev = m_s[h]
        for c in order:
            r0, r1 = c * L, (c + 1) * L
            u_row = gr_ref[gi:gi + 1, r0:r1]
            b_row = gr_ref[half + gi:half + gi + 1, r0:r1]
            g = b_row[:, 0:1] if rev else b_row[:, L - 1:L]
            mx = jnp.maximum(m_prev, jnp.max(u_row, axis=1, keepdims=True))
            m_in[c, h], mx_of[c, h] = m_prev, mx
            decay_of[c, h] = jnp.exp(m_prev - mx)
            m_prev = g + mx
        m_s[h] = m_prev

    for c, h in units:
        r0, r1 = c * L, (c + 1) * L
        q = qv_ref[r0:r1, h * DK:(h + 1) * DK]
        kt = kt_ref[h * DK:(h + 1) * DK, r0:r1]
        s_s[c, h] = _dot(q, kt)
        u_row = gr_ref[gcol + h:gcol + h + 1, r0:r1]
        ktw_s[c, h] = (kt.astype(F32) * jnp.exp(u_row - mx_of[c, h])).astype(BF16)
    o_piece = 2 if n_chunks % 2 == 0 else 1
    for i, (c, h) in enumerate(units):
        r0, r1 = c * L, (c + 1) * L
        gi = gcol + h
        u_row = gr_ref[gi:gi + 1, r0:r1]
        cm_bc = jnp.broadcast_to(gc_ref[r0:r1, gi:gi + 1], (L, DK))
        r = jnp.maximum(m_in[c, h], cm_bc)
        w = jnp.exp(jnp.where(mask, u_row - r, -jnp.inf)) * s_s[c, h]
        w_s[c, h] = w.astype(BF16)
        if readout and i % (o_piece * N_HEADS) == 0:
            p0 = (i // (o_piece * N_HEADS)) * o_piece * L
            rows = slice(p0, p0 + o_piece * L)
            hx = _modulate(x_ref[rows, :], g_ref[...], mod_ref[0:1, :],
                           mod_ref[1:2, :]).astype(BF16)
            out_ref[rows, :] = _dot(hx, wo_ref[...])

    for c in order:
        r0, r1 = c * L, (c + 1) * L
        for h in range(N_HEADS):
            v = qv_ref[r0:r1, nq + h * DV:nq + (h + 1) * DV]
            v_aug = jnp.concatenate([v, ones], axis=1)
            a_s[h] = _dot(w_s[c, h], v_aug)
            x_s[h] = _dot(ktw_s[c, h], v_aug)
        for h in range(N_HEADS):
            gi = gcol + h
            q = qv_ref[r0:r1, h * DK:(h + 1) * DK]
            m_prev = m_in[c, h]
            c_prev = c_s[h]
            a_aug = a_s[h]
            b_aug = _dot(q, c_prev.astype(BF16))
            cm_bc = jnp.broadcast_to(gc_ref[r0:r1, gi:gi + 1], (L, DK))
            b_bc = jnp.broadcast_to(gc_ref[r0:r1, half + gi:half + gi + 1],
                                    (L, DK))
            r = jnp.maximum(m_prev, cm_bc)
            wi = jnp.exp(m_prev - r)
            den = a_aug[:, DV:] + wi * b_aug[:, DV:]
            inv = 1.0 / jnp.maximum(jnp.abs(den), jnp.exp(-(b_bc + r)))
            beta = wi * inv
            for j in range(DV // DK):
                lo, hi = j * DK, (j + 1) * DK
                cols = slice(h * DV + lo, h * DV + hi)
                hh = inv * a_aug[:, lo:hi] + beta * b_aug[:, lo:hi]
                if has_add:
                    hh = hh + hadd_ref[r0:r1, cols]
                h_ref[r0:r1, cols] = hh
            for j in range((DV + DK) // DK):
                lo, hi = j * DK, (j + 1) * DK
                c_s[h, :, lo:hi] = (decay_of[c, h] * c_prev[:, lo:hi]
                                    + x_s[h, :, lo:hi])

    @pl.when(t == n_t - 1)
    def _():
        cf_ref[...] = c_s[...]
        mf_ref[...] = m_s[...]

    if readout:
        pr = o_piece * L
        for p in range(n_chunks // o_piece):
            rows = slice(p * pr, (p + 1) * pr)
            for h in range(N_HEADS):
                lo, hi = h * DV, (h + 1) * DV
                hh = h_s[rows, lo:hi]
                ms = jnp.mean(hh * hh, axis=-1, keepdims=True)
                hn = hh * lax.rsqrt(ms + EPS) * gain_ref[:, lo:hi]
                z_s[rows, lo:hi] = (hn * _sigmoid(out_ref[rows, lo:hi])
                                    ).astype(BF16)
            y = _dot(z_s[rows, :], wout_ref[...])
            out_ref[rows, :] = x_ref[rows, :] + mod_ref[2:3, :] * y


def _scan_scratch(n_chunks):
    return [
        pltpu.VMEM((N_HEADS, DK, DV + DK), F32),
        pltpu.VMEM((N_HEADS, 1, DK), F32),
        pltpu.VMEM((N_HEADS, SCAN_CHUNK, DV + DK), F32),
        pltpu.VMEM((N_HEADS, DK, DV + DK), F32),
        pltpu.VMEM((n_chunks, N_HEADS, SCAN_CHUNK, SCAN_CHUNK), F32),
        pltpu.VMEM((n_chunks, N_HEADS, SCAN_CHUNK, SCAN_CHUNK), BF16),
        pltpu.VMEM((n_chunks, N_HEADS, DK, SCAN_CHUNK), BF16),
    ]


def _scan(qv, kt, gc, gr, state, *, batch, ts, rev, add=None, readout=None):
    n = qv.shape[0]
    n_t = n // batch // ts
    n_chunks = ts // SCAN_CHUNK
    nq = N_HEADS * DK
    nv = N_HEADS * DV
    nqv = nq + nv

    def blk(b, t):
        return b * n_t + ((n_t - 1 - t) if rev else t)

    st_shapes = [(N_HEADS, DK, DV + DK), (N_HEADS, 1, DK)]
    st_specs = [pl.BlockSpec((None,) + s, lambda b, t: (b, 0, 0, 0))
                for s in st_shapes]
    h_spec = pl.BlockSpec((ts, nv), lambda b, t: (blk(b, t), 0))
    has_add = add is not None
    in_specs = [
        pl.BlockSpec((ts, nqv), lambda b, t: (blk(b, t), 0)),
        pl.BlockSpec((nq, ts), lambda b, t: (0, blk(b, t))),
        pl.BlockSpec((ts, GATE_PAD), lambda b, t: (blk(b, t), 0)),
        pl.BlockSpec((N_GATE, ts), lambda b, t: (0, blk(b, t))),
    ] + st_specs
    args = [qv, kt, gc, gr, *state]
    scratch = _scan_scratch(n_chunks)
    if has_add:
        in_specs.append(h_spec)
        args.append(add)
    out_cols = nv
    if readout is not None:
        x2d, mod_l, mod_row_fn, g, w_o, gain, w_out = readout
        wout_spec, w_out = _weight(w_out)
        d = x2d.shape[1]
        out_cols = d
        in_specs += [
            pl.BlockSpec((ts, d), lambda b, t: (blk(b, t), 0)),
            pl.BlockSpec((None, 6, d), lambda b, t: (mod_row_fn(b), 0, 0)),
            _const_spec((1, d)),
            _const_spec((d, nv)),
            _const_spec((1, nv)),
            wout_spec,
        ]
        args += [x2d, mod_l, g, w_o, gain, w_out]
        scratch += [pltpu.VMEM((ts, nv), F32), pltpu.VMEM((ts, nv), BF16)]
    out, cf, mf = pl.pallas_call(
        functools.partial(_scan_kernel, rev=rev, has_add=has_add,
                          readout=readout is not None),
        grid=(batch, n_t),
        in_specs=in_specs,
        out_specs=[pl.BlockSpec((ts, out_cols), lambda b, t: (blk(b, t), 0))]
        + st_specs,
        out_shape=[jax.ShapeDtypeStruct((n, out_cols), F32)]
        + [jax.ShapeDtypeStruct((batch,) + s, F32) for s in st_shapes],
        scratch_shapes=scratch,
        compiler_params=_params(2),
        name="mlstm_scan_bwd" if rev else "mlstm_scan_fwd",
    )(*args)
    return out, (cf, mf)


N_PROJ_IN = 9
N_PROJ_OUT = 4


def _proj_scan_kernel(*refs):
    proj_in = refs[:N_PROJ_IN]
    c0_ref, m0_ref = refs[N_PROJ_IN:N_PROJ_IN + 2]
    outs = refs[N_PROJ_IN + 2:]
    proj_out = outs[:N_PROJ_OUT]
    _mlstm_proj_kernel(*proj_in, *proj_out)
    _scan_kernel(*proj_out, c0_ref, m0_ref, *outs[N_PROJ_OUT:], rev=True,
                 has_add=False, readout=False)


def _proj_scan(x2d, mod_l, mod_row_fn, g, wts, state, *, batch, ts):
    n, d = x2d.shape
    n_t = n // batch // ts
    n_chunks = ts // SCAN_CHUNK
    nq = N_HEADS * DK
    nv = N_HEADS * DV
    nqv = nq + nv

    def blk(b, t):
        return b * n_t + (n_t - 1 - t)

    rows = lambda w: pl.BlockSpec((ts, w), lambda b, t: (blk(b, t), 0))
    cols = lambda h: pl.BlockSpec((h, ts), lambda b, t: (0, blk(b, t)))
    st_shapes = [(N_HEADS, DK, DV + DK), (N_HEADS, 1, DK)]
    st_specs = [pl.BlockSpec((None,) + s, lambda b, t: (b, 0, 0, 0))
                for s in st_shapes]
    qv, kt, gc, gr, h, cf, mf = pl.pallas_call(
        _proj_scan_kernel,
        grid=(batch, n_t),
        in_specs=[
            rows(d),
            pl.BlockSpec((None, 6, d), lambda b, t: (mod_row_fn(b), 0, 0)),
            _const_spec((1, d)),
            _const_spec((d, nqv)),
            _const_spec((nq, d)),
            _const_spec((d, GATE_PAD)),
            _const_spec((N_GATE, d)),
            _const_spec((1, GATE_PAD)),
            _const_spec((N_GATE, 1)),
        ] + st_specs,
        out_specs=[rows(nqv), cols(nq), rows(GATE_PAD), cols(N_GATE), rows(nv)]
        + st_specs,
        out_shape=[
            jax.ShapeDtypeStruct((n, nqv), BF16),
            jax.ShapeDtypeStruct((nq, n), BF16),
            jax.ShapeDtypeStruct((n, GATE_PAD), F32),
            jax.ShapeDtypeStruct((N_GATE, n), F32),
            jax.ShapeDtypeStruct((n, nv), F32),
        ] + [jax.ShapeDtypeStruct((batch,) + s, F32) for s in st_shapes],
        scratch_shapes=_scan_scratch(n_chunks),
        compiler_params=_params(2),
        name="mlstm_proj_scan_bwd",
    )(x2d, mod_l, g, wts["w_qv"], wts["w_kt"], wts["w_g"], wts["w_gt"],
      wts["b_g"], wts["b_gt"], *state)
    return (qv, kt, gc, gr), h, (cf, mf)


def _mlstm_weights(w_in, b_gate):
    nq = N_HEADS * DK
    o2 = 2 * nq
    o3 = o2 + N_HEADS * DV
    o4 = o3 + D_MODEL
    w_g = w_in[:, o4:]
    return {
        "w_qv": jnp.concatenate([w_in[:, :nq], w_in[:, o2:o3]], axis=1).astype(BF16),
        "w_kt": w_in[:, nq:o2].T.astype(BF16),
        "w_o": w_in[:, o3:o4].astype(BF16),
        "w_g": jnp.pad(w_g, ((0, 0), (0, GATE_PAD - N_GATE))).astype(BF16),
        "w_gt": w_g.T.astype(BF16),
        "b_g": jnp.pad(b_gate, (0, GATE_PAD - N_GATE)).reshape(1, GATE_PAD),
        "b_gt": b_gate.reshape(N_GATE, 1),
    }


def kernel(x, c, ctx, c_ctx, w_mod, b_mod, g_mix, g_ffn, a_w_in, a_b_gate,
           a_head_gain, a_w_out, b_w_in, b_w_conv, b_w_out, f_w_up, f_w_conv,
           f_b_conv, f_w_down, g_final):
    bn, t, d = x.shape
    n_ctx = ctx.shape[1]
    depth = w_mod.shape[0]
    n_mixers = 2
    assert d == D_MODEL and bn <= 7 and t % TOKEN_TILE == 0
    assert n_ctx % SCAN_CHUNK == 0 and TOKEN_TILE % SCAN_CHUNK == 0

    rec_layers = [i for i in range(depth) if i % n_mixers == 0]
    last_rec = max(rec_layers) if rec_layers else -1

    cc = jnp.zeros((8, d), F32).at[:bn].set(c).at[bn].set(c_ctx)
    mod = _mod_all(cc, w_mod, b_mod).reshape(depth, 8, 6, d)

    tm = TOKEN_TILE
    tiles_per_batch = t // tm
    lat_row = lambda i: i // tiles_per_batch
    ctx_row = lambda i: bn

    xs = x.reshape(bn * t, d)
    cs = ctx.reshape(bn * n_ctx, d)
    zero_state = (jnp.zeros((bn, N_HEADS, DK, DV + DK), F32),
                  jnp.zeros((bn, N_HEADS, 1, DK), F32))

    a_w_out_b = a_w_out.astype(BF16)
    b_w_in_b = b_w_in.astype(BF16)
    b_w_out_b = b_w_out.astype(BF16)
    f_w_up_b = f_w_up.astype(BF16)
    f_w_down_b = f_w_down.astype(BF16)

    for i in range(depth):
        j = i // n_mixers
        ctx_read = i <= last_rec
        ctx_live = i < last_rec
        gm = g_mix[i].reshape(1, d)
        gf = g_ffn[i].reshape(1, d)
        if i % n_mixers == 0:
            wts = _mlstm_weights(a_w_in[j], a_b_gate[j])
            w_out = (a_w_out_b, j)
            gain = a_head_gain[j].reshape(1, N_HEADS * DV)
            s_f = s_b = zero_state
            if ctx_read:
                pc, hc, s_b = _proj_scan(cs, mod[i], lambda b: bn, gm, wts,
                                         zero_state, batch=bn, ts=n_ctx)
                ro = ((cs, mod[i], lambda b: bn, gm, wts["w_o"], gain, w_out)
                      if ctx_live else None)
                cs_new, s_f = _scan(*pc, zero_state, batch=bn, ts=n_ctx,
                                    rev=False, add=hc, readout=ro)
            px, hl, _ = _proj_scan(xs, mod[i], lambda b: b, gm, wts, s_b,
                                   batch=bn, ts=tm)
            ro = (xs, mod[i], lambda b: b, gm, wts["w_o"], gain, w_out)
            xs, _ = _scan(*px, s_f, batch=bn, ts=tm, rev=False, add=hl,
                          readout=ro)
            if ctx_live:
                cs = cs_new
        else:
            w_in = (b_w_in_b, j)
            w_out = (b_w_out_b, j)
            xs = _shortconv(xs, mod[i], gm, w_in, b_w_conv[j], w_out,
                            tm=tm, seg=GRID_W, row_fn=lat_row)
            if ctx_live:
                cs = _shortconv(cs, mod[i], gm, w_in, b_w_conv[j], w_out,
                                tm=n_ctx, seg=n_ctx, row_fn=ctx_row)
        w_up = (f_w_up_b, i)
        w_down = (f_w_down_b, i)
        b_conv = f_b_conv[i].reshape(1, D_FF)
        gfin = g_final.reshape(1, d)
        xs = _ffn(xs, mod[i], gf, w_up, f_w_conv[i], b_conv, w_down, gfin,
                  batch=bn, strip=FFN_STRIP, final=(i == depth - 1))
        if ctx_live:
            cs = _ffn(cs, mod[i], gf, w_up, f_w_conv[i], b_conv, w_down, gfin,
                      batch=bn, strip=0, final=False)

    return xs.reshape(bn, t, d)
```

```python
import functools

import jax
import jax.numpy as jnp
from jax import lax
from jax.experimental import pallas as pl
from jax.experimental.pallas import tpu as pltpu

F32 = jnp.float32
BF16 = jnp.bfloat16

D_MODEL = 1024
GRID_W = 64
N_HEADS = 4
DK = 128
DV = 256
D_FF = 2816
EPS = 1e-6
N_GATE = 16
GATE_PAD = 128

SCAN_CHUNK = 128
FF_CHUNK = 256
DOWN_PIECES = 2
ROW_PIECES = 2
TOKEN_TILE = 1024
FFN_STRIP = 8
VMEM_LIMIT = 56 * 1024 * 1024


def _params(n_axes):
    return pltpu.CompilerParams(
        dimension_semantics=("arbitrary",) * n_axes,
        vmem_limit_bytes=VMEM_LIMIT)


def _const_spec(shape):
    zeros = (0,) * len(shape)
    return pl.BlockSpec(shape, lambda *_: zeros, pipeline_mode=pl.Buffered(1))


def _weight(w):
    if isinstance(w, tuple):
        arr, layer = w
        idx = (layer,) + (0,) * (arr.ndim - 1)
        spec = pl.BlockSpec((None,) + arr.shape[1:], lambda *_: idx,
                            pipeline_mode=pl.Buffered(1))
        return spec, arr
    return _const_spec(w.shape), w


def _dot(a, b):
    return jnp.dot(a, b, preferred_element_type=F32)


def _sigmoid(v):
    return 1.0 / (1.0 + jnp.exp(-v))


def _rmsnorm(x, g):
    ms = jnp.mean(x * x, axis=-1, keepdims=True)
    return (x * lax.rsqrt(ms + EPS)) * g


def _modulate(x, g, shift, scale):
    return _rmsnorm(x, g) * (1.0 + scale) + shift


def _shift_conv(u, w_ref, seg, shift=1):
    n = u.shape[0]
    if seg == n and shift % 8 == 0:
        pad = jnp.zeros((shift, u.shape[1]), u.dtype)
        prev = jnp.concatenate([pad, u[:n - shift]], axis=0)
        nxt = jnp.concatenate([u[shift:], pad], axis=0)
    else:
        pos = lax.broadcasted_iota(jnp.int32, (n, 1), 0) & (seg - 1)
        prev = jnp.where(pos >= shift, pltpu.roll(u, shift, axis=0), 0.0)
        nxt = jnp.where(pos < seg - shift, pltpu.roll(u, n - shift, axis=0), 0.0)
    return prev * w_ref[0:1, :] + u * w_ref[1:2, :] + nxt * w_ref[2:3, :]


def _mod_kernel(cc_ref, w_ref, b_ref, o_ref):
    cc = cc_ref[...]
    s = cc * _sigmoid(cc)
    o_ref[...] = _dot(s, w_ref[...]) + b_ref[...]


def _mod_all(cc, w_mod, b_mod):
    depth, d, n = w_mod.shape
    tn = 3072
    return pl.pallas_call(
        _mod_kernel,
        grid=(depth, n // tn),
        in_specs=[
            pl.BlockSpec((8, d), lambda l, j: (0, 0)),
            pl.BlockSpec((None, d, tn), lambda l, j: (l, 0, j)),
            pl.BlockSpec((None, 1, tn), lambda l, j: (l, 0, j)),
        ],
        out_specs=pl.BlockSpec((None, 8, tn), lambda l, j: (l, 0, j)),
        out_shape=jax.ShapeDtypeStruct((depth, 8, n), F32),
        compiler_params=_params(2),
        name="adaln_mod",
    )(cc, w_mod, b_mod.reshape(depth, 1, n))


def _shortconv_kernel(x_ref, mod_ref, g_ref, win_ref, wconv_ref, wout_ref,
                      o_ref, z_ref, *, seg):
    d = D_MODEL
    cw = 256
    tm = x_ref.shape[0]
    pieces = ROW_PIECES if (tm // ROW_PIECES) % seg == 0 else 1
    pr = tm // pieces
    for p in range(pieces):
        rows = slice(p * pr, (p + 1) * pr)
        x = x_ref[rows, :]
        hx = _modulate(x, g_ref[...], mod_ref[0:1, :], mod_ref[1:2, :]).astype(BF16)
        for j in range(d // cw):
            lo, hi = j * cw, (j + 1) * cw
            bg = _dot(hx, win_ref[:, lo:hi])
            cg = _dot(hx, win_ref[:, d + lo:d + hi])
            xv = _dot(hx, win_ref[:, 2 * d + lo:2 * d + hi])
            cv = _shift_conv(cg * xv, wconv_ref.at[:, lo:hi], seg)
            z_ref[rows, lo:hi] = (bg * cv).astype(BF16)
        y = _dot(z_ref[rows, :], wout_ref[...])
        o_ref[rows, :] = x + mod_ref[2:3, :] * y


def _shortconv(x2d, mod_l, g, w_in, w_conv, w_out, *, tm, seg, row_fn):
    n, d = x2d.shape
    win_spec, w_in = _weight(w_in)
    wout_spec, w_out = _weight(w_out)
    return pl.pallas_call(
        functools.partial(_shortconv_kernel, seg=seg),
        grid=(n // tm,),
        in_specs=[
            pl.BlockSpec((tm, d), lambda i: (i, 0)),
            pl.BlockSpec((None, 6, d), lambda i: (row_fn(i), 0, 0)),
            _const_spec((1, d)),
            win_spec,
            _const_spec((3, d)),
            wout_spec,
        ],
        out_specs=pl.BlockSpec((tm, d), lambda i: (i, 0)),
        out_shape=jax.ShapeDtypeStruct((n, d), F32),
        scratch_shapes=[pltpu.VMEM((tm, d), BF16)],
        compiler_params=_params(1),
        name="shortconv_mixer",
    )(x2d, mod_l, g, w_in, w_conv, w_out)


def _ffn_kernel(x_ref, mod_ref, g_ref, wup_ref, wconv_ref, bconv_ref, wdown_ref,
                gfin_ref, o_ref, act_ref, *, shift, seg, final):
    d = x_ref.shape[-1]
    x = x_ref[...].reshape(-1, d)
    hx = _modulate(x, g_ref[...], mod_ref[3:4, :], mod_ref[4:5, :]).astype(BF16)
    for c in range(D_FF // FF_CHUNK):
        lo, hi = c * FF_CHUNK, (c + 1) * FF_CHUNK
        val = _dot(hx, wup_ref[:, D_FF + lo:D_FF + hi])
        gate = _shift_conv(_dot(hx, wup_ref[:, lo:hi]), wconv_ref.at[:, lo:hi],
                           seg, shift)
        gate = gate + bconv_ref[:, lo:hi]
        act_ref[:, lo:hi] = (gate * _sigmoid(gate) * val).astype(BF16)
    tm = x.shape[0]
    pr = tm // DOWN_PIECES
    lead = o_ref.shape[0] // DOWN_PIECES
    for p in range(DOWN_PIECES):
        rows = slice(p * pr, (p + 1) * pr)
        y = _dot(act_ref[rows, :], wdown_ref[...])
        out = x[rows] + mod_ref[5:6, :] * y
        if final:
            out = _rmsnorm(out, gfin_ref[...])
        o_ref[p * lead:(p + 1) * lead] = out.reshape((lead,) + o_ref.shape[1:])


def _ffn(x2d, mod_l, g, w_up, w_conv, b_conv, w_down, g_final, *, batch,
         strip, final):
    n, d = x2d.shape
    wup_spec, w_up = _weight(w_up)
    wdown_spec, w_down = _weight(w_down)
    per = n // batch
    if strip:
        rows = per // GRID_W
        strips = GRID_W // strip
        tm = rows * strip
        xin = x2d.reshape(batch * rows, GRID_W, d)
        x_spec = pl.BlockSpec((rows, strip, d),
                              lambda i: (i // strips, i % strips, 0))
        grid = (batch * strips,)
        row_fn = lambda i: i // strips
        shift = strip
        seg = tm
    else:
        tm = n
        seg = per
        xin = x2d
        x_spec = pl.BlockSpec((tm, d), lambda i: (i, 0))
        grid = (1,)
        row_fn = lambda i: batch
        shift = 1
    out = pl.pallas_call(
        functools.partial(_ffn_kernel, shift=shift, seg=seg, final=final),
        grid=grid,
        in_specs=[
            x_spec,
            pl.BlockSpec((None, 6, d), lambda i: (row_fn(i), 0, 0)),
            _const_spec((1, d)),
            wup_spec,
            _const_spec((3, D_FF)),
            _const_spec((1, D_FF)),
            wdown_spec,
            _const_spec((1, d)),
        ],
        out_specs=x_spec,
        out_shape=jax.ShapeDtypeStruct(xin.shape, F32),
        scratch_shapes=[pltpu.VMEM((tm, D_FF), BF16)],
        compiler_params=_params(1),
        name="convglu_ffn",
    )(xin, mod_l, g, w_up, w_conv, b_conv, w_down, g_final)
    return out.reshape(n, d)


def _log_sigmoid(v):
    return -(jnp.maximum(-v, 0.0) + jnp.log(1.0 + jnp.exp(-jnp.abs(v))))


def _chunk_scans(v, axis, chunk, op, identity):
    n = v.shape[axis]
    shape = (n, 1) if axis == 0 else (1, n)
    pos = lax.broadcasted_iota(jnp.int32, shape, axis) & (chunk - 1)
    pre, suf = v, v
    step = 1
    while step < chunk:
        pre = op(pre, jnp.where(pos >= step,
                                pltpu.roll(pre, step, axis=axis), identity))
        suf = op(suf, jnp.where(pos < chunk - step,
                                pltpu.roll(suf, n - step, axis=axis), identity))
        step *= 2
    return pre, suf


def _mlstm_proj_kernel(x_ref, mod_ref, g_ref, wqv_ref, wkt_ref, wg_ref,
                       wgt_ref, bg_ref, bgt_ref,
                       qv_ref, kt_ref, gc_ref, gr_ref):
    nt = (((1,), (1,)), ((), ()))
    half = N_GATE // 2
    lane = lax.broadcasted_iota(jnp.int32, (1, GATE_PAD), 1)
    fwd = (lane & (half - 1)) < N_HEADS
    sub = lax.broadcasted_iota(jnp.int32, (N_GATE, 1), 0)
    tm = x_ref.shape[0]
    pieces = ROW_PIECES if (tm // ROW_PIECES) % SCAN_CHUNK == 0 else 1
    pr = tm // pieces
    for p in range(pieces):
        rows = slice(p * pr, (p + 1) * pr)
        hx = _modulate(x_ref[rows, :], g_ref[...], mod_ref[0:1, :],
                       mod_ref[1:2, :]).astype(BF16)

        gc = _dot(hx, wg_ref[...]) + bg_ref[...]
        pre, suf = _chunk_scans(_log_sigmoid(gc), 0, SCAN_CHUNK, jnp.add, 0.0)
        b = jnp.where(fwd, pre, suf)
        u = gc - pltpu.roll(b, GATE_PAD - half, axis=1)
        pre, suf = _chunk_scans(u, 0, SCAN_CHUNK, jnp.maximum, -jnp.inf)
        gc_ref[rows, :] = jnp.where(lane < half, jnp.where(fwd, pre, suf), b)

        gr = lax.dot_general(wgt_ref[...], hx, nt, preferred_element_type=F32)
        gr = gr + bgt_ref[...]
        pre, suf = _chunk_scans(_log_sigmoid(gr), 1, SCAN_CHUNK, jnp.add, 0.0)
        b = jnp.where((sub & (half - 1)) < N_HEADS, pre, suf)
        gr_ref[0:half, rows] = gr[0:half] - b[half:]
        gr_ref[half:, rows] = b[half:]

        qv_ref[rows, :] = _dot(hx, wqv_ref[...]).astype(BF16)
        kt = lax.dot_general(wkt_ref[...], hx, nt, preferred_element_type=F32)
        kt_ref[:, rows] = (kt * (DK ** -0.5)).astype(BF16)


def _scan_kernel(*refs, rev, has_add, readout):
    it = iter(refs)
    qv_ref, kt_ref, gc_ref, gr_ref, c0_ref, m0_ref = (next(it) for _ in range(6))
    hadd_ref = next(it) if has_add else None
    if readout:
        x_ref, mod_ref, g_ref, wo_ref, gain_ref, wout_ref = (
            next(it) for _ in range(6))
    out_ref, cf_ref, mf_ref = (next(it) for _ in range(3))
    c_s, m_s, a_s, x_s, s_s, w_s, ktw_s = (next(it) for _ in range(7))
    if readout:
        h_s, z_s = next(it), next(it)
    h_ref = h_s if readout else out_ref
    assert not (rev and readout)
    t = pl.program_id(1)
    n_t = pl.num_programs(1)
    L = SCAN_CHUNK
    nq = N_HEADS * DK
    n_chunks = qv_ref.shape[0] // L
    half = N_GATE // 2

    @pl.when(t == 0)
    def _():
        c_s[...] = c0_ref[...]
        m_s[...] = m0_ref[...]

    ti = lax.broadcasted_iota(jnp.int32, (L, L), 0)
    si = lax.broadcasted_iota(jnp.int32, (L, L), 1)
    mask = (si >= ti) if rev else (si <= ti)
    gcol = N_HEADS if rev else 0
    ones = jnp.ones((L, DK), BF16)
    order = range(n_chunks - 1, -1, -1) if rev else range(n_chunks)

    units = [(c, h) for c in order for h in range(N_HEADS)]

    m_in, mx_of, decay_of = {}, {}, {}
    for h in range(N_HEADS):
        gi = gcol + h
        m_prev = m_s[h]
        for c in order:
            r0, r1 = c * L, (c + 1) * L
            u_row = gr_ref[gi:gi + 1, r0:r1]
            b_row = gr_ref[half + gi:half + gi + 1, r0:r1]
            g = b_row[:, 0:1] if rev else b_row[:, L - 1:L]
            mx = jnp.maximum(m_prev, jnp.max(u_row, axis=1, keepdims=True))
            m_in[c, h], mx_of[c, h] = m_prev, mx
            decay_of[c, h] = jnp.exp(m_prev - mx)
            m_prev = g + mx
        m_s[h] = m_prev

    for c, h in units:
        r0, r1 = c * L, (c + 1) * L
        q = qv_ref[r0:r1, h * DK:(h + 1) * DK]
        kt = kt_ref[h * DK:(h + 1) * DK, r0:r1]
        s_s[c, h] = _dot(q, kt)
        u_row = gr_ref[gcol + h:gcol + h + 1, r0:r1]
        ktw_s[c, h] = (kt.astype(F32) * jnp.exp(u_row - mx_of[c, h])).astype(BF16)
    o_piece = 2 if n_chunks % 2 == 0 else 1
    for i, (c, h) in enumerate(units):
        r0, r1 = c * L, (c + 1) * L
        gi = gcol + h
        u_row = gr_ref[gi:gi + 1, r0:r1]
        cm_bc = jnp.broadcast_to(gc_ref[r0:r1, gi:gi + 1], (L, DK))
        r = jnp.maximum(m_in[c, h], cm_bc)
        w = jnp.exp(jnp.where(mask, u_row - r, -jnp.inf)) * s_s[c, h]
        w_s[c, h] = w.astype(BF16)
        if readout and i % (o_piece * N_HEADS) == 0:
            p0 = (i // (o_piece * N_HEADS)) * o_piece * L
            rows = slice(p0, p0 + o_piece * L)
            hx = _modulate(x_ref[rows, :], g_ref[...], mod_ref[0:1, :],
                           mod_ref[1:2, :]).astype(BF16)
            out_ref[rows, :] = _dot(hx, wo_ref[...])

    for c in order:
        r0, r1 = c * L, (c + 1) * L
        for h in range(N_HEADS):
            v = qv_ref[r0:r1, nq + h * DV:nq + (h + 1) * DV]
            v_aug = jnp.concatenate([v, ones], axis=1)
            a_s[h] = _dot(w_s[c, h], v_aug)
            x_s[h] = _dot(ktw_s[c, h], v_aug)
        for h in range(N_HEADS):
            gi = gcol + h
            q = qv_ref[r0:r1, h * DK:(h + 1) * DK]
            m_prev = m_in[c, h]
            c_prev = c_s[h]
            a_aug = a_s[h]
            b_aug = _dot(q, c_prev.astype(BF16))
            cm_bc = jnp.broadcast_to(gc_ref[r0:r1, gi:gi + 1], (L, DK))
            b_bc = jnp.broadcast_to(gc_ref[r0:r1, half + gi:half + gi + 1],
                                    (L, DK))
            r = jnp.maximum(m_prev, cm_bc)
            wi = jnp.exp(m_prev - r)
            den = a_aug[:, DV:] + wi * b_aug[:, DV:]
            inv = 1.0 / jnp.maximum(jnp.abs(den), jnp.exp(-(b_bc + r)))
            beta = wi * inv
            for j in range(DV // DK):
                lo, hi = j * DK, (j + 1) * DK
                cols = slice(h * DV + lo, h * DV + hi)
                hh = inv * a_aug[:, lo:hi] + beta * b_aug[:, lo:hi]
                if has_add:
                    hh = hh + hadd_ref[r0:r1, cols]
                h_ref[r0:r1, cols] = hh
            for j in range((DV + DK) // DK):
                lo, hi = j * DK, (j + 1) * DK
                c_s[h, :, lo:hi] = (decay_of[c, h] * c_prev[:, lo:hi]
                                    + x_s[h, :, lo:hi])

    @pl.when(t == n_t - 1)
    def _():
        cf_ref[...] = c_s[...]
        mf_ref[...] = m_s[...]

    if readout:
        pr = o_piece * L
        for p in range(n_chunks // o_piece):
            rows = slice(p * pr, (p + 1) * pr)
            for h in range(N_HEADS):
                lo, hi = h * DV, (h + 1) * DV
                hh = h_s[rows, lo:hi]
                ms = jnp.mean(hh * hh, axis=-1, keepdims=True)
                hn = hh * lax.rsqrt(ms + EPS) * gain_ref[:, lo:hi]
                z_s[rows, lo:hi] = (hn * _sigmoid(out_ref[rows, lo:hi])
                                    ).astype(BF16)
            y = _dot(z_s[rows, :], wout_ref[...])
            out_ref[rows, :] = x_ref[rows, :] + mod_ref[2:3, :] * y


def _scan_scratch(n_chunks):
    return [
        pltpu.VMEM((N_HEADS, DK, DV + DK), F32),
        pltpu.VMEM((N_HEADS, 1, DK), F32),
        pltpu.VMEM((N_HEADS, SCAN_CHUNK, DV + DK), F32),
        pltpu.VMEM((N_HEADS, DK, DV + DK), F32),
        pltpu.VMEM((n_chunks, N_HEADS, SCAN_CHUNK, SCAN_CHUNK), F32),
        pltpu.VMEM((n_chunks, N_HEADS, SCAN_CHUNK, SCAN_CHUNK), BF16),
        pltpu.VMEM((n_chunks, N_HEADS, DK, SCAN_CHUNK), BF16),
    ]


def _scan(qv, kt, gc, gr, state, *, batch, ts, rev, add=None, readout=None):
    n = qv.shape[0]
    n_t = n // batch // ts
    n_chunks = ts // SCAN_CHUNK
    nq = N_HEADS * DK
    nv = N_HEADS * DV
    nqv = nq + nv

    def blk(b, t):
        return b * n_t + ((n_t - 1 - t) if rev else t)

    st_shapes = [(N_HEADS, DK, DV + DK), (N_HEADS, 1, DK)]
    st_specs = [pl.BlockSpec((None,) + s, lambda b, t: (b, 0, 0, 0))
                for s in st_shapes]
    h_spec = pl.BlockSpec((ts, nv), lambda b, t: (blk(b, t), 0))
    has_add = add is not None
    in_specs = [
        pl.BlockSpec((ts, nqv), lambda b, t: (blk(b, t), 0)),
        pl.BlockSpec((nq, ts), lambda b, t: (0, blk(b, t))),
        pl.BlockSpec((ts, GATE_PAD), lambda b, t: (blk(b, t), 0)),
        pl.BlockSpec((N_GATE, ts), lambda b, t: (0, blk(b, t))),
    ] + st_specs
    args = [qv, kt, gc, gr, *state]
    scratch = _scan_scratch(n_chunks)
    if has_add:
        in_specs.append(h_spec)
        args.append(add)
    out_cols = nv
    if readout is not None:
        x2d, mod_l, mod_row_fn, g, w_o, gain, w_out = readout
        wout_spec, w_out = _weight(w_out)
        d = x2d.shape[1]
        out_cols = d
        in_specs += [
            pl.BlockSpec((ts, d), lambda b, t: (blk(b, t), 0)),
            pl.BlockSpec((None, 6, d), lambda b, t: (mod_row_fn(b), 0, 0)),
            _const_spec((1, d)),
            _const_spec((d, nv)),
            _const_spec((1, nv)),
            wout_spec,
        ]
        args += [x2d, mod_l, g, w_o, gain, w_out]
        scratch += [pltpu.VMEM((ts, nv), F32), pltpu.VMEM((ts, nv), BF16)]
    out, cf, mf = pl.pallas_call(
        functools.partial(_scan_kernel, rev=rev, has_add=has_add,
                          readout=readout is not None),
        grid=(batch, n_t),
        in_specs=in_specs,
        out_specs=[pl.BlockSpec((ts, out_cols), lambda b, t: (blk(b, t), 0))]
        + st_specs,
        out_shape=[jax.ShapeDtypeStruct((n, out_cols), F32)]
        + [jax.ShapeDtypeStruct((batch,) + s, F32) for s in st_shapes],
        scratch_shapes=scratch,
        compiler_params=_params(2),
        name="mlstm_scan_bwd" if rev else "mlstm_scan_fwd",
    )(*args)
    return out, (cf, mf)


N_PROJ_IN = 9
N_PROJ_OUT = 4


def _proj_scan_kernel(*refs):
    proj_in = refs[:N_PROJ_IN]
    c0_ref, m0_ref = refs[N_PROJ_IN:N_PROJ_IN + 2]
    outs = refs[N_PROJ_IN + 2:]
    proj_out = outs[:N_PROJ_OUT]
    _mlstm_proj_kernel(*proj_in, *proj_out)
    _scan_kernel(*proj_out, c0_ref, m0_ref, *outs[N_PROJ_OUT:], rev=True,
                 has_add=False, readout=False)


def _proj_scan(x2d, mod_l, mod_row_fn, g, wts, state, *, batch, ts):
    n, d = x2d.shape
    n_t = n // batch // ts
    n_chunks = ts // SCAN_CHUNK
    nq = N_HEADS * DK
    nv = N_HEADS * DV
    nqv = nq + nv

    def blk(b, t):
        return b * n_t + (n_t - 1 - t)

    rows = lambda w: pl.BlockSpec((ts, w), lambda b, t: (blk(b, t), 0))
    cols = lambda h: pl.BlockSpec((h, ts), lambda b, t: (0, blk(b, t)))
    st_shapes = [(N_HEADS, DK, DV + DK), (N_HEADS, 1, DK)]
    st_specs = [pl.BlockSpec((None,) + s, lambda b, t: (b, 0, 0, 0))
                for s in st_shapes]
    qv, kt, gc, gr, h, cf, mf = pl.pallas_call(
        _proj_scan_kernel,
        grid=(batch, n_t),
        in_specs=[
            rows(d),
            pl.BlockSpec((None, 6, d), lambda b, t: (mod_row_fn(b), 0, 0)),
            _const_spec((1, d)),
            _const_spec((d, nqv)),
            _const_spec((nq, d)),
            _const_spec((d, GATE_PAD)),
            _const_spec((N_GATE, d)),
            _const_spec((1, GATE_PAD)),
            _const_spec((N_GATE, 1)),
        ] + st_specs,
        out_specs=[rows(nqv), cols(nq), rows(GATE_PAD), cols(N_GATE), rows(nv)]
        + st_specs,
        out_shape=[
            jax.ShapeDtypeStruct((n, nqv), BF16),
            jax.ShapeDtypeStruct((nq, n), BF16),
            jax.ShapeDtypeStruct((n, GATE_PAD), F32),
            jax.ShapeDtypeStruct((N_GATE, n), F32),
            jax.ShapeDtypeStruct((n, nv), F32),
        ] + [jax.ShapeDtypeStruct((batch,) + s, F32) for s in st_shapes],
        scratch_shapes=_scan_scratch(n_chunks),
        compiler_params=_params(2),
        name="mlstm_proj_scan_bwd",
    )(x2d, mod_l, g, wts["w_qv"], wts["w_kt"], wts["w_g"], wts["w_gt"],
      wts["b_g"], wts["b_gt"], *state)
    return (qv, kt, gc, gr), h, (cf, mf)


def _mlstm_weights(w_in, b_gate):
    nq = N_HEADS * DK
    o2 = 2 * nq
    o3 = o2 + N_HEADS * DV
    o4 = o3 + D_MODEL
    w_g = w_in[:, o4:]
    return {
        "w_qv": jnp.concatenate([w_in[:, :nq], w_in[:, o2:o3]], axis=1).astype(BF16),
        "w_kt": w_in[:, nq:o2].T.astype(BF16),
        "w_o": w_in[:, o3:o4].astype(BF16),
        "w_g": jnp.pad(w_g, ((0, 0), (0, GATE_PAD - N_GATE))).astype(BF16),
        "w_gt": w_g.T.astype(BF16),
        "b_g": jnp.pad(b_gate, (0, GATE_PAD - N_GATE)).reshape(1, GATE_PAD),
        "b_gt": b_gate.reshape(N_GATE, 1),
    }


def kernel(x, c, ctx, c_ctx, w_mod, b_mod, g_mix, g_ffn, a_w_in, a_b_gate,
           a_head_gain, a_w_out, b_w_in, b_w_conv, b_w_out, f_w_up, f_w_conv,
           f_b_conv, f_w_down, g_final):
    bn, t, d = x.shape
    n_ctx = ctx.shape[1]
    depth = w_mod.shape[0]
    n_mixers = 2
    assert d == D_MODEL and bn <= 7 and t % TOKEN_TILE == 0
    assert n_ctx % SCAN_CHUNK == 0 and TOKEN_TILE % SCAN_CHUNK == 0

    rec_layers = [i for i in range(depth) if i % n_mixers == 0]
    last_rec = max(rec_layers) if rec_layers else -1

    cc = jnp.zeros((8, d), F32).at[:bn].set(c).at[bn].set(c_ctx)
    mod = _mod_all(cc, w_mod, b_mod).reshape(depth, 8, 6, d)

    tm = TOKEN_TILE
    tiles_per_batch = t // tm
    lat_row = lambda i: i // tiles_per_batch
    ctx_row = lambda i: bn

    xs = x.reshape(bn * t, d)
    cs = ctx.reshape(bn * n_ctx, d)
    zero_state = (jnp.zeros((bn, N_HEADS, DK, DV + DK), F32),
                  jnp.zeros((bn, N_HEADS, 1, DK), F32))

    a_w_out_b = a_w_out.astype(BF16)
    b_w_in_b = b_w_in.astype(BF16)
    b_w_out_b = b_w_out.astype(BF16)
    f_w_up_b = f_w_up.astype(BF16)
    f_w_down_b = f_w_down.astype(BF16)

    for i in range(depth):
        j = i // n_mixers
        ctx_read = i <= last_rec
        ctx_live = i < last_rec
        gm = g_mix[i].reshape(1, d)
        gf = g_ffn[i].reshape(1, d)
        if i % n_mixers == 0:
            wts = _mlstm_weights(a_w_in[j], a_b_gate[j])
            w_out = (a_w_out_b, j)
            gain = a_head_gain[j].reshape(1, N_HEADS * DV)
            s_f = s_b = zero_state
            if ctx_read:
                pc, hc, s_b = _proj_scan(cs, mod[i], lambda b: bn, gm, wts,
                                         zero_state, batch=bn, ts=n_ctx)
                ro = ((cs, mod[i], lambda b: bn, gm, wts["w_o"], gain, w_out)
                      if ctx_live else None)
                cs_new, s_f = _scan(*pc, zero_state, batch=bn, ts=n_ctx,
                                    rev=False, add=hc, readout=ro)
            px, hl, _ = _proj_scan(xs, mod[i], lambda b: b, gm, wts, s_b,
                                   batch=bn, ts=tm)
            ro = (xs, mod[i], lambda b: b, gm, wts["w_o"], gain, w_out)
            xs, _ = _scan(*px, s_f, batch=bn, ts=tm, rev=False, add=hl,
                          readout=ro)
            if ctx_live:
                cs = cs_new
        else:
            w_in = (b_w_in_b, j)
            w_out = (b_w_out_b, j)
            xs = _shortconv(xs, mod[i], gm, w_in, b_w_conv[j], w_out,
                            tm=tm, seg=GRID_W, row_fn=lat_row)
            if ctx_live:
                cs = _shortconv(cs, mod[i], gm, w_in, b_w_conv[j], w_out,
                                tm=bn * n_ctx, seg=n_ctx, row_fn=ctx_row)
        w_up = (f_w_up_b, i)
        w_down = (f_w_down_b, i)
        b_conv = f_b_conv[i].reshape(1, D_FF)
        gfin = g_final.reshape(1, d)
        xs = _ffn(xs, mod[i], gf, w_up, f_w_conv[i], b_conv, w_down, gfin,
                  batch=bn, strip=FFN_STRIP, final=(i == depth - 1))
        if ctx_live:
            cs = _ffn(cs, mod[i], gf, w_up, f_w_conv[i], b_conv, w_down, gfin,
                      batch=bn, strip=0, final=False)

    return xs.reshape(bn, t, d)
```

```python
import functools

import jax
import jax.numpy as jnp
from jax import lax
from jax.experimental import pallas as pl
from jax.experimental.pallas import tpu as pltpu

F32 = jnp.float32
BF16 = jnp.bfloat16

D_MODEL = 1024
GRID_W = 64
N_HEADS = 4
DK = 128
DV = 256
D_FF = 2816
EPS = 1e-6
N_GATE = 16
GATE_PAD = 128

SCAN_CHUNK = 128
FF_CHUNK = 256
DOWN_PIECES = 2
ROW_PIECES = 2
TOKEN_TILE = 1024
FFN_STRIP = 8
VMEM_LIMIT = 56 * 1024 * 1024


def _params(n_axes):
    return pltpu.CompilerParams(
        dimension_semantics=("arbitrary",) * n_axes,
        vmem_limit_bytes=VMEM_LIMIT)


def _const_spec(shape):
    zeros = (0,) * len(shape)
    return pl.BlockSpec(shape, lambda *_: zeros, pipeline_mode=pl.Buffered(1))


def _weight(w):
    if isinstance(w, tuple):
        arr, layer = w
        idx = (layer,) + (0,) * (arr.ndim - 1)
        spec = pl.BlockSpec((None,) + arr.shape[1:], lambda *_: idx,
                            pipeline_mode=pl.Buffered(1))
        return spec, arr
    return _const_spec(w.shape), w


def _dot(a, b):
    return jnp.dot(a, b, preferred_element_type=F32)


def _sigmoid(v):
    return 1.0 / (1.0 + jnp.exp(-v))


def _rmsnorm(x, g):
    ms = jnp.mean(x * x, axis=-1, keepdims=True)
    return (x * lax.rsqrt(ms + EPS)) * g


def _modulate(x, g, shift, scale):
    return _rmsnorm(x, g) * (1.0 + scale) + shift


def _shift_conv(u, w_ref, seg, shift=1):
    n = u.shape[0]
    if seg == n and shift % 8 == 0:
        pad = jnp.zeros((shift, u.shape[1]), u.dtype)
        prev = jnp.concatenate([pad, u[:n - shift]], axis=0)
        nxt = jnp.concatenate([u[shift:], pad], axis=0)
    else:
        pos = lax.broadcasted_iota(jnp.int32, (n, 1), 0) & (seg - 1)
        prev = jnp.where(pos >= shift, pltpu.roll(u, shift, axis=0), 0.0)
        nxt = jnp.where(pos < seg - shift, pltpu.roll(u, n - shift, axis=0), 0.0)
    return prev * w_ref[0:1, :] + u * w_ref[1:2, :] + nxt * w_ref[2:3, :]


def _mod_kernel(cc_ref, w_ref, b_ref, o_ref):
    cc = cc_ref[...]
    s = cc * _sigmoid(cc)
    o_ref[...] = _dot(s, w_ref[...]) + b_ref[...]


def _mod_all(cc, w_mod, b_mod):
    depth, d, n = w_mod.shape
    tn = 1536
    return pl.pallas_call(
        _mod_kernel,
        grid=(depth, n // tn),
        in_specs=[
            pl.BlockSpec((8, d), lambda l, j: (0, 0)),
            pl.BlockSpec((None, d, tn), lambda l, j: (l, 0, j)),
            pl.BlockSpec((None, 1, tn), lambda l, j: (l, 0, j)),
        ],
        out_specs=pl.BlockSpec((None, 8, tn), lambda l, j: (l, 0, j)),
        out_shape=jax.ShapeDtypeStruct((depth, 8, n), F32),
        compiler_params=_params(2),
        name="adaln_mod",
    )(cc, w_mod, b_mod.reshape(depth, 1, n))


def _shortconv_kernel(x_ref, mod_ref, g_ref, win_ref, wconv_ref, wout_ref,
                      o_ref, z_ref, *, seg):
    d = D_MODEL
    cw = 256
    tm = x_ref.shape[0]
    pieces = ROW_PIECES if (tm // ROW_PIECES) % seg == 0 else 1
    pr = tm // pieces
    for p in range(pieces):
        rows = slice(p * pr, (p + 1) * pr)
        x = x_ref[rows, :]
        hx = _modulate(x, g_ref[...], mod_ref[0:1, :], mod_ref[1:2, :]).astype(BF16)
        for j in range(d // cw):
            lo, hi = j * cw, (j + 1) * cw
            bg = _dot(hx, win_ref[:, lo:hi])
            cg = _dot(hx, win_ref[:, d + lo:d + hi])
            xv = _dot(hx, win_ref[:, 2 * d + lo:2 * d + hi])
            cv = _shift_conv(cg * xv, wconv_ref.at[:, lo:hi], seg)
            z_ref[rows, lo:hi] = (bg * cv).astype(BF16)
        y = _dot(z_ref[rows, :], wout_ref[...])
        o_ref[rows, :] = x + mod_ref[2:3, :] * y


def _shortconv(x2d, mod_l, g, w_in, w_conv, w_out, *, tm, seg, row_fn):
    n, d = x2d.shape
    win_spec, w_in = _weight(w_in)
    wout_spec, w_out = _weight(w_out)
    return pl.pallas_call(
        functools.partial(_shortconv_kernel, seg=seg),
        grid=(n // tm,),
        in_specs=[
            pl.BlockSpec((tm, d), lambda i: (i, 0)),
            pl.BlockSpec((None, 6, d), lambda i: (row_fn(i), 0, 0)),
            _const_spec((1, d)),
            win_spec,
            _const_spec((3, d)),
            wout_spec,
        ],
        out_specs=pl.BlockSpec((tm, d), lambda i: (i, 0)),
        out_shape=jax.ShapeDtypeStruct((n, d), F32),
        scratch_shapes=[pltpu.VMEM((tm, d), BF16)],
        compiler_params=_params(1),
        name="shortconv_mixer",
    )(x2d, mod_l, g, w_in, w_conv, w_out)


def _ffn_kernel(*refs, shift, seg, final, cast_next):
    (x_ref, mod_ref, g_ref, wup_ref, wconv_ref, bconv_ref, wdown_ref,
     gfin_ref) = refs[:8]
    if cast_next:
        wun_i, wdn_i, o_ref, wun_o, wdn_o, act_ref = refs[8:]
        wun_o[...] = wun_i[...].astype(BF16)
        wdn_o[...] = wdn_i[...].astype(BF16)
    else:
        o_ref, act_ref = refs[8:]
    d = x_ref.shape[-1]
    x = x_ref[...].reshape(-1, d)
    hx = _modulate(x, g_ref[...], mod_ref[3:4, :], mod_ref[4:5, :]).astype(BF16)
    for c in range(D_FF // FF_CHUNK):
        lo, hi = c * FF_CHUNK, (c + 1) * FF_CHUNK
        val = _dot(hx, wup_ref[:, D_FF + lo:D_FF + hi])
        gate = _shift_conv(_dot(hx, wup_ref[:, lo:hi]), wconv_ref.at[:, lo:hi],
                           seg, shift)
        gate = gate + bconv_ref[:, lo:hi]
        act_ref[:, lo:hi] = (gate * _sigmoid(gate) * val).astype(BF16)
    tm = x.shape[0]
    pr = tm // DOWN_PIECES
    lead = o_ref.shape[0] // DOWN_PIECES
    for p in range(DOWN_PIECES):
        rows = slice(p * pr, (p + 1) * pr)
        y = _dot(act_ref[rows, :], wdown_ref[...])
        out = x[rows] + mod_ref[5:6, :] * y
        if final:
            out = _rmsnorm(out, gfin_ref[...])
        o_ref[p * lead:(p + 1) * lead] = out.reshape((lead,) + o_ref.shape[1:])


def _ffn(x2d, mod_l, g, w_up, w_conv, b_conv, w_down, g_final, *, batch,
         strip, final, cast_next=None):
    n, d = x2d.shape
    wup_spec, w_up = _weight(w_up)
    wdown_spec, w_down = _weight(w_down)
    per = n // batch
    if strip:
        rows = per // GRID_W
        strips = GRID_W // strip
        tm = rows * strip
        xin = x2d.reshape(batch * rows, GRID_W, d)
        x_spec = pl.BlockSpec((rows, strip, d),
                              lambda i: (i // strips, i % strips, 0))
        grid = (batch * strips,)
        row_fn = lambda i: i // strips
        shift = strip
    else:
        tm = per
        xin = x2d
        x_spec = pl.BlockSpec((tm, d), lambda i: (i, 0))
        grid = (batch,)
        row_fn = lambda i: batch
        shift = 1
    in_specs = [
        x_spec,
        pl.BlockSpec((None, 6, d), lambda i: (row_fn(i), 0, 0)),
        _const_spec((1, d)),
        wup_spec,
        _const_spec((3, D_FF)),
        _const_spec((1, D_FF)),
        wdown_spec,
        _const_spec((1, d)),
    ]
    args = [xin, mod_l, g, w_up, w_conv, b_conv, w_down, g_final]
    out_specs = [x_spec]
    out_shape = [jax.ShapeDtypeStruct(xin.shape, F32)]
    if cast_next is not None:
        steps = grid[0]
        for w in cast_next:
            slabs = w.reshape(steps, w.shape[0] // steps, w.shape[1])
            spec = pl.BlockSpec((None,) + slabs.shape[1:], lambda i: (i, 0, 0))
            in_specs.append(spec)
            args.append(slabs)
            out_specs.append(spec)
            out_shape.append(jax.ShapeDtypeStruct(slabs.shape, BF16))
    outs = pl.pallas_call(
        functools.partial(_ffn_kernel, shift=shift, seg=tm, final=final,
                          cast_next=cast_next is not None),
        grid=grid,
        in_specs=in_specs,
        out_specs=out_specs,
        out_shape=out_shape,
        scratch_shapes=[pltpu.VMEM((tm, D_FF), BF16)],
        compiler_params=_params(1),
        name="convglu_ffn",
    )(*args)
    out = outs[0].reshape(n, d)
    if cast_next is None:
        return out
    return out, tuple(o.reshape(w.shape) for o, w in zip(outs[1:], cast_next))


def _log_sigmoid(v):
    return -(jnp.maximum(-v, 0.0) + jnp.log(1.0 + jnp.exp(-jnp.abs(v))))


def _chunk_scans(v, axis, chunk, op, identity):
    n = v.shape[axis]
    shape = (n, 1) if axis == 0 else (1, n)
    pos = lax.broadcasted_iota(jnp.int32, shape, axis) & (chunk - 1)
    pre, suf = v, v
    step = 1
    while step < chunk:
        pre = op(pre, jnp.where(pos >= step,
                                pltpu.roll(pre, step, axis=axis), identity))
        suf = op(suf, jnp.where(pos < chunk - step,
                                pltpu.roll(suf, n - step, axis=axis), identity))
        step *= 2
    return pre, suf


def _mlstm_proj_kernel(x_ref, mod_ref, g_ref, wqv_ref, wkt_ref, wg_ref,
                       wgt_ref, bg_ref, bgt_ref,
                       qv_ref, kt_ref, gc_ref, gr_ref):
    nt = (((1,), (1,)), ((), ()))
    half = N_GATE // 2
    lane = lax.broadcasted_iota(jnp.int32, (1, GATE_PAD), 1)
    fwd = (lane & (half - 1)) < N_HEADS
    sub = lax.broadcasted_iota(jnp.int32, (N_GATE, 1), 0)
    tm = x_ref.shape[0]
    pieces = ROW_PIECES if (tm // ROW_PIECES) % SCAN_CHUNK == 0 else 1
    pr = tm // pieces
    for p in range(pieces):
        rows = slice(p * pr, (p + 1) * pr)
        hx = _modulate(x_ref[rows, :], g_ref[...], mod_ref[0:1, :],
                       mod_ref[1:2, :]).astype(BF16)

        gc = _dot(hx, wg_ref[...]) + bg_ref[...]
        pre, suf = _chunk_scans(_log_sigmoid(gc), 0, SCAN_CHUNK, jnp.add, 0.0)
        b = jnp.where(fwd, pre, suf)
        u = gc - pltpu.roll(b, GATE_PAD - half, axis=1)
        pre, suf = _chunk_scans(u, 0, SCAN_CHUNK, jnp.maximum, -jnp.inf)
        gc_ref[rows, :] = jnp.where(lane < half, jnp.where(fwd, pre, suf), b)

        gr = lax.dot_general(wgt_ref[...], hx, nt, preferred_element_type=F32)
        gr = gr + bgt_ref[...]
        pre, suf = _chunk_scans(_log_sigmoid(gr), 1, SCAN_CHUNK, jnp.add, 0.0)
        b = jnp.where((sub & (half - 1)) < N_HEADS, pre, suf)
        gr_ref[0:half, rows] = gr[0:half] - b[half:]
        gr_ref[half:, rows] = b[half:]

        qv_ref[rows, :] = _dot(hx, wqv_ref[...]).astype(BF16)
        kt = lax.dot_general(wkt_ref[...], hx, nt, preferred_element_type=F32)
        kt_ref[:, rows] = (kt * (DK ** -0.5)).astype(BF16)


def _scan_kernel(*refs, rev, has_add, readout):
    it = iter(refs)
    qv_ref, kt_ref, gc_ref, gr_ref, c0_ref, m0_ref = (next(it) for _ in range(6))
    hadd_ref = next(it) if has_add else None
    if readout:
        x_ref, mod_ref, g_ref, wo_ref, gain_ref, wout_ref = (
            next(it) for _ in range(6))
    out_ref, cf_ref, mf_ref = (next(it) for _ in range(3))
    c_s, m_s, a_s, x_s, s_s, w_s, ktw_s = (next(it) for _ in range(7))
    if readout:
        h_s, z_s = next(it), next(it)
    h_ref = h_s if readout else out_ref
    assert not (rev and readout)
    t = pl.program_id(1)
    n_t = pl.num_programs(1)
    L = SCAN_CHUNK
    nq = N_HEADS * DK
    n_chunks = qv_ref.shape[0] // L
    half = N_GATE // 2

    @pl.when(t == 0)
    def _():
        c_s[...] = c0_ref[...]
        m_s[...] = m0_ref[...]

    ti = lax.broadcasted_iota(jnp.int32, (L, L), 0)
    si = lax.broadcasted_iota(jnp.int32, (L, L), 1)
    mask = (si >= ti) if rev else (si <= ti)
    gcol = N_HEADS if rev else 0
    ones = jnp.ones((L, DK), BF16)
    order = range(n_chunks - 1, -1, -1) if rev else range(n_chunks)

    units = [(c, h) for c in order for h in range(N_HEADS)]

    m_in, mx_of, decay_of = {}, {}, {}
    for h in range(N_HEADS):
        gi = gcol + h
        m_prev = m_s[h]
        for c in order:
            r0, r1 = c * L, (c + 1) * L
            u_row = gr_ref[gi:gi + 1, r0:r1]
            b_row = gr_ref[half + gi:half + gi + 1, r0:r1]
            g = b_row[:, 0:1] if rev else b_row[:, L - 1:L]
            mx = jnp.maximum(m_prev, jnp.max(u_row, axis=1, keepdims=True))
            m_in[c, h], mx_of[c, h] = m_prev, mx
            decay_of[c, h] = jnp.exp(m_prev - mx)
            m_prev = g + mx
        m_s[h] = m_prev

    for c, h in units:
        r0, r1 = c * L, (c + 1) * L
        q = qv_ref[r0:r1, h * DK:(h + 1) * DK]
        kt = kt_ref[h * DK:(h + 1) * DK, r0:r1]
        s_s[c, h] = _dot(q, kt)
        u_row = gr_ref[gcol + h:gcol + h + 1, r0:r1]
        ktw_s[c, h] = (kt.astype(F32) * jnp.exp(u_row - mx_of[c, h])).astype(BF16)
    o_piece = 2 if n_chunks % 2 == 0 else 1
    for i, (c, h) in enumerate(units):
        r0, r1 = c * L, (c + 1) * L
        gi = gcol + h
        u_row = gr_ref[gi:gi + 1, r0:r1]
        cm_bc = jnp.broadcast_to(gc_ref[r0:r1, gi:gi + 1], (L, DK))
        r = jnp.maximum(m_in[c, h], cm_bc)
        w = jnp.exp(jnp.where(mask, u_row - r, -jnp.inf)) * s_s[c, h]
        w_s[c, h] = w.astype(BF16)
        if readout and i % (o_piece * N_HEADS) == 0:
            p0 = (i // (o_piece * N_HEADS)) * o_piece * L
            rows = slice(p0, p0 + o_piece * L)
            hx = _modulate(x_ref[rows, :], g_ref[...], mod_ref[0:1, :],
                           mod_ref[1:2, :]).astype(BF16)
            out_ref[rows, :] = _dot(hx, wo_ref[...])

    for c in order:
        r0, r1 = c * L, (c + 1) * L
        for h in range(N_HEADS):
            v = qv_ref[r0:r1, nq + h * DV:nq + (h + 1) * DV]
            v_aug = jnp.concatenate([v, ones], axis=1)
            a_s[h] = _dot(w_s[c, h], v_aug)
            x_s[h] = _dot(ktw_s[c, h], v_aug)
        for h in range(N_HEADS):
            gi = gcol + h
            q = qv_ref[r0:r1, h * DK:(h + 1) * DK]
            m_prev = m_in[c, h]
            c_prev = c_s[h]
            a_aug = a_s[h]
            b_aug = _dot(q, c_prev.astype(BF16))
            cm_bc = jnp.broadcast_to(gc_ref[r0:r1, gi:gi + 1], (L, DK))
            b_bc = jnp.broadcast_to(gc_ref[r0:r1, half + gi:half + gi + 1],
                                    (L, DK))
            r = jnp.maximum(m_prev, cm_bc)
            wi = jnp.exp(m_prev - r)
            den = a_aug[:, DV:] + wi * b_aug[:, DV:]
            inv = 1.0 / jnp.maximum(jnp.abs(den), jnp.exp(-(b_bc + r)))
            beta = wi * inv
            for j in range(DV // DK):
                lo, hi = j * DK, (j + 1) * DK
                cols = slice(h * DV + lo, h * DV + hi)
                hh = inv * a_aug[:, lo:hi] + beta * b_aug[:, lo:hi]
                if has_add:
                    hh = hh + hadd_ref[r0:r1, cols]
                h_ref[r0:r1, cols] = hh
            for j in range((DV + DK) // DK):
                lo, hi = j * DK, (j + 1) * DK
                c_s[h, :, lo:hi] = (decay_of[c, h] * c_prev[:, lo:hi]
                                    + x_s[h, :, lo:hi])

    @pl.when(t == n_t - 1)
    def _():
        cf_ref[...] = c_s[...]
        mf_ref[...] = m_s[...]

    if readout:
        pr = o_piece * L
        for p in range(n_chunks // o_piece):
            rows = slice(p * pr, (p + 1) * pr)
            for h in range(N_HEADS):
                lo, hi = h * DV, (h + 1) * DV
                hh = h_s[rows, lo:hi]
                ms = jnp.mean(hh * hh, axis=-1, keepdims=True)
                hn = hh * lax.rsqrt(ms + EPS) * gain_ref[:, lo:hi]
                z_s[rows, lo:hi] = (hn * _sigmoid(out_ref[rows, lo:hi])
                                    ).astype(BF16)
            y = _dot(z_s[rows, :], wout_ref[...])
            out_ref[rows, :] = x_ref[rows, :] + mod_ref[2:3, :] * y


def _scan_scratch(n_chunks):
    return [
        pltpu.VMEM((N_HEADS, DK, DV + DK), F32),
        pltpu.VMEM((N_HEADS, 1, DK), F32),
        pltpu.VMEM((N_HEADS, SCAN_CHUNK, DV + DK), F32),
        pltpu.VMEM((N_HEADS, DK, DV + DK), F32),
        pltpu.VMEM((n_chunks, N_HEADS, SCAN_CHUNK, SCAN_CHUNK), F32),
        pltpu.VMEM((n_chunks, N_HEADS, SCAN_CHUNK, SCAN_CHUNK), BF16),
        pltpu.VMEM((n_chunks, N_HEADS, DK, SCAN_CHUNK), BF16),
    ]


def _scan(qv, kt, gc, gr, state, *, batch, ts, rev, add=None, readout=None):
    n = qv.shape[0]
    n_t = n // batch // ts
    n_chunks = ts // SCAN_CHUNK
    nq = N_HEADS * DK
    nv = N_HEADS * DV
    nqv = nq + nv

    def blk(b, t):
        return b * n_t + ((n_t - 1 - t) if rev else t)

    st_shapes = [(N_HEADS, DK, DV + DK), (N_HEADS, 1, DK)]
    st_specs = [pl.BlockSpec((None,) + s, lambda b, t: (b, 0, 0, 0))
                for s in st_shapes]
    h_spec = pl.BlockSpec((ts, nv), lambda b, t: (blk(b, t), 0))
    has_add = add is not None
    in_specs = [
        pl.BlockSpec((ts, nqv), lambda b, t: (blk(b, t), 0)),
        pl.BlockSpec((nq, ts), lambda b, t: (0, blk(b, t))),
        pl.BlockSpec((ts, GATE_PAD), lambda b, t: (blk(b, t), 0)),
        pl.BlockSpec((N_GATE, ts), lambda b, t: (0, blk(b, t))),
    ] + st_specs
    args = [qv, kt, gc, gr, *state]
    scratch = _scan_scratch(n_chunks)
    if has_add:
        in_specs.append(h_spec)
        args.append(add)
    out_cols = nv
    if readout is not None:
        x2d, mod_l, mod_row_fn, g, w_o, gain, w_out = readout
        wout_spec, w_out = _weight(w_out)
        d = x2d.shape[1]
        out_cols = d
        in_specs += [
            pl.BlockSpec((ts, d), lambda b, t: (blk(b, t), 0)),
            pl.BlockSpec((None, 6, d), lambda b, t: (mod_row_fn(b), 0, 0)),
            _const_spec((1, d)),
            _const_spec((d, nv)),
            _const_spec((1, nv)),
            wout_spec,
        ]
        args += [x2d, mod_l, g, w_o, gain, w_out]
        scratch += [pltpu.VMEM((ts, nv), F32), pltpu.VMEM((ts, nv), BF16)]
    out, cf, mf = pl.pallas_call(
        functools.partial(_scan_kernel, rev=rev, has_add=has_add,
                          readout=readout is not None),
        grid=(batch, n_t),
        in_specs=in_specs,
        out_specs=[pl.BlockSpec((ts, out_cols), lambda b, t: (blk(b, t), 0))]
        + st_specs,
        out_shape=[jax.ShapeDtypeStruct((n, out_cols), F32)]
        + [jax.ShapeDtypeStruct((batch,) + s, F32) for s in st_shapes],
        scratch_shapes=scratch,
        compiler_params=_params(2),
        name="mlstm_scan_bwd" if rev else "mlstm_scan_fwd",
    )(*args)
    return out, (cf, mf)


N_PROJ_IN = 9
N_PROJ_OUT = 4


def _proj_scan_kernel(*refs):
    proj_in = refs[:N_PROJ_IN]
    c0_ref, m0_ref = refs[N_PROJ_IN:N_PROJ_IN + 2]
    outs = refs[N_PROJ_IN + 2:]
    proj_out = outs[:N_PROJ_OUT]
    _mlstm_proj_kernel(*proj_in, *proj_out)
    _scan_kernel(*proj_out, c0_ref, m0_ref, *outs[N_PROJ_OUT:], rev=True,
                 has_add=False, readout=False)


def _proj_scan(x2d, mod_l, mod_row_fn, g, wts, state, *, batch, ts):
    n, d = x2d.shape
    n_t = n // batch // ts
    n_chunks = ts // SCAN_CHUNK
    nq = N_HEADS * DK
    nv = N_HEADS * DV
    nqv = nq + nv

    def blk(b, t):
        return b * n_t + (n_t - 1 - t)

    rows = lambda w: pl.BlockSpec((ts, w), lambda b, t: (blk(b, t), 0))
    cols = lambda h: pl.BlockSpec((h, ts), lambda b, t: (0, blk(b, t)))
    st_shapes = [(N_HEADS, DK, DV + DK), (N_HEADS, 1, DK)]
    st_specs = [pl.BlockSpec((None,) + s, lambda b, t: (b, 0, 0, 0))
                for s in st_shapes]
    qv, kt, gc, gr, h, cf, mf = pl.pallas_call(
        _proj_scan_kernel,
        grid=(batch, n_t),
        in_specs=[
            rows(d),
            pl.BlockSpec((None, 6, d), lambda b, t: (mod_row_fn(b), 0, 0)),
            _const_spec((1, d)),
            _const_spec((d, nqv)),
            _const_spec((nq, d)),
            _const_spec((d, GATE_PAD)),
            _const_spec((N_GATE, d)),
            _const_spec((1, GATE_PAD)),
            _const_spec((N_GATE, 1)),
        ] + st_specs,
        out_specs=[rows(nqv), cols(nq), rows(GATE_PAD), cols(N_GATE), rows(nv)]
        + st_specs,
        out_shape=[
            jax.ShapeDtypeStruct((n, nqv), BF16),
            jax.ShapeDtypeStruct((nq, n), BF16),
            jax.ShapeDtypeStruct((n, GATE_PAD), F32),
            jax.ShapeDtypeStruct((N_GATE, n), F32),
            jax.ShapeDtypeStruct((n, nv), F32),
        ] + [jax.ShapeDtypeStruct((batch,) + s, F32) for s in st_shapes],
        scratch_shapes=_scan_scratch(n_chunks),
        compiler_params=_params(2),
        name="mlstm_proj_scan_bwd",
    )(x2d, mod_l, g, wts["w_qv"], wts["w_kt"], wts["w_g"], wts["w_gt"],
      wts["b_g"], wts["b_gt"], *state)
    return (qv, kt, gc, gr), h, (cf, mf)


def _mlstm_weights(w_in, b_gate):
    nq = N_HEADS * DK
    o2 = 2 * nq
    o3 = o2 + N_HEADS * DV
    o4 = o3 + D_MODEL
    w_g = w_in[:, o4:]
    return {
        "w_qv": jnp.concatenate([w_in[:, :nq], w_in[:, o2:o3]], axis=1).astype(BF16),
        "w_kt": w_in[:, nq:o2].T.astype(BF16),
        "w_o": w_in[:, o3:o4].astype(BF16),
        "w_g": jnp.pad(w_g, ((0, 0), (0, GATE_PAD - N_GATE))).astype(BF16),
        "w_gt": w_g.T.astype(BF16),
        "b_g": jnp.pad(b_gate, (0, GATE_PAD - N_GATE)).reshape(1, GATE_PAD),
        "b_gt": b_gate.reshape(N_GATE, 1),
    }


def kernel(x, c, ctx, c_ctx, w_mod, b_mod, g_mix, g_ffn, a_w_in, a_b_gate,
           a_head_gain, a_w_out, b_w_in, b_w_conv, b_w_out, f_w_up, f_w_conv,
           f_b_conv, f_w_down, g_final):
    bn, t, d = x.shape
    n_ctx = ctx.shape[1]
    depth = w_mod.shape[0]
    n_mixers = 2
    assert d == D_MODEL and bn <= 7 and t % TOKEN_TILE == 0
    assert n_ctx % SCAN_CHUNK == 0 and TOKEN_TILE % SCAN_CHUNK == 0

    rec_layers = [i for i in range(depth) if i % n_mixers == 0]
    last_rec = max(rec_layers) if rec_layers else -1

    cc = jnp.zeros((8, d), F32).at[:bn].set(c).at[bn].set(c_ctx)
    mod = _mod_all(cc, w_mod, b_mod).reshape(depth, 8, 6, d)

    tm = TOKEN_TILE
    tiles_per_batch = t // tm
    lat_row = lambda i: i // tiles_per_batch
    ctx_row = lambda i: bn

    xs = x.reshape(bn * t, d)
    cs = ctx.reshape(bn * n_ctx, d)
    zero_state = (jnp.zeros((bn, N_HEADS, DK, DV + DK), F32),
                  jnp.zeros((bn, N_HEADS, 1, DK), F32))

    a_w_out_b = a_w_out.astype(BF16)
    b_w_in_b = b_w_in.astype(BF16)
    b_w_out_b = b_w_out.astype(BF16)
    ffn_w = (f_w_up[0].astype(BF16), f_w_down[0].astype(BF16))

    for i in range(depth):
        j = i // n_mixers
        ctx_read = i <= last_rec
        ctx_live = i < last_rec
        gm = g_mix[i].reshape(1, d)
        gf = g_ffn[i].reshape(1, d)
        if i % n_mixers == 0:
            wts = _mlstm_weights(a_w_in[j], a_b_gate[j])
            w_out = (a_w_out_b, j)
            gain = a_head_gain[j].reshape(1, N_HEADS * DV)
            s_f = s_b = zero_state
            if ctx_read:
                pc, hc, s_b = _proj_scan(cs, mod[i], lambda b: bn, gm, wts,
                                         zero_state, batch=bn, ts=n_ctx)
                ro = ((cs, mod[i], lambda b: bn, gm, wts["w_o"], gain, w_out)
                      if ctx_live else None)
                cs_new, s_f = _scan(*pc, zero_state, batch=bn, ts=n_ctx,
                                    rev=False, add=hc, readout=ro)
            px, hl, _ = _proj_scan(xs, mod[i], lambda b: b, gm, wts, s_b,
                                   batch=bn, ts=tm)
            ro = (xs, mod[i], lambda b: b, gm, wts["w_o"], gain, w_out)
            xs, _ = _scan(*px, s_f, batch=bn, ts=tm, rev=False, add=hl,
                          readout=ro)
            if ctx_live:
                cs = cs_new
        else:
            w_in = (b_w_in_b, j)
            w_out = (b_w_out_b, j)
            xs = _shortconv(xs, mod[i], gm, w_in, b_w_conv[j], w_out,
                            tm=tm, seg=GRID_W, row_fn=lat_row)
            if ctx_live:
                cs = _shortconv(cs, mod[i], gm, w_in, b_w_conv[j], w_out,
                                tm=n_ctx, seg=n_ctx, row_fn=ctx_row)
        w_up, w_down = ffn_w
        b_conv = f_b_conv[i].reshape(1, D_FF)
        gfin = g_final.reshape(1, d)
        if i + 1 < depth:
            xs, ffn_w = _ffn(xs, mod[i], gf, w_up, f_w_conv[i], b_conv, w_down,
                             gfin, batch=bn, strip=FFN_STRIP, final=False,
                             cast_next=(f_w_up[i + 1], f_w_down[i + 1]))
        else:
            xs = _ffn(xs, mod[i], gf, w_up, f_w_conv[i], b_conv, w_down, gfin,
                      batch=bn, strip=FFN_STRIP, final=True)
        if ctx_live:
            cs = _ffn(cs, mod[i], gf, w_up, f_w_conv[i], b_conv, w_down, gfin,
                      batch=bn, strip=0, final=False)

    return xs.reshape(bn, t, d)
```

```python
import functools

import jax
import jax.numpy as jnp
from jax import lax
from jax.experimental import pallas as pl
from jax.experimental.pallas import tpu as pltpu

F32 = jnp.float32
BF16 = jnp.bfloat16

D_MODEL = 1024
GRID_W = 64
N_HEADS = 4
DK = 128
DV = 256
D_FF = 2816
EPS = 1e-6
N_GATE = 16
GATE_PAD = 128

SCAN_CHUNK = 128
FF_CHUNK = 256
DOWN_PIECES = 2
ROW_PIECES = 2
TOKEN_TILE = 1024
FFN_STRIP = 8
VMEM_LIMIT = 56 * 1024 * 1024


def _params(n_axes):
    return pltpu.CompilerParams(
        dimension_semantics=("arbitrary",) * n_axes,
        vmem_limit_bytes=VMEM_LIMIT)


def _const_spec(shape):
    zeros = (0,) * len(shape)
    return pl.BlockSpec(shape, lambda *_: zeros, pipeline_mode=pl.Buffered(1))


def _weight(w):
    if isinstance(w, tuple):
        arr, layer = w
        idx = (layer,) + (0,) * (arr.ndim - 1)
        spec = pl.BlockSpec((None,) + arr.shape[1:], lambda *_: idx,
                            pipeline_mode=pl.Buffered(1))
        return spec, arr
    return _const_spec(w.shape), w


def _dot(a, b):
    return jnp.dot(a, b, preferred_element_type=F32)


def _sigmoid(v):
    return 1.0 / (1.0 + jnp.exp(-v))


def _rmsnorm(x, g):
    ms = jnp.mean(x * x, axis=-1, keepdims=True)
    return (x * lax.rsqrt(ms + EPS)) * g


def _modulate(x, g, shift, scale):
    return _rmsnorm(x, g) * (1.0 + scale) + shift


def _shift_conv(u, w_ref, seg, shift=1):
    n = u.shape[0]
    if seg == n and shift % 8 == 0:
        pad = jnp.zeros((shift, u.shape[1]), u.dtype)
        prev = jnp.concatenate([pad, u[:n - shift]], axis=0)
        nxt = jnp.concatenate([u[shift:], pad], axis=0)
    else:
        pos = lax.broadcasted_iota(jnp.int32, (n, 1), 0) & (seg - 1)
        prev = jnp.where(pos >= shift, pltpu.roll(u, shift, axis=0), 0.0)
        nxt = jnp.where(pos < seg - shift, pltpu.roll(u, n - shift, axis=0), 0.0)
    return prev * w_ref[0:1, :] + u * w_ref[1:2, :] + nxt * w_ref[2:3, :]


def _mod_kernel(cc_ref, w_ref, b_ref, o_ref):
    cc = cc_ref[...]
    s = cc * _sigmoid(cc)
    o_ref[...] = _dot(s, w_ref[...]) + b_ref[...]


def _mod_all(cc, w_mod, b_mod):
    depth, d, n = w_mod.shape
    tn = 1536
    return pl.pallas_call(
        _mod_kernel,
        grid=(depth, n // tn),
        in_specs=[
            pl.BlockSpec((8, d), lambda l, j: (0, 0)),
            pl.BlockSpec((None, d, tn), lambda l, j: (l, 0, j)),
            pl.BlockSpec((None, 1, tn), lambda l, j: (l, 0, j)),
        ],
        out_specs=pl.BlockSpec((None, 8, tn), lambda l, j: (l, 0, j)),
        out_shape=jax.ShapeDtypeStruct((depth, 8, n), F32),
        compiler_params=_params(2),
        name="adaln_mod",
    )(cc, w_mod, b_mod.reshape(depth, 1, n))


def _shortconv_kernel(x_ref, mod_ref, g_ref, win_ref, wconv_ref, wout_ref,
                      o_ref, z_ref, *, seg):
    d = D_MODEL
    cw = 256
    tm = x_ref.shape[0]
    pieces = ROW_PIECES if (tm // ROW_PIECES) % seg == 0 else 1
    pr = tm // pieces
    for p in range(pieces):
        rows = slice(p * pr, (p + 1) * pr)
        x = x_ref[rows, :]
        hx = _modulate(x, g_ref[...], mod_ref[0:1, :], mod_ref[1:2, :]).astype(BF16)
        for j in range(d // cw):
            lo, hi = j * cw, (j + 1) * cw
            bg = _dot(hx, win_ref[:, lo:hi])
            cg = _dot(hx, win_ref[:, d + lo:d + hi])
            xv = _dot(hx, win_ref[:, 2 * d + lo:2 * d + hi])
            cv = _shift_conv(cg * xv, wconv_ref.at[:, lo:hi], seg)
            z_ref[rows, lo:hi] = (bg * cv).astype(BF16)
        y = _dot(z_ref[rows, :], wout_ref[...])
        o_ref[rows, :] = x + mod_ref[2:3, :] * y


def _shortconv(x2d, mod_l, g, w_in, w_conv, w_out, *, tm, seg, row_fn):
    n, d = x2d.shape
    win_spec, w_in = _weight(w_in)
    wout_spec, w_out = _weight(w_out)
    return pl.pallas_call(
        functools.partial(_shortconv_kernel, seg=seg),
        grid=(n // tm,),
        in_specs=[
            pl.BlockSpec((tm, d), lambda i: (i, 0)),
            pl.BlockSpec((None, 6, d), lambda i: (row_fn(i), 0, 0)),
            _const_spec((1, d)),
            win_spec,
            _const_spec((3, d)),
            wout_spec,
        ],
        out_specs=pl.BlockSpec((tm, d), lambda i: (i, 0)),
        out_shape=jax.ShapeDtypeStruct((n, d), F32),
        scratch_shapes=[pltpu.VMEM((tm, d), BF16)],
        compiler_params=_params(1),
        name="shortconv_mixer",
    )(x2d, mod_l, g, w_in, w_conv, w_out)


def _ffn_kernel(*refs, shift, seg, final, cast_next):
    (x_ref, mod_ref, g_ref, wup_ref, wconv_ref, bconv_ref, wdown_ref,
     gfin_ref) = refs[:8]
    if cast_next:
        wun_i, wdn_i, o_ref, wun_o, wdn_o, act_ref = refs[8:]
        wun_o[...] = wun_i[...].astype(BF16)
        wdn_o[...] = wdn_i[...].astype(BF16)
    else:
        o_ref, act_ref = refs[8:]
    d = x_ref.shape[-1]
    x = x_ref[...].reshape(-1, d)
    hx = _modulate(x, g_ref[...], mod_ref[3:4, :], mod_ref[4:5, :]).astype(BF16)
    for c in range(D_FF // FF_CHUNK):
        lo, hi = c * FF_CHUNK, (c + 1) * FF_CHUNK
        val = _dot(hx, wup_ref[:, D_FF + lo:D_FF + hi])
        gate = _shift_conv(_dot(hx, wup_ref[:, lo:hi]), wconv_ref.at[:, lo:hi],
                           seg, shift)
        gate = gate + bconv_ref[:, lo:hi]
        act_ref[:, lo:hi] = (gate * _sigmoid(gate) * val).astype(BF16)
    tm = x.shape[0]
    pr = tm // DOWN_PIECES
    lead = o_ref.shape[0] // DOWN_PIECES
    for p in range(DOWN_PIECES):
        rows = slice(p * pr, (p + 1) * pr)
        y = _dot(act_ref[rows, :], wdown_ref[...])
        out = x[rows] + mod_ref[5:6, :] * y
        if final:
            out = _rmsnorm(out, gfin_ref[...])
        o_ref[p * lead:(p + 1) * lead] = out.reshape((lead,) + o_ref.shape[1:])


def _ffn(x2d, mod_l, g, w_up, w_conv, b_conv, w_down, g_final, *, batch,
         strip, final, cast_next=None):
    n, d = x2d.shape
    wup_spec, w_up = _weight(w_up)
    wdown_spec, w_down = _weight(w_down)
    per = n // batch
    if strip:
        rows = per // GRID_W
        strips = GRID_W // strip
        tm = rows * strip
        xin = x2d.reshape(batch * rows, GRID_W, d)
        x_spec = pl.BlockSpec((rows, strip, d),
                              lambda i: (i // strips, i % strips, 0))
        grid = (batch * strips,)
        row_fn = lambda i: i // strips
        shift = strip
    else:
        tm = per
        xin = x2d
        x_spec = pl.BlockSpec((tm, d), lambda i: (i, 0))
        grid = (batch,)
        row_fn = lambda i: batch
        shift = 1
    in_specs = [
        x_spec,
        pl.BlockSpec((None, 6, d), lambda i: (row_fn(i), 0, 0)),
        _const_spec((1, d)),
        wup_spec,
        _const_spec((3, D_FF)),
        _const_spec((1, D_FF)),
        wdown_spec,
        _const_spec((1, d)),
    ]
    args = [xin, mod_l, g, w_up, w_conv, b_conv, w_down, g_final]
    out_specs = [x_spec]
    out_shape = [jax.ShapeDtypeStruct(xin.shape, F32)]
    if cast_next is not None:
        steps = grid[0]
        layer, stacks = cast_next
        for w in stacks:
            slab = (w.shape[1] // steps, w.shape[2])
            first = layer * steps
            in_specs.append(pl.BlockSpec((None,) + slab,
                                         lambda i: (first + i, 0, 0)))
            args.append(w.reshape((w.shape[0] * steps,) + slab))
            out_specs.append(pl.BlockSpec((None,) + slab, lambda i: (i, 0, 0)))
            out_shape.append(jax.ShapeDtypeStruct((steps,) + slab, BF16))
    outs = pl.pallas_call(
        functools.partial(_ffn_kernel, shift=shift, seg=tm, final=final,
                          cast_next=cast_next is not None),
        grid=grid,
        in_specs=in_specs,
        out_specs=out_specs,
        out_shape=out_shape,
        scratch_shapes=[pltpu.VMEM((tm, D_FF), BF16)],
        compiler_params=_params(1),
        name="convglu_ffn",
    )(*args)
    out = outs[0].reshape(n, d)
    if cast_next is None:
        return out
    return out, tuple(o.reshape(w.shape[1:]) for o, w in zip(outs[1:], stacks))


def _log_sigmoid(v):
    return -(jnp.maximum(-v, 0.0) + jnp.log(1.0 + jnp.exp(-jnp.abs(v))))


def _chunk_scans(v, axis, chunk, op, identity):
    n = v.shape[axis]
    shape = (n, 1) if axis == 0 else (1, n)
    pos = lax.broadcasted_iota(jnp.int32, shape, axis) & (chunk - 1)
    pre, suf = v, v
    step = 1
    while step < chunk:
        pre = op(pre, jnp.where(pos >= step,
                                pltpu.roll(pre, step, axis=axis), identity))
        suf = op(suf, jnp.where(pos < chunk - step,
                                pltpu.roll(suf, n - step, axis=axis), identity))
        step *= 2
    return pre, suf


def _mlstm_proj_kernel(x_ref, mod_ref, g_ref, wqv_ref, wkt_ref, wg_ref,
                       wgt_ref, bg_ref, bgt_ref,
                       qv_ref, kt_ref, gc_ref, gr_ref):
    nt = (((1,), (1,)), ((), ()))
    half = N_GATE // 2
    lane = lax.broadcasted_iota(jnp.int32, (1, GATE_PAD), 1)
    fwd = (lane & (half - 1)) < N_HEADS
    sub = lax.broadcasted_iota(jnp.int32, (N_GATE, 1), 0)
    tm = x_ref.shape[0]
    pieces = ROW_PIECES if (tm // ROW_PIECES) % SCAN_CHUNK == 0 else 1
    pr = tm // pieces
    for p in range(pieces):
        rows = slice(p * pr, (p + 1) * pr)
        hx = _modulate(x_ref[rows, :], g_ref[...], mod_ref[0:1, :],
                       mod_ref[1:2, :]).astype(BF16)

        gc = _dot(hx, wg_ref[...]) + bg_ref[...]
        pre, suf = _chunk_scans(_log_sigmoid(gc), 0, SCAN_CHUNK, jnp.add, 0.0)
        b = jnp.where(fwd, pre, suf)
        u = gc - pltpu.roll(b, GATE_PAD - half, axis=1)
        pre, suf = _chunk_scans(u, 0, SCAN_CHUNK, jnp.maximum, -jnp.inf)
        gc_ref[rows, :] = jnp.where(lane < half, jnp.where(fwd, pre, suf), b)

        gr = lax.dot_general(wgt_ref[...], hx, nt, preferred_element_type=F32)
        gr = gr + bgt_ref[...]
        pre, suf = _chunk_scans(_log_sigmoid(gr), 1, SCAN_CHUNK, jnp.add, 0.0)
        b = jnp.where((sub & (half - 1)) < N_HEADS, pre, suf)
        gr_ref[0:half, rows] = gr[0:half] - b[half:]
        gr_ref[half:, rows] = b[half:]

        qv_ref[rows, :] = _dot(hx, wqv_ref[...]).astype(BF16)
        kt = lax.dot_general(wkt_ref[...], hx, nt, preferred_element_type=F32)
        kt_ref[:, rows] = (kt * (DK ** -0.5)).astype(BF16)


def _scan_kernel(*refs, rev, has_add, readout):
    it = iter(refs)
    qv_ref, kt_ref, gc_ref, gr_ref, c0_ref, m0_ref = (next(it) for _ in range(6))
    hadd_ref = next(it) if has_add else None
    if readout:
        x_ref, mod_ref, g_ref, wo_ref, gain_ref, wout_ref = (
            next(it) for _ in range(6))
    out_ref, cf_ref, mf_ref = (next(it) for _ in range(3))
    c_s, m_s, a_s, x_s, s_s, w_s, ktw_s = (next(it) for _ in range(7))
    if readout:
        h_s, z_s = next(it), next(it)
    h_ref = h_s if readout else out_ref
    assert not (rev and readout)
    t = pl.program_id(1)
    n_t = pl.num_programs(1)
    L = SCAN_CHUNK
    nq = N_HEADS * DK
    n_chunks = qv_ref.shape[0] // L
    half = N_GATE // 2

    @pl.when(t == 0)
    def _():
        c_s[...] = c0_ref[...]
        m_s[...] = m0_ref[...]

    ti = lax.broadcasted_iota(jnp.int32, (L, L), 0)
    si = lax.broadcasted_iota(jnp.int32, (L, L), 1)
    mask = (si >= ti) if rev else (si <= ti)
    gcol = N_HEADS if rev else 0
    ones = jnp.ones((L, DK), BF16)
    order = range(n_chunks - 1, -1, -1) if rev else range(n_chunks)

    units = [(c, h) for c in order for h in range(N_HEADS)]

    m_in, mx_of, decay_of = {}, {}, {}
    for h in range(N_HEADS):
        gi = gcol + h
        m_prev = m_s[h]
        for c in order:
            r0, r1 = c * L, (c + 1) * L
            u_row = gr_ref[gi:gi + 1, r0:r1]
            b_row = gr_ref[half + gi:half + gi + 1, r0:r1]
            g = b_row[:, 0:1] if rev else b_row[:, L - 1:L]
            mx = jnp.maximum(m_prev, jnp.max(u_row, axis=1, keepdims=True))
            m_in[c, h], mx_of[c, h] = m_prev, mx
            decay_of[c, h] = jnp.exp(m_prev - mx)
            m_prev = g + mx
        m_s[h] = m_prev

    for c, h in units:
        r0, r1 = c * L, (c + 1) * L
        q = qv_ref[r0:r1, h * DK:(h + 1) * DK]
        kt = kt_ref[h * DK:(h + 1) * DK, r0:r1]
        s_s[c, h] = _dot(q, kt)
        u_row = gr_ref[gcol + h:gcol + h + 1, r0:r1]
        ktw_s[c, h] = (kt.astype(F32) * jnp.exp(u_row - mx_of[c, h])).astype(BF16)
    o_piece = 2 if n_chunks % 2 == 0 else 1
    for i, (c, h) in enumerate(units):
        r0, r1 = c * L, (c + 1) * L
        gi = gcol + h
        u_row = gr_ref[gi:gi + 1, r0:r1]
        cm_bc = jnp.broadcast_to(gc_ref[r0:r1, gi:gi + 1], (L, DK))
        r = jnp.maximum(m_in[c, h], cm_bc)
        w = jnp.exp(jnp.where(mask, u_row - r, -jnp.inf)) * s_s[c, h]
        w_s[c, h] = w.astype(BF16)
        if readout and i % (o_piece * N_HEADS) == 0:
            p0 = (i // (o_piece * N_HEADS)) * o_piece * L
            rows = slice(p0, p0 + o_piece * L)
            hx = _modulate(x_ref[rows, :], g_ref[...], mod_ref[0:1, :],
                           mod_ref[1:2, :]).astype(BF16)
            out_ref[rows, :] = _dot(hx, wo_ref[...])

    for c in order:
        r0, r1 = c * L, (c + 1) * L
        for h in range(N_HEADS):
            v = qv_ref[r0:r1, nq + h * DV:nq + (h + 1) * DV]
            v_aug = jnp.concatenate([v, ones], axis=1)
            a_s[h] = _dot(w_s[c, h], v_aug)
            x_s[h] = _dot(ktw_s[c, h], v_aug)
        for h in range(N_HEADS):
            gi = gcol + h
            q = qv_ref[r0:r1, h * DK:(h + 1) * DK]
            m_prev = m_in[c, h]
            c_prev = c_s[h]
            a_aug = a_s[h]
            b_aug = _dot(q, c_prev.astype(BF16))
            cm_bc = jnp.broadcast_to(gc_ref[r0:r1, gi:gi + 1], (L, DK))
            b_bc = jnp.broadcast_to(gc_ref[r0:r1, half + gi:half + gi + 1],
                                    (L, DK))
            r = jnp.maximum(m_prev, cm_bc)
            wi = jnp.exp(m_prev - r)
            den = a_aug[:, DV:] + wi * b_aug[:, DV:]
            inv = 1.0 / jnp.maximum(jnp.abs(den), jnp.exp(-(b_bc + r)))
            beta = wi * inv
            for j in range(DV // DK):
                lo, hi = j * DK, (j + 1) * DK
                cols = slice(h * DV + lo, h * DV + hi)
                hh = inv * a_aug[:, lo:hi] + beta * b_aug[:, lo:hi]
                if has_add:
                    hh = hh + hadd_ref[r0:r1, cols]
                h_ref[r0:r1, cols] = hh
            for j in range((DV + DK) // DK):
                lo, hi = j * DK, (j + 1) * DK
                c_s[h, :, lo:hi] = (decay_of[c, h] * c_prev[:, lo:hi]
                                    + x_s[h, :, lo:hi])

    @pl.when(t == n_t - 1)
    def _():
        cf_ref[...] = c_s[...]
        mf_ref[...] = m_s[...]

    if readout:
        pr = o_piece * L
        for p in range(n_chunks // o_piece):
            rows = slice(p * pr, (p + 1) * pr)
            for h in range(N_HEADS):
                lo, hi = h * DV, (h + 1) * DV
                hh = h_s[rows, lo:hi]
                ms = jnp.mean(hh * hh, axis=-1, keepdims=True)
                hn = hh * lax.rsqrt(ms + EPS) * gain_ref[:, lo:hi]
                z_s[rows, lo:hi] = (hn * _sigmoid(out_ref[rows, lo:hi])
                                    ).astype(BF16)
            y = _dot(z_s[rows, :], wout_ref[...])
            out_ref[rows, :] = x_ref[rows, :] + mod_ref[2:3, :] * y


def _scan_scratch(n_chunks):
    return [
        pltpu.VMEM((N_HEADS, DK, DV + DK), F32),
        pltpu.VMEM((N_HEADS, 1, DK), F32),
        pltpu.VMEM((N_HEADS, SCAN_CHUNK, DV + DK), F32),
        pltpu.VMEM((N_HEADS, DK, DV + DK), F32),
        pltpu.VMEM((n_chunks, N_HEADS, SCAN_CHUNK, SCAN_CHUNK), F32),
        pltpu.VMEM((n_chunks, N_HEADS, SCAN_CHUNK, SCAN_CHUNK), BF16),
        pltpu.VMEM((n_chunks, N_HEADS, DK, SCAN_CHUNK), BF16),
    ]


def _scan(qv, kt, gc, gr, state, *, batch, ts, rev, add=None, readout=None):
    n = qv.shape[0]
    n_t = n // batch // ts
    n_chunks = ts // SCAN_CHUNK
    nq = N_HEADS * DK
    nv = N_HEADS * DV
    nqv = nq + nv

    def blk(b, t):
        return b * n_t + ((n_t - 1 - t) if rev else t)

    st_shapes = [(N_HEADS, DK, DV + DK), (N_HEADS, 1, DK)]
    st_specs = [pl.BlockSpec((None,) + s, lambda b, t: (b, 0, 0, 0))
                for s in st_shapes]
    h_spec = pl.BlockSpec((ts, nv), lambda b, t: (blk(b, t), 0))
    has_add = add is not None
    in_specs = [
        pl.BlockSpec((ts, nqv), lambda b, t: (blk(b, t), 0)),
        pl.BlockSpec((nq, ts), lambda b, t: (0, blk(b, t))),
        pl.BlockSpec((ts, GATE_PAD), lambda b, t: (blk(b, t), 0)),
        pl.BlockSpec((N_GATE, ts), lambda b, t: (0, blk(b, t))),
    ] + st_specs
    args = [qv, kt, gc, gr, *state]
    scratch = _scan_scratch(n_chunks)
    if has_add:
        in_specs.append(h_spec)
        args.append(add)
    out_cols = nv
    if readout is not None:
        x2d, mod_l, mod_row_fn, g, w_o, gain, w_out = readout
        wout_spec, w_out = _weight(w_out)
        d = x2d.shape[1]
        out_cols = d
        in_specs += [
            pl.BlockSpec((ts, d), lambda b, t: (blk(b, t), 0)),
            pl.BlockSpec((None, 6, d), lambda b, t: (mod_row_fn(b), 0, 0)),
            _const_spec((1, d)),
            _const_spec((d, nv)),
            _const_spec((1, nv)),
            wout_spec,
        ]
        args += [x2d, mod_l, g, w_o, gain, w_out]
        scratch += [pltpu.VMEM((ts, nv), F32), pltpu.VMEM((ts, nv), BF16)]
    out, cf, mf = pl.pallas_call(
        functools.partial(_scan_kernel, rev=rev, has_add=has_add,
                          readout=readout is not None),
        grid=(batch, n_t),
        in_specs=in_specs,
        out_specs=[pl.BlockSpec((ts, out_cols), lambda b, t: (blk(b, t), 0))]
        + st_specs,
        out_shape=[jax.ShapeDtypeStruct((n, out_cols), F32)]
        + [jax.ShapeDtypeStruct((batch,) + s, F32) for s in st_shapes],
        scratch_shapes=scratch,
        compiler_params=_params(2),
        name="mlstm_scan_bwd" if rev else "mlstm_scan_fwd",
    )(*args)
    return out, (cf, mf)


N_PROJ_IN = 9
N_PROJ_OUT = 4


def _proj_scan_kernel(*refs):
    proj_in = refs[:N_PROJ_IN]
    c0_ref, m0_ref = refs[N_PROJ_IN:N_PROJ_IN + 2]
    outs = refs[N_PROJ_IN + 2:]
    proj_out = outs[:N_PROJ_OUT]
    _mlstm_proj_kernel(*proj_in, *proj_out)
    _scan_kernel(*proj_out, c0_ref, m0_ref, *outs[N_PROJ_OUT:], rev=True,
                 has_add=False, readout=False)


def _proj_scan(x2d, mod_l, mod_row_fn, g, wts, state, *, batch, ts):
    n, d = x2d.shape
    n_t = n // batch // ts
    n_chunks = ts // SCAN_CHUNK
    nq = N_HEADS * DK
    nv = N_HEADS * DV
    nqv = nq + nv

    def blk(b, t):
        return b * n_t + (n_t - 1 - t)

    rows = lambda w: pl.BlockSpec((ts, w), lambda b, t: (blk(b, t), 0))
    cols = lambda h: pl.BlockSpec((h, ts), lambda b, t: (0, blk(b, t)))
    st_shapes = [(N_HEADS, DK, DV + DK), (N_HEADS, 1, DK)]
    st_specs = [pl.BlockSpec((None,) + s, lambda b, t: (b, 0, 0, 0))
                for s in st_shapes]
    qv, kt, gc, gr, h, cf, mf = pl.pallas_call(
        _proj_scan_kernel,
        grid=(batch, n_t),
        in_specs=[
            rows(d),
            pl.BlockSpec((None, 6, d), lambda b, t: (mod_row_fn(b), 0, 0)),
            _const_spec((1, d)),
            _const_spec((d, nqv)),
            _const_spec((nq, d)),
            _const_spec((d, GATE_PAD)),
            _const_spec((N_GATE, d)),
            _const_spec((1, GATE_PAD)),
            _const_spec((N_GATE, 1)),
        ] + st_specs,
        out_specs=[rows(nqv), cols(nq), rows(GATE_PAD), cols(N_GATE), rows(nv)]
        + st_specs,
        out_shape=[
            jax.ShapeDtypeStruct((n, nqv), BF16),
            jax.ShapeDtypeStruct((nq, n), BF16),
            jax.ShapeDtypeStruct((n, GATE_PAD), F32),
            jax.ShapeDtypeStruct((N_GATE, n), F32),
            jax.ShapeDtypeStruct((n, nv), F32),
        ] + [jax.ShapeDtypeStruct((batch,) + s, F32) for s in st_shapes],
        scratch_shapes=_scan_scratch(n_chunks),
        compiler_params=_params(2),
        name="mlstm_proj_scan_bwd",
    )(x2d, mod_l, g, wts["w_qv"], wts["w_kt"], wts["w_g"], wts["w_gt"],
      wts["b_g"], wts["b_gt"], *state)
    return (qv, kt, gc, gr), h, (cf, mf)


def _mlstm_weights(w_in, b_gate):
    nq = N_HEADS * DK
    o2 = 2 * nq
    o3 = o2 + N_HEADS * DV
    o4 = o3 + D_MODEL
    w_g = w_in[:, o4:]
    return {
        "w_qv": jnp.concatenate([w_in[:, :nq], w_in[:, o2:o3]], axis=1).astype(BF16),
        "w_kt": w_in[:, nq:o2].T.astype(BF16),
        "w_o": w_in[:, o3:o4].astype(BF16),
        "w_g": jnp.pad(w_g, ((0, 0), (0, GATE_PAD - N_GATE))).astype(BF16),
        "w_gt": w_g.T.astype(BF16),
        "b_g": jnp.pad(b_gate, (0, GATE_PAD - N_GATE)).reshape(1, GATE_PAD),
        "b_gt": b_gate.reshape(N_GATE, 1),
    }


def kernel(x, c, ctx, c_ctx, w_mod, b_mod, g_mix, g_ffn, a_w_in, a_b_gate,
           a_head_gain, a_w_out, b_w_in, b_w_conv, b_w_out, f_w_up, f_w_conv,
           f_b_conv, f_w_down, g_final):
    bn, t, d = x.shape
    n_ctx = ctx.shape[1]
    depth = w_mod.shape[0]
    n_mixers = 2
    assert d == D_MODEL and bn <= 7 and t % TOKEN_TILE == 0
    assert n_ctx % SCAN_CHUNK == 0 and TOKEN_TILE % SCAN_CHUNK == 0

    rec_layers = [i for i in range(depth) if i % n_mixers == 0]
    last_rec = max(rec_layers) if rec_layers else -1

    cc = jnp.zeros((8, d), F32).at[:bn].set(c).at[bn].set(c_ctx)
    mod = _mod_all(cc, w_mod, b_mod).reshape(depth, 8, 6, d)

    tm = TOKEN_TILE
    tiles_per_batch = t // tm
    lat_row = lambda i: i // tiles_per_batch
    ctx_row = lambda i: bn

    xs = x.reshape(bn * t, d)
    cs = ctx.reshape(bn * n_ctx, d)
    zero_state = (jnp.zeros((bn, N_HEADS, DK, DV + DK), F32),
                  jnp.zeros((bn, N_HEADS, 1, DK), F32))

    a_w_out_b = a_w_out.astype(BF16)
    b_w_in_b = b_w_in.astype(BF16)
    b_w_out_b = b_w_out.astype(BF16)
    ffn_w = (f_w_up[0].astype(BF16), f_w_down[0].astype(BF16))

    for i in range(depth):
        j = i // n_mixers
        ctx_read = i <= last_rec
        ctx_live = i < last_rec
        gm = g_mix[i].reshape(1, d)
        gf = g_ffn[i].reshape(1, d)
        if i % n_mixers == 0:
            wts = _mlstm_weights(a_w_in[j], a_b_gate[j])
            w_out = (a_w_out_b, j)
            gain = a_head_gain[j].reshape(1, N_HEADS * DV)
            s_f = s_b = zero_state
            if ctx_read:
                pc, hc, s_b = _proj_scan(cs, mod[i], lambda b: bn, gm, wts,
                                         zero_state, batch=bn, ts=n_ctx)
                ro = ((cs, mod[i], lambda b: bn, gm, wts["w_o"], gain, w_out)
                      if ctx_live else None)
                cs_new, s_f = _scan(*pc, zero_state, batch=bn, ts=n_ctx,
                                    rev=False, add=hc, readout=ro)
            px, hl, _ = _proj_scan(xs, mod[i], lambda b: b, gm, wts, s_b,
                                   batch=bn, ts=tm)
            ro = (xs, mod[i], lambda b: b, gm, wts["w_o"], gain, w_out)
            xs, _ = _scan(*px, s_f, batch=bn, ts=tm, rev=False, add=hl,
                          readout=ro)
            if ctx_live:
                cs = cs_new
        else:
            w_in = (b_w_in_b, j)
            w_out = (b_w_out_b, j)
            xs = _shortconv(xs, mod[i], gm, w_in, b_w_conv[j], w_out,
                            tm=tm, seg=GRID_W, row_fn=lat_row)
            if ctx_live:
                cs = _shortconv(cs, mod[i], gm, w_in, b_w_conv[j], w_out,
                                tm=n_ctx, seg=n_ctx, row_fn=ctx_row)
        w_up, w_down = ffn_w
        b_conv = f_b_conv[i].reshape(1, D_FF)
        gfin = g_final.reshape(1, d)
        if i + 1 < depth:
            xs, ffn_w = _ffn(xs, mod[i], gf, w_up, f_w_conv[i], b_conv, w_down,
                             gfin, batch=bn, strip=FFN_STRIP, final=False,
                             cast_next=(i + 1, (f_w_up, f_w_down)))
        else:
            xs = _ffn(xs, mod[i], gf, w_up, f_w_conv[i], b_conv, w_down, gfin,
                      batch=bn, strip=FFN_STRIP, final=True)
        if ctx_live:
            cs = _ffn(cs, mod[i], gf, w_up, f_w_conv[i], b_conv, w_down, gfin,
                      batch=bn, strip=0, final=False)

    return xs.reshape(bn, t, d)
```

```python
import functools

import jax
import jax.numpy as jnp
from jax import lax
from jax.experimental import pallas as pl
from jax.experimental.pallas import tpu as pltpu

F32 = jnp.float32
BF16 = jnp.bfloat16

D_MODEL = 1024
GRID_W = 64
N_HEADS = 4
DK = 128
DV = 256
D_FF = 2816
EPS = 1e-6
N_GATE = 16
GATE_PAD = 128

SCAN_CHUNK = 128
FF_CHUNK = 256
DOWN_PIECES = 2
ROW_PIECES = 2
TOKEN_TILE = 1024
FFN_STRIP = 8
VMEM_LIMIT = 56 * 1024 * 1024


def _params(n_axes):
    return pltpu.CompilerParams(
        dimension_semantics=("arbitrary",) * n_axes,
        vmem_limit_bytes=VMEM_LIMIT)


def _const_spec(shape):
    zeros = (0,) * len(shape)
    return pl.BlockSpec(shape, lambda *_: zeros, pipeline_mode=pl.Buffered(1))


def _weight(w):
    if isinstance(w, tuple):
        arr, layer = w
        idx = (layer,) + (0,) * (arr.ndim - 1)
        spec = pl.BlockSpec((None,) + arr.shape[1:], lambda *_: idx,
                            pipeline_mode=pl.Buffered(1))
        return spec, arr
    return _const_spec(w.shape), w


def _dot(a, b):
    return jnp.dot(a, b, preferred_element_type=F32)


def _sigmoid(v):
    return 1.0 / (1.0 + jnp.exp(-v))


def _rmsnorm(x, g):
    ms = jnp.mean(x * x, axis=-1, keepdims=True)
    return (x * lax.rsqrt(ms + EPS)) * g


def _modulate(x, g, shift, scale):
    return _rmsnorm(x, g) * (1.0 + scale) + shift


def _shift_conv(u, w_ref, seg, shift=1):
    n = u.shape[0]
    if seg == n and shift % 8 == 0:
        pad = jnp.zeros((shift, u.shape[1]), u.dtype)
        prev = jnp.concatenate([pad, u[:n - shift]], axis=0)
        nxt = jnp.concatenate([u[shift:], pad], axis=0)
    else:
        pos = lax.broadcasted_iota(jnp.int32, (n, 1), 0) & (seg - 1)
        prev = jnp.where(pos >= shift, pltpu.roll(u, shift, axis=0), 0.0)
        nxt = jnp.where(pos < seg - shift, pltpu.roll(u, n - shift, axis=0), 0.0)
    return prev * w_ref[0:1, :] + u * w_ref[1:2, :] + nxt * w_ref[2:3, :]


def _mod_kernel(cc_ref, w_ref, b_ref, o_ref):
    cc = cc_ref[...]
    s = cc * _sigmoid(cc)
    o_ref[...] = _dot(s, w_ref[...]) + b_ref[...]


def _mod_all(cc, w_mod, b_mod):
    depth, d, n = w_mod.shape
    tn = 1536
    return pl.pallas_call(
        _mod_kernel,
        grid=(depth, n // tn),
        in_specs=[
            pl.BlockSpec((8, d), lambda l, j: (0, 0)),
            pl.BlockSpec((None, d, tn), lambda l, j: (l, 0, j)),
            pl.BlockSpec((None, 1, tn), lambda l, j: (l, 0, j)),
        ],
        out_specs=pl.BlockSpec((None, 8, tn), lambda l, j: (l, 0, j)),
        out_shape=jax.ShapeDtypeStruct((depth, 8, n), F32),
        compiler_params=_params(2),
        name="adaln_mod",
    )(cc, w_mod, b_mod.reshape(depth, 1, n))


def _shortconv_kernel(x_ref, mod_ref, g_ref, win_ref, wconv_ref, wout_ref,
                      o_ref, z_ref, *, seg):
    d = D_MODEL
    cw = 256
    tm = x_ref.shape[0]
    pieces = ROW_PIECES if (tm // ROW_PIECES) % seg == 0 else 1
    pr = tm // pieces
    for p in range(pieces):
        rows = slice(p * pr, (p + 1) * pr)
        x = x_ref[rows, :]
        hx = _modulate(x, g_ref[...], mod_ref[0:1, :], mod_ref[1:2, :]).astype(BF16)
        for j in range(d // cw):
            lo, hi = j * cw, (j + 1) * cw
            bg = _dot(hx, win_ref[:, lo:hi])
            cg = _dot(hx, win_ref[:, d + lo:d + hi])
            xv = _dot(hx, win_ref[:, 2 * d + lo:2 * d + hi])
            cv = _shift_conv(cg * xv, wconv_ref.at[:, lo:hi], seg)
            z_ref[rows, lo:hi] = (bg * cv).astype(BF16)
        y = _dot(z_ref[rows, :], wout_ref[...])
        o_ref[rows, :] = x + mod_ref[2:3, :] * y


def _shortconv(x2d, mod_l, g, w_in, w_conv, w_out, *, tm, seg, row_fn):
    n, d = x2d.shape
    win_spec, w_in = _weight(w_in)
    wout_spec, w_out = _weight(w_out)
    return pl.pallas_call(
        functools.partial(_shortconv_kernel, seg=seg),
        grid=(n // tm,),
        in_specs=[
            pl.BlockSpec((tm, d), lambda i: (i, 0)),
            pl.BlockSpec((None, 6, d), lambda i: (row_fn(i), 0, 0)),
            _const_spec((1, d)),
            win_spec,
            _const_spec((3, d)),
            wout_spec,
        ],
        out_specs=pl.BlockSpec((tm, d), lambda i: (i, 0)),
        out_shape=jax.ShapeDtypeStruct((n, d), F32),
        scratch_shapes=[pltpu.VMEM((tm, d), BF16)],
        compiler_params=_params(1),
        name="shortconv_mixer",
    )(x2d, mod_l, g, w_in, w_conv, w_out)


def _ffn_kernel(*refs, shift, seg, final, n_casts):
    (x_ref, mod_ref, g_ref, wup_ref, wconv_ref, bconv_ref, wdown_ref,
     gfin_ref) = refs[:8]
    o_ref = refs[8 + n_casts]
    act_ref = refs[-1]
    for src, dst in zip(refs[8:8 + n_casts], refs[9 + n_casts:-1]):
        dst[...] = src[...].astype(BF16)
    d = x_ref.shape[-1]
    x = x_ref[...].reshape(-1, d)
    hx = _modulate(x, g_ref[...], mod_ref[3:4, :], mod_ref[4:5, :]).astype(BF16)
    for c in range(D_FF // FF_CHUNK):
        lo, hi = c * FF_CHUNK, (c + 1) * FF_CHUNK
        val = _dot(hx, wup_ref[:, D_FF + lo:D_FF + hi])
        gate = _shift_conv(_dot(hx, wup_ref[:, lo:hi]), wconv_ref.at[:, lo:hi],
                           seg, shift)
        gate = gate + bconv_ref[:, lo:hi]
        act_ref[:, lo:hi] = (gate * _sigmoid(gate) * val).astype(BF16)
    tm = x.shape[0]
    pr = tm // DOWN_PIECES
    lead = o_ref.shape[0] // DOWN_PIECES
    for p in range(DOWN_PIECES):
        rows = slice(p * pr, (p + 1) * pr)
        y = _dot(act_ref[rows, :], wdown_ref[...])
        out = x[rows] + mod_ref[5:6, :] * y
        if final:
            out = _rmsnorm(out, gfin_ref[...])
        o_ref[p * lead:(p + 1) * lead] = out.reshape((lead,) + o_ref.shape[1:])


def _ffn(x2d, mod_l, g, w_up, w_conv, b_conv, w_down, g_final, *, batch,
         strip, final, casts=()):
    n, d = x2d.shape
    wup_spec, w_up = _weight(w_up)
    wdown_spec, w_down = _weight(w_down)
    per = n // batch
    if strip:
        rows = per // GRID_W
        strips = GRID_W // strip
        tm = rows * strip
        xin = x2d.reshape(batch * rows, GRID_W, d)
        x_spec = pl.BlockSpec((rows, strip, d),
                              lambda i: (i // strips, i % strips, 0))
        grid = (batch * strips,)
        row_fn = lambda i: i // strips
        shift = strip
    else:
        tm = per
        xin = x2d
        x_spec = pl.BlockSpec((tm, d), lambda i: (i, 0))
        grid = (batch,)
        row_fn = lambda i: batch
        shift = 1
    in_specs = [
        x_spec,
        pl.BlockSpec((None, 6, d), lambda i: (row_fn(i), 0, 0)),
        _const_spec((1, d)),
        wup_spec,
        _const_spec((3, D_FF)),
        _const_spec((1, D_FF)),
        wdown_spec,
        _const_spec((1, d)),
    ]
    args = [xin, mod_l, g, w_up, w_conv, b_conv, w_down, g_final]
    out_specs = [x_spec]
    out_shape = [jax.ShapeDtypeStruct(xin.shape, F32)]
    steps = grid[0]
    for w, layer in casts:
        slab = (w.shape[1] // steps, w.shape[2])
        in_specs.append(pl.BlockSpec(
            (None,) + slab, lambda i, first=layer * steps: (first + i, 0, 0)))
        args.append(w.reshape((w.shape[0] * steps,) + slab))
        out_specs.append(pl.BlockSpec((None,) + slab, lambda i: (i, 0, 0)))
        out_shape.append(jax.ShapeDtypeStruct((steps,) + slab, BF16))
    outs = pl.pallas_call(
        functools.partial(_ffn_kernel, shift=shift, seg=tm, final=final,
                          n_casts=len(casts)),
        grid=grid,
        in_specs=in_specs,
        out_specs=out_specs,
        out_shape=out_shape,
        scratch_shapes=[pltpu.VMEM((tm, D_FF), BF16)],
        compiler_params=_params(1),
        name="convglu_ffn",
    )(*args)
    out = outs[0].reshape(n, d)
    if not casts:
        return out
    return out, [o.reshape(w.shape[1:]) for o, (w, _) in zip(outs[1:], casts)]


def _log_sigmoid(v):
    return -(jnp.maximum(-v, 0.0) + jnp.log(1.0 + jnp.exp(-jnp.abs(v))))


def _chunk_scans(v, axis, chunk, op, identity):
    n = v.shape[axis]
    shape = (n, 1) if axis == 0 else (1, n)
    pos = lax.broadcasted_iota(jnp.int32, shape, axis) & (chunk - 1)
    pre, suf = v, v
    step = 1
    while step < chunk:
        pre = op(pre, jnp.where(pos >= step,
                                pltpu.roll(pre, step, axis=axis), identity))
        suf = op(suf, jnp.where(pos < chunk - step,
                                pltpu.roll(suf, n - step, axis=axis), identity))
        step *= 2
    return pre, suf


def _mlstm_proj_kernel(x_ref, mod_ref, g_ref, wqv_ref, wkt_ref, wg_ref,
                       wgt_ref, bg_ref, bgt_ref,
                       qv_ref, kt_ref, gc_ref, gr_ref):
    nt = (((1,), (1,)), ((), ()))
    half = N_GATE // 2
    lane = lax.broadcasted_iota(jnp.int32, (1, GATE_PAD), 1)
    fwd = (lane & (half - 1)) < N_HEADS
    sub = lax.broadcasted_iota(jnp.int32, (N_GATE, 1), 0)
    tm = x_ref.shape[0]
    pieces = ROW_PIECES if (tm // ROW_PIECES) % SCAN_CHUNK == 0 else 1
    pr = tm // pieces
    for p in range(pieces):
        rows = slice(p * pr, (p + 1) * pr)
        hx = _modulate(x_ref[rows, :], g_ref[...], mod_ref[0:1, :],
                       mod_ref[1:2, :]).astype(BF16)

        gc = _dot(hx, wg_ref[...]) + bg_ref[...]
        pre, suf = _chunk_scans(_log_sigmoid(gc), 0, SCAN_CHUNK, jnp.add, 0.0)
        b = jnp.where(fwd, pre, suf)
        u = gc - pltpu.roll(b, GATE_PAD - half, axis=1)
        pre, suf = _chunk_scans(u, 0, SCAN_CHUNK, jnp.maximum, -jnp.inf)
        gc_ref[rows, :] = jnp.where(lane < half, jnp.where(fwd, pre, suf), b)

        gr = lax.dot_general(wgt_ref[...], hx, nt, preferred_element_type=F32)
        gr = gr + bgt_ref[...]
        pre, suf = _chunk_scans(_log_sigmoid(gr), 1, SCAN_CHUNK, jnp.add, 0.0)
        b = jnp.where((sub & (half - 1)) < N_HEADS, pre, suf)
        gr_ref[0:half, rows] = gr[0:half] - b[half:]
        gr_ref[half:, rows] = b[half:]

        qv_ref[rows, :] = _dot(hx, wqv_ref[...]).astype(BF16)
        kt = lax.dot_general(wkt_ref[...], hx, nt, preferred_element_type=F32)
        kt_ref[:, rows] = (kt * (DK ** -0.5)).astype(BF16)


def _scan_kernel(*refs, rev, has_add, readout):
    it = iter(refs)
    qv_ref, kt_ref, gc_ref, gr_ref, c0_ref, m0_ref = (next(it) for _ in range(6))
    hadd_ref = next(it) if has_add else None
    if readout:
        x_ref, mod_ref, g_ref, wo_ref, gain_ref, wout_ref = (
            next(it) for _ in range(6))
    out_ref, cf_ref, mf_ref = (next(it) for _ in range(3))
    c_s, m_s, a_s, x_s, s_s, w_s, ktw_s = (next(it) for _ in range(7))
    if readout:
        h_s, z_s = next(it), next(it)
    h_ref = h_s if readout else out_ref
    assert not (rev and readout)
    t = pl.program_id(1)
    n_t = pl.num_programs(1)
    L = SCAN_CHUNK
    nq = N_HEADS * DK
    n_chunks = qv_ref.shape[0] // L
    half = N_GATE // 2

    @pl.when(t == 0)
    def _():
        c_s[...] = c0_ref[...]
        m_s[...] = m0_ref[...]

    ti = lax.broadcasted_iota(jnp.int32, (L, L), 0)
    si = lax.broadcasted_iota(jnp.int32, (L, L), 1)
    mask = (si >= ti) if rev else (si <= ti)
    gcol = N_HEADS if rev else 0
    ones = jnp.ones((L, DK), BF16)
    order = range(n_chunks - 1, -1, -1) if rev else range(n_chunks)

    units = [(c, h) for c in order for h in range(N_HEADS)]

    m_in, mx_of, decay_of = {}, {}, {}
    for h in range(N_HEADS):
        gi = gcol + h
        m_prev = m_s[h]
        for c in order:
            r0, r1 = c * L, (c + 1) * L
            u_row = gr_ref[gi:gi + 1, r0:r1]
            b_row = gr_ref[half + gi:half + gi + 1, r0:r1]
            g = b_row[:, 0:1] if rev else b_row[:, L - 1:L]
            mx = jnp.maximum(m_prev, jnp.max(u_row, axis=1, keepdims=True))
            m_in[c, h], mx_of[c, h] = m_prev, mx
            decay_of[c, h] = jnp.exp(m_prev - mx)
            m_prev = g + mx
        m_s[h] = m_prev

    for c, h in units:
        r0, r1 = c * L, (c + 1) * L
        q = qv_ref[r0:r1, h * DK:(h + 1) * DK]
        kt = kt_ref[h * DK:(h + 1) * DK, r0:r1]
        s_s[c, h] = _dot(q, kt)
        u_row = gr_ref[gcol + h:gcol + h + 1, r0:r1]
        ktw_s[c, h] = (kt.astype(F32) * jnp.exp(u_row - mx_of[c, h])).astype(BF16)
    o_piece = 2 if n_chunks % 2 == 0 else 1
    for i, (c, h) in enumerate(units):
        r0, r1 = c * L, (c + 1) * L
        gi = gcol + h
        u_row = gr_ref[gi:gi + 1, r0:r1]
        cm_bc = jnp.broadcast_to(gc_ref[r0:r1, gi:gi + 1], (L, DK))
        r = jnp.maximum(m_in[c, h], cm_bc)
        w = jnp.exp(jnp.where(mask, u_row - r, -jnp.inf)) * s_s[c, h]
        w_s[c, h] = w.astype(BF16)
        if readout and i % (o_piece * N_HEADS) == 0:
            p0 = (i // (o_piece * N_HEADS)) * o_piece * L
            rows = slice(p0, p0 + o_piece * L)
            hx = _modulate(x_ref[rows, :], g_ref[...], mod_ref[0:1, :],
                           mod_ref[1:2, :]).astype(BF16)
            out_ref[rows, :] = _dot(hx, wo_ref[...])

    for c in order:
        r0, r1 = c * L, (c + 1) * L
        for h in range(N_HEADS):
            v = qv_ref[r0:r1, nq + h * DV:nq + (h + 1) * DV]
            v_aug = jnp.concatenate([v, ones], axis=1)
            a_s[h] = _dot(w_s[c, h], v_aug)
            x_s[h] = _dot(ktw_s[c, h], v_aug)
        for h in range(N_HEADS):
            gi = gcol + h
            q = qv_ref[r0:r1, h * DK:(h + 1) * DK]
            m_prev = m_in[c, h]
            c_prev = c_s[h]
            a_aug = a_s[h]
            b_aug = _dot(q, c_prev.astype(BF16))
            cm_bc = jnp.broadcast_to(gc_ref[r0:r1, gi:gi + 1], (L, DK))
            b_bc = jnp.broadcast_to(gc_ref[r0:r1, half + gi:half + gi + 1],
                                    (L, DK))
            r = jnp.maximum(m_prev, cm_bc)
            wi = jnp.exp(m_prev - r)
            den = a_aug[:, DV:] + wi * b_aug[:, DV:]
            inv = 1.0 / jnp.maximum(jnp.abs(den), jnp.exp(-(b_bc + r)))
            beta = wi * inv
            for j in range(DV // DK):
                lo, hi = j * DK, (j + 1) * DK
                cols = slice(h * DV + lo, h * DV + hi)
                hh = inv * a_aug[:, lo:hi] + beta * b_aug[:, lo:hi]
                if has_add:
                    hh = hh + hadd_ref[r0:r1, cols]
                h_ref[r0:r1, cols] = hh
            for j in range((DV + DK) // DK):
                lo, hi = j * DK, (j + 1) * DK
                c_s[h, :, lo:hi] = (decay_of[c, h] * c_prev[:, lo:hi]
                                    + x_s[h, :, lo:hi])

    @pl.when(t == n_t - 1)
    def _():
        cf_ref[...] = c_s[...]
        mf_ref[...] = m_s[...]

    if readout:
        pr = o_piece * L
        for p in range(n_chunks // o_piece):
            rows = slice(p * pr, (p + 1) * pr)
            for h in range(N_HEADS):
                lo, hi = h * DV, (h + 1) * DV
                hh = h_s[rows, lo:hi]
                ms = jnp.mean(hh * hh, axis=-1, keepdims=True)
                hn = hh * lax.rsqrt(ms + EPS) * gain_ref[:, lo:hi]
                z_s[rows, lo:hi] = (hn * _sigmoid(out_ref[rows, lo:hi])
                                    ).astype(BF16)
            y = _dot(z_s[rows, :], wout_ref[...])
            out_ref[rows, :] = x_ref[rows, :] + mod_ref[2:3, :] * y


def _scan_scratch(n_chunks):
    return [
        pltpu.VMEM((N_HEADS, DK, DV + DK), F32),
        pltpu.VMEM((N_HEADS, 1, DK), F32),
        pltpu.VMEM((N_HEADS, SCAN_CHUNK, DV + DK), F32),
        pltpu.VMEM((N_HEADS, DK, DV + DK), F32),
        pltpu.VMEM((n_chunks, N_HEADS, SCAN_CHUNK, SCAN_CHUNK), F32),
        pltpu.VMEM((n_chunks, N_HEADS, SCAN_CHUNK, SCAN_CHUNK), BF16),
        pltpu.VMEM((n_chunks, N_HEADS, DK, SCAN_CHUNK), BF16),
    ]


def _scan(qv, kt, gc, gr, state, *, batch, ts, rev, add=None, readout=None):
    n = qv.shape[0]
    n_t = n // batch // ts
    n_chunks = ts // SCAN_CHUNK
    nq = N_HEADS * DK
    nv = N_HEADS * DV
    nqv = nq + nv

    def blk(b, t):
        return b * n_t + ((n_t - 1 - t) if rev else t)

    st_shapes = [(N_HEADS, DK, DV + DK), (N_HEADS, 1, DK)]
    st_specs = [pl.BlockSpec((None,) + s, lambda b, t: (b, 0, 0, 0))
                for s in st_shapes]
    h_spec = pl.BlockSpec((ts, nv), lambda b, t: (blk(b, t), 0))
    has_add = add is not None
    in_specs = [
        pl.BlockSpec((ts, nqv), lambda b, t: (blk(b, t), 0)),
        pl.BlockSpec((nq, ts), lambda b, t: (0, blk(b, t))),
        pl.BlockSpec((ts, GATE_PAD), lambda b, t: (blk(b, t), 0)),
        pl.BlockSpec((N_GATE, ts), lambda b, t: (0, blk(b, t))),
    ] + st_specs
    args = [qv, kt, gc, gr, *state]
    scratch = _scan_scratch(n_chunks)
    if has_add:
        in_specs.append(h_spec)
        args.append(add)
    out_cols = nv
    if readout is not None:
        x2d, mod_l, mod_row_fn, g, w_o, gain, w_out = readout
        wout_spec, w_out = _weight(w_out)
        d = x2d.shape[1]
        out_cols = d
        in_specs += [
            pl.BlockSpec((ts, d), lambda b, t: (blk(b, t), 0)),
            pl.BlockSpec((None, 6, d), lambda b, t: (mod_row_fn(b), 0, 0)),
            _const_spec((1, d)),
            _const_spec((d, nv)),
            _const_spec((1, nv)),
            wout_spec,
        ]
        args += [x2d, mod_l, g, w_o, gain, w_out]
        scratch += [pltpu.VMEM((ts, nv), F32), pltpu.VMEM((ts, nv), BF16)]
    out, cf, mf = pl.pallas_call(
        functools.partial(_scan_kernel, rev=rev, has_add=has_add,
                          readout=readout is not None),
        grid=(batch, n_t),
        in_specs=in_specs,
        out_specs=[pl.BlockSpec((ts, out_cols), lambda b, t: (blk(b, t), 0))]
        + st_specs,
        out_shape=[jax.ShapeDtypeStruct((n, out_cols), F32)]
        + [jax.ShapeDtypeStruct((batch,) + s, F32) for s in st_shapes],
        scratch_shapes=scratch,
        compiler_params=_params(2),
        name="mlstm_scan_bwd" if rev else "mlstm_scan_fwd",
    )(*args)
    return out, (cf, mf)


N_PROJ_IN = 9
N_PROJ_OUT = 4


def _proj_scan_kernel(*refs):
    proj_in = refs[:N_PROJ_IN]
    c0_ref, m0_ref = refs[N_PROJ_IN:N_PROJ_IN + 2]
    outs = refs[N_PROJ_IN + 2:]
    proj_out = outs[:N_PROJ_OUT]
    _mlstm_proj_kernel(*proj_in, *proj_out)
    _scan_kernel(*proj_out, c0_ref, m0_ref, *outs[N_PROJ_OUT:], rev=True,
                 has_add=False, readout=False)


def _proj_scan(x2d, mod_l, mod_row_fn, g, wts, state, *, batch, ts):
    n, d = x2d.shape
    n_t = n // batch // ts
    n_chunks = ts // SCAN_CHUNK
    nq = N_HEADS * DK
    nv = N_HEADS * DV
    nqv = nq + nv

    def blk(b, t):
        return b * n_t + (n_t - 1 - t)

    rows = lambda w: pl.BlockSpec((ts, w), lambda b, t: (blk(b, t), 0))
    cols = lambda h: pl.BlockSpec((h, ts), lambda b, t: (0, blk(b, t)))
    st_shapes = [(N_HEADS, DK, DV + DK), (N_HEADS, 1, DK)]
    st_specs = [pl.BlockSpec((None,) + s, lambda b, t: (b, 0, 0, 0))
                for s in st_shapes]
    qv, kt, gc, gr, h, cf, mf = pl.pallas_call(
        _proj_scan_kernel,
        grid=(batch, n_t),
        in_specs=[
            rows(d),
            pl.BlockSpec((None, 6, d), lambda b, t: (mod_row_fn(b), 0, 0)),
            _const_spec((1, d)),
            _const_spec((d, nqv)),
            _const_spec((nq, d)),
            _const_spec((d, GATE_PAD)),
            _const_spec((N_GATE, d)),
            _const_spec((1, GATE_PAD)),
            _const_spec((N_GATE, 1)),
        ] + st_specs,
        out_specs=[rows(nqv), cols(nq), rows(GATE_PAD), cols(N_GATE), rows(nv)]
        + st_specs,
        out_shape=[
            jax.ShapeDtypeStruct((n, nqv), BF16),
            jax.ShapeDtypeStruct((nq, n), BF16),
            jax.ShapeDtypeStruct((n, GATE_PAD), F32),
            jax.ShapeDtypeStruct((N_GATE, n), F32),
            jax.ShapeDtypeStruct((n, nv), F32),
        ] + [jax.ShapeDtypeStruct((batch,) + s, F32) for s in st_shapes],
        scratch_shapes=_scan_scratch(n_chunks),
        compiler_params=_params(2),
        name="mlstm_proj_scan_bwd",
    )(x2d, mod_l, g, wts["w_qv"], wts["w_kt"], wts["w_g"], wts["w_gt"],
      wts["b_g"], wts["b_gt"], *state)
    return (qv, kt, gc, gr), h, (cf, mf)


def _mlstm_weights(w_in, b_gate):
    nq = N_HEADS * DK
    o2 = 2 * nq
    o3 = o2 + N_HEADS * DV
    o4 = o3 + D_MODEL
    w_g = w_in[:, o4:]
    return {
        "w_qv": jnp.concatenate([w_in[:, :nq], w_in[:, o2:o3]], axis=1).astype(BF16),
        "w_kt": w_in[:, nq:o2].T.astype(BF16),
        "w_o": w_in[:, o3:o4].astype(BF16),
        "w_g": jnp.pad(w_g, ((0, 0), (0, GATE_PAD - N_GATE))).astype(BF16),
        "w_gt": w_g.T.astype(BF16),
        "b_g": jnp.pad(b_gate, (0, GATE_PAD - N_GATE)).reshape(1, GATE_PAD),
        "b_gt": b_gate.reshape(N_GATE, 1),
    }


def kernel(x, c, ctx, c_ctx, w_mod, b_mod, g_mix, g_ffn, a_w_in, a_b_gate,
           a_head_gain, a_w_out, b_w_in, b_w_conv, b_w_out, f_w_up, f_w_conv,
           f_b_conv, f_w_down, g_final):
    bn, t, d = x.shape
    n_ctx = ctx.shape[1]
    depth = w_mod.shape[0]
    n_mixers = 2
    assert d == D_MODEL and bn <= 7 and t % TOKEN_TILE == 0
    assert n_ctx % SCAN_CHUNK == 0 and TOKEN_TILE % SCAN_CHUNK == 0

    rec_layers = [i for i in range(depth) if i % n_mixers == 0]
    last_rec = max(rec_layers) if rec_layers else -1

    cc = jnp.zeros((8, d), F32).at[:bn].set(c).at[bn].set(c_ctx)
    mod = _mod_all(cc, w_mod, b_mod).reshape(depth, 8, 6, d)

    tm = TOKEN_TILE
    tiles_per_batch = t // tm
    lat_row = lambda i: i // tiles_per_batch
    ctx_row = lambda i: bn

    xs = x.reshape(bn * t, d)
    cs = ctx.reshape(bn * n_ctx, d)
    zero_state = (jnp.zeros((bn, N_HEADS, DK, DV + DK), F32),
                  jnp.zeros((bn, N_HEADS, 1, DK), F32))

    a_w_out_b = a_w_out.astype(BF16)
    ffn_w = (f_w_up[0].astype(BF16), f_w_down[0].astype(BF16))
    conv_w = None

    for i in range(depth):
        j = i // n_mixers
        ctx_read = i <= last_rec
        ctx_live = i < last_rec
        gm = g_mix[i].reshape(1, d)
        gf = g_ffn[i].reshape(1, d)
        if i % n_mixers == 0:
            wts = _mlstm_weights(a_w_in[j], a_b_gate[j])
            w_out = (a_w_out_b, j)
            gain = a_head_gain[j].reshape(1, N_HEADS * DV)
            s_f = s_b = zero_state
            if ctx_read:
                pc, hc, s_b = _proj_scan(cs, mod[i], lambda b: bn, gm, wts,
                                         zero_state, batch=bn, ts=n_ctx)
                ro = ((cs, mod[i], lambda b: bn, gm, wts["w_o"], gain, w_out)
                      if ctx_live else None)
                cs_new, s_f = _scan(*pc, zero_state, batch=bn, ts=n_ctx,
                                    rev=False, add=hc, readout=ro)
            px, hl, _ = _proj_scan(xs, mod[i], lambda b: b, gm, wts, s_b,
                                   batch=bn, ts=tm)
            ro = (xs, mod[i], lambda b: b, gm, wts["w_o"], gain, w_out)
            xs, _ = _scan(*px, s_f, batch=bn, ts=tm, rev=False, add=hl,
                          readout=ro)
            if ctx_live:
                cs = cs_new
        else:
            w_in, w_out = conv_w
            xs = _shortconv(xs, mod[i], gm, w_in, b_w_conv[j], w_out,
                            tm=tm, seg=GRID_W, row_fn=lat_row)
            if ctx_live:
                cs = _shortconv(cs, mod[i], gm, w_in, b_w_conv[j], w_out,
                                tm=n_ctx, seg=n_ctx, row_fn=ctx_row)
        w_up, w_down = ffn_w
        b_conv = f_b_conv[i].reshape(1, D_FF)
        gfin = g_final.reshape(1, d)
        if i + 1 < depth:
            casts = [(f_w_up, i + 1), (f_w_down, i + 1)]
            if (i + 1) % n_mixers == 1:
                casts += [(b_w_in, (i + 1) // n_mixers),
                          (b_w_out, (i + 1) // n_mixers)]
            xs, cast = _ffn(xs, mod[i], gf, w_up, f_w_conv[i], b_conv, w_down,
                            gfin, batch=bn, strip=FFN_STRIP, final=False,
                            casts=casts)
            ffn_w, conv_w = cast[:2], cast[2:]
        else:
            xs = _ffn(xs, mod[i], gf, w_up, f_w_conv[i], b_conv, w_down, gfin,
                      batch=bn, strip=FFN_STRIP, final=True)
        if ctx_live:
            cs = _ffn(cs, mod[i], gf, w_up, f_w_conv[i], b_conv, w_down, gfin,
                      batch=bn, strip=0, final=False)

    return xs.reshape(bn, t, d)
```

```python
import functools

import jax
import jax.numpy as jnp
from jax import lax
from jax.experimental import pallas as pl
from jax.experimental.pallas import tpu as pltpu

F32 = jnp.float32
BF16 = jnp.bfloat16

D_MODEL = 1024
GRID_W = 64
N_HEADS = 4
DK = 128
DV = 256
D_FF = 2816
EPS = 1e-6
N_GATE = 16
GATE_PAD = 128

SCAN_CHUNK = 128
FF_CHUNK = 256
DOWN_PIECES = 2
ROW_PIECES = 2
TOKEN_TILE = 1024
FFN_STRIP = 8
VMEM_LIMIT = 56 * 1024 * 1024


def _params(n_axes):
    return pltpu.CompilerParams(
        dimension_semantics=("arbitrary",) * n_axes,
        vmem_limit_bytes=VMEM_LIMIT)


def _const_spec(shape):
    zeros = (0,) * len(shape)
    return pl.BlockSpec(shape, lambda *_: zeros, pipeline_mode=pl.Buffered(1))


def _weight(w):
    if isinstance(w, tuple):
        arr, layer = w
        idx = (layer,) + (0,) * (arr.ndim - 1)
        spec = pl.BlockSpec((None,) + arr.shape[1:], lambda *_: idx,
                            pipeline_mode=pl.Buffered(1))
        return spec, arr
    return _const_spec(w.shape), w


def _dot(a, b):
    return jnp.dot(a, b, preferred_element_type=F32)


def _sigmoid(v):
    return 1.0 / (1.0 + jnp.exp(-v))


def _rmsnorm(x, g):
    ms = jnp.mean(x * x, axis=-1, keepdims=True)
    return (x * lax.rsqrt(ms + EPS)) * g


def _modulate(x, g, shift, scale):
    return _rmsnorm(x, g) * (1.0 + scale) + shift


def _shift_conv(u, w_ref, seg, shift=1):
    n = u.shape[0]
    if seg == n and shift % 8 == 0:
        pad = jnp.zeros((shift, u.shape[1]), u.dtype)
        prev = jnp.concatenate([pad, u[:n - shift]], axis=0)
        nxt = jnp.concatenate([u[shift:], pad], axis=0)
    else:
        pos = lax.broadcasted_iota(jnp.int32, (n, 1), 0) & (seg - 1)
        prev = jnp.where(pos >= shift, pltpu.roll(u, shift, axis=0), 0.0)
        nxt = jnp.where(pos < seg - shift, pltpu.roll(u, n - shift, axis=0), 0.0)
    return prev * w_ref[0:1, :] + u * w_ref[1:2, :] + nxt * w_ref[2:3, :]


def _mod_kernel(cc_ref, w_ref, b_ref, o_ref):
    cc = cc_ref[...]
    s = cc * _sigmoid(cc)
    o_ref[...] = _dot(s, w_ref[...]) + b_ref[...]


def _mod_all(cc, w_mod, b_mod):
    depth, d, n = w_mod.shape
    tn = 1536
    return pl.pallas_call(
        _mod_kernel,
        grid=(depth, n // tn),
        in_specs=[
            pl.BlockSpec((8, d), lambda l, j: (0, 0)),
            pl.BlockSpec((None, d, tn), lambda l, j: (l, 0, j)),
            pl.BlockSpec((None, 1, tn), lambda l, j: (l, 0, j)),
        ],
        out_specs=pl.BlockSpec((None, 8, tn), lambda l, j: (l, 0, j)),
        out_shape=jax.ShapeDtypeStruct((depth, 8, n), F32),
        compiler_params=_params(2),
        name="adaln_mod",
    )(cc, w_mod, b_mod.reshape(depth, 1, n))


def _shortconv_kernel(x_ref, mod_ref, g_ref, win_ref, wconv_ref, wout_ref,
                      o_ref, z_ref, *, seg):
    d = D_MODEL
    cw = 256
    tm = x_ref.shape[0]
    pieces = ROW_PIECES if (tm // ROW_PIECES) % seg == 0 else 1
    pr = tm // pieces
    for p in range(pieces):
        rows = slice(p * pr, (p + 1) * pr)
        x = x_ref[rows, :]
        hx = _modulate(x, g_ref[...], mod_ref[0:1, :], mod_ref[1:2, :]).astype(BF16)
        for j in range(d // cw):
            lo, hi = j * cw, (j + 1) * cw
            bg = _dot(hx, win_ref[:, lo:hi])
            cg = _dot(hx, win_ref[:, d + lo:d + hi])
            xv = _dot(hx, win_ref[:, 2 * d + lo:2 * d + hi])
            cv = _shift_conv(cg * xv, wconv_ref.at[:, lo:hi], seg)
            z_ref[rows, lo:hi] = (bg * cv).astype(BF16)
        y = _dot(z_ref[rows, :], wout_ref[...])
        o_ref[rows, :] = x + mod_ref[2:3, :] * y


def _shortconv(x2d, mod_l, g, w_in, w_conv, w_out, *, tm, seg, row_fn):
    n, d = x2d.shape
    win_spec, w_in = _weight(w_in)
    wout_spec, w_out = _weight(w_out)
    return pl.pallas_call(
        functools.partial(_shortconv_kernel, seg=seg),
        grid=(n // tm,),
        in_specs=[
            pl.BlockSpec((tm, d), lambda i: (i, 0)),
            pl.BlockSpec((None, 6, d), lambda i: (row_fn(i), 0, 0)),
            _const_spec((1, d)),
            win_spec,
            _const_spec((3, d)),
            wout_spec,
        ],
        out_specs=pl.BlockSpec((tm, d), lambda i: (i, 0)),
        out_shape=jax.ShapeDtypeStruct((n, d), F32),
        scratch_shapes=[pltpu.VMEM((tm, d), BF16)],
        compiler_params=_params(1),
        name="shortconv_mixer",
    )(x2d, mod_l, g, w_in, w_conv, w_out)


def _ffn_kernel(*refs, shift, seg, final, n_casts):
    (x_ref, mod_ref, g_ref, wup_ref, wconv_ref, bconv_ref, wdown_ref,
     gfin_ref) = refs[:8]
    o_ref = refs[8 + n_casts]
    act_ref = refs[-1]
    for src, dst in zip(refs[8:8 + n_casts], refs[9 + n_casts:-1]):
        dst[...] = src[...].astype(BF16)
    d = x_ref.shape[-1]
    x = x_ref[...].reshape(-1, d)
    hx = _modulate(x, g_ref[...], mod_ref[3:4, :], mod_ref[4:5, :]).astype(BF16)
    for c in range(D_FF // FF_CHUNK):
        lo, hi = c * FF_CHUNK, (c + 1) * FF_CHUNK
        val = _dot(hx, wup_ref[:, D_FF + lo:D_FF + hi])
        gate = _shift_conv(_dot(hx, wup_ref[:, lo:hi]), wconv_ref.at[:, lo:hi],
                           seg, shift)
        gate = gate + bconv_ref[:, lo:hi]
        act_ref[:, lo:hi] = (gate * _sigmoid(gate) * val).astype(BF16)
    tm = x.shape[0]
    pr = tm // DOWN_PIECES
    lead = o_ref.shape[0] // DOWN_PIECES
    for p in range(DOWN_PIECES):
        rows = slice(p * pr, (p + 1) * pr)
        y = _dot(act_ref[rows, :], wdown_ref[...])
        out = x[rows] + mod_ref[5:6, :] * y
        if final:
            out = _rmsnorm(out, gfin_ref[...])
        o_ref[p * lead:(p + 1) * lead] = out.reshape((lead,) + o_ref.shape[1:])


def _ffn(x2d, mod_l, g, w_up, w_conv, b_conv, w_down, g_final, *, batch,
         strip, final, casts=()):
    n, d = x2d.shape
    wup_spec, w_up = _weight(w_up)
    wdown_spec, w_down = _weight(w_down)
    per = n // batch
    if strip:
        rows = per // GRID_W
        strips = GRID_W // strip
        tm = rows * strip
        xin = x2d.reshape(batch * rows, GRID_W, d)
        x_spec = pl.BlockSpec((rows, strip, d),
                              lambda i: (i // strips, i % strips, 0))
        grid = (batch * strips,)
        row_fn = lambda i: i // strips
        shift = strip
    else:
        tm = per
        xin = x2d
        x_spec = pl.BlockSpec((tm, d), lambda i: (i, 0))
        grid = (batch,)
        row_fn = lambda i: batch
        shift = 1
    in_specs = [
        x_spec,
        pl.BlockSpec((None, 6, d), lambda i: (row_fn(i), 0, 0)),
        _const_spec((1, d)),
        wup_spec,
        _const_spec((3, D_FF)),
        _const_spec((1, D_FF)),
        wdown_spec,
        _const_spec((1, d)),
    ]
    args = [xin, mod_l, g, w_up, w_conv, b_conv, w_down, g_final]
    out_specs = [x_spec]
    out_shape = [jax.ShapeDtypeStruct(xin.shape, F32)]
    steps = grid[0]
    for w, layer in casts:
        slab = (w.shape[1] // steps, w.shape[2])
        in_specs.append(pl.BlockSpec(
            (None,) + slab, lambda i, first=layer * steps: (first + i, 0, 0)))
        args.append(w.reshape((w.shape[0] * steps,) + slab))
        out_specs.append(pl.BlockSpec((None,) + slab, lambda i: (i, 0, 0)))
        out_shape.append(jax.ShapeDtypeStruct((steps,) + slab, BF16))
    outs = pl.pallas_call(
        functools.partial(_ffn_kernel, shift=shift, seg=tm, final=final,
                          n_casts=len(casts)),
        grid=grid,
        in_specs=in_specs,
        out_specs=out_specs,
        out_shape=out_shape,
        scratch_shapes=[pltpu.VMEM((tm, D_FF), BF16)],
        compiler_params=_params(1),
        name="convglu_ffn",
    )(*args)
    out = outs[0].reshape(n, d)
    if not casts:
        return out
    return out, [o.reshape(w.shape[1:]) for o, (w, _) in zip(outs[1:], casts)]


def _log_sigmoid(v):
    return -(jnp.maximum(-v, 0.0) + jnp.log(1.0 + jnp.exp(-jnp.abs(v))))


def _chunk_scans(v, axis, chunk, op, identity):
    n = v.shape[axis]
    shape = (n, 1) if axis == 0 else (1, n)
    pos = lax.broadcasted_iota(jnp.int32, shape, axis) & (chunk - 1)
    pre, suf = v, v
    step = 1
    while step < chunk:
        pre = op(pre, jnp.where(pos >= step,
                                pltpu.roll(pre, step, axis=axis), identity))
        suf = op(suf, jnp.where(pos < chunk - step,
                                pltpu.roll(suf, n - step, axis=axis), identity))
        step *= 2
    return pre, suf


def _mlstm_proj_kernel(x_ref, mod_ref, g_ref, wqv_ref, wkt_ref, wg_ref,
                       wgt_ref, bg_ref, bgt_ref,
                       qv_ref, kt_ref, gc_ref, gr_ref):
    nt = (((1,), (1,)), ((), ()))
    half = N_GATE // 2
    lane = lax.broadcasted_iota(jnp.int32, (1, GATE_PAD), 1)
    fwd = (lane & (half - 1)) < N_HEADS
    sub = lax.broadcasted_iota(jnp.int32, (N_GATE, 1), 0)
    tm = x_ref.shape[0]
    pieces = ROW_PIECES if (tm // ROW_PIECES) % SCAN_CHUNK == 0 else 1
    pr = tm // pieces
    for p in range(pieces):
        rows = slice(p * pr, (p + 1) * pr)
        hx = _modulate(x_ref[rows, :], g_ref[...], mod_ref[0:1, :],
                       mod_ref[1:2, :]).astype(BF16)

        gc = _dot(hx, wg_ref[...]) + bg_ref[...]
        pre, suf = _chunk_scans(_log_sigmoid(gc), 0, SCAN_CHUNK, jnp.add, 0.0)
        b = jnp.where(fwd, pre, suf)
        u = gc - pltpu.roll(b, GATE_PAD - half, axis=1)
        pre, suf = _chunk_scans(u, 0, SCAN_CHUNK, jnp.maximum, -jnp.inf)
        gc_ref[rows, :] = jnp.where(lane < half, jnp.where(fwd, pre, suf), b)

        gr = lax.dot_general(wgt_ref[...], hx, nt, preferred_element_type=F32)
        gr = gr + bgt_ref[...]
        pre, suf = _chunk_scans(_log_sigmoid(gr), 1, SCAN_CHUNK, jnp.add, 0.0)
        b = jnp.where((sub & (half - 1)) < N_HEADS, pre, suf)
        gr_ref[0:half, rows] = gr[0:half] - b[half:]
        gr_ref[half:, rows] = b[half:]

        qv_ref[rows, :] = _dot(hx, wqv_ref[...]).astype(BF16)
        kt = lax.dot_general(wkt_ref[...], hx, nt, preferred_element_type=F32)
        kt_ref[:, rows] = (kt * (DK ** -0.5)).astype(BF16)


def _scan_kernel(*refs, rev, has_add, readout):
    it = iter(refs)
    qv_ref, kt_ref, gc_ref, gr_ref, c0_ref, m0_ref = (next(it) for _ in range(6))
    hadd_ref = next(it) if has_add else None
    if readout:
        x_ref, mod_ref, g_ref, wo_ref, gain_ref, wout_ref = (
            next(it) for _ in range(6))
    out_ref, cf_ref, mf_ref = (next(it) for _ in range(3))
    c_s, m_s, a_s, x_s, s_s, w_s, ktw_s = (next(it) for _ in range(7))
    if readout:
        h_s, z_s = next(it), next(it)
    h_ref = h_s if readout else out_ref
    assert not (rev and readout)
    t = pl.program_id(1)
    n_t = pl.num_programs(1)
    L = SCAN_CHUNK
    nq = N_HEADS * DK
    n_chunks = qv_ref.shape[0] // L
    half = N_GATE // 2

    @pl.when(t == 0)
    def _():
        c_s[...] = c0_ref[...]
        m_s[...] = m0_ref[...]

    ti = lax.broadcasted_iota(jnp.int32, (L, L), 0)
    si = lax.broadcasted_iota(jnp.int32, (L, L), 1)
    mask = (si >= ti) if rev else (si <= ti)
    gcol = N_HEADS if rev else 0
    ones = jnp.ones((L, DK), BF16)
    order = range(n_chunks - 1, -1, -1) if rev else range(n_chunks)

    units = [(c, h) for c in order for h in range(N_HEADS)]

    m_in, mx_of, decay_of = {}, {}, {}
    for h in range(N_HEADS):
        gi = gcol + h
        m_prev = m_s[h]
        for c in order:
            r0, r1 = c * L, (c + 1) * L
            u_row = gr_ref[gi:gi + 1, r0:r1]
            b_row = gr_ref[half + gi:half + gi + 1, r0:r1]
            g = b_row[:, 0:1] if rev else b_row[:, L - 1:L]
            mx = jnp.maximum(m_prev, jnp.max(u_row, axis=1, keepdims=True))
            m_in[c, h], mx_of[c, h] = m_prev, mx
            decay_of[c, h] = jnp.exp(m_prev - mx)
            m_prev = g + mx
        m_s[h] = m_prev

    for c, h in units:
        r0, r1 = c * L, (c + 1) * L
        q = qv_ref[r0:r1, h * DK:(h + 1) * DK]
        kt = kt_ref[h * DK:(h + 1) * DK, r0:r1]
        s_s[c, h] = _dot(q, kt)
        u_row = gr_ref[gcol + h:gcol + h + 1, r0:r1]
        ktw_s[c, h] = (kt.astype(F32) * jnp.exp(u_row - mx_of[c, h])).astype(BF16)
    o_piece = 2 if n_chunks % 2 == 0 else 1
    for i, (c, h) in enumerate(units):
        r0, r1 = c * L, (c + 1) * L
        gi = gcol + h
        u_row = gr_ref[gi:gi + 1, r0:r1]
        cm_bc = jnp.broadcast_to(gc_ref[r0:r1, gi:gi + 1], (L, DK))
        r = jnp.maximum(m_in[c, h], cm_bc)
        w = jnp.exp(jnp.where(mask, u_row - r, -jnp.inf)) * s_s[c, h]
        w_s[c, h] = w.astype(BF16)
        if readout and i % (o_piece * N_HEADS) == 0:
            p0 = (i // (o_piece * N_HEADS)) * o_piece * L
            rows = slice(p0, p0 + o_piece * L)
            hx = _modulate(x_ref[rows, :], g_ref[...], mod_ref[0:1, :],
                           mod_ref[1:2, :]).astype(BF16)
            out_ref[rows, :] = _dot(hx, wo_ref[...])

    for c in order:
        r0, r1 = c * L, (c + 1) * L
        for h in range(N_HEADS):
            v = qv_ref[r0:r1, nq + h * DV:nq + (h + 1) * DV]
            v_aug = jnp.concatenate([v, ones], axis=1)
            a_s[h] = _dot(w_s[c, h], v_aug)
            x_s[h] = _dot(ktw_s[c, h], v_aug)
        for h in range(N_HEADS):
            gi = gcol + h
            q = qv_ref[r0:r1, h * DK:(h + 1) * DK]
            m_prev = m_in[c, h]
            c_prev = c_s[h]
            a_aug = a_s[h]
            b_aug = _dot(q, c_prev.astype(BF16))
            cm_bc = jnp.broadcast_to(gc_ref[r0:r1, gi:gi + 1], (L, DK))
            b_bc = jnp.broadcast_to(gc_ref[r0:r1, half + gi:half + gi + 1],
                                    (L, DK))
            r = jnp.maximum(m_prev, cm_bc)
            wi = jnp.exp(m_prev - r)
            den = a_aug[:, DV:] + wi * b_aug[:, DV:]
            inv = 1.0 / jnp.maximum(jnp.abs(den), jnp.exp(-(b_bc + r)))
            beta = wi * inv
            for j in range(DV // DK):
                lo, hi = j * DK, (j + 1) * DK
                cols = slice(h * DV + lo, h * DV + hi)
                hh = inv * a_aug[:, lo:hi] + beta * b_aug[:, lo:hi]
                if has_add:
                    hh = hh + hadd_ref[r0:r1, cols]
                h_ref[r0:r1, cols] = hh
            for j in range((DV + DK) // DK):
                lo, hi = j * DK, (j + 1) * DK
                c_s[h, :, lo:hi] = (decay_of[c, h] * c_prev[:, lo:hi]
                                    + x_s[h, :, lo:hi])

    @pl.when(t == n_t - 1)
    def _():
        cf_ref[...] = c_s[...]
        mf_ref[...] = m_s[...]

    if readout:
        pr = o_piece * L
        for p in range(n_chunks // o_piece):
            rows = slice(p * pr, (p + 1) * pr)
            for h in range(N_HEADS):
                lo, hi = h * DV, (h + 1) * DV
                hh = h_s[rows, lo:hi]
                ms = jnp.mean(hh * hh, axis=-1, keepdims=True)
                hn = hh * lax.rsqrt(ms + EPS) * gain_ref[:, lo:hi]
                z_s[rows, lo:hi] = (hn * _sigmoid(out_ref[rows, lo:hi])
                                    ).astype(BF16)
            y = _dot(z_s[rows, :], wout_ref[...])
            out_ref[rows, :] = x_ref[rows, :] + mod_ref[2:3, :] * y


def _scan_scratch(n_chunks):
    return [
        pltpu.VMEM((N_HEADS, DK, DV + DK), F32),
        pltpu.VMEM((N_HEADS, 1, DK), F32),
        pltpu.VMEM((N_HEADS, SCAN_CHUNK, DV + DK), F32),
        pltpu.VMEM((N_HEADS, DK, DV + DK), F32),
        pltpu.VMEM((n_chunks, N_HEADS, SCAN_CHUNK, SCAN_CHUNK), F32),
        pltpu.VMEM((n_chunks, N_HEADS, SCAN_CHUNK, SCAN_CHUNK), BF16),
        pltpu.VMEM((n_chunks, N_HEADS, DK, SCAN_CHUNK), BF16),
    ]


def _scan(qv, kt, gc, gr, state, *, batch, ts, rev, add=None, readout=None):
    n = qv.shape[0]
    n_t = n // batch // ts
    n_chunks = ts // SCAN_CHUNK
    nq = N_HEADS * DK
    nv = N_HEADS * DV
    nqv = nq + nv

    def blk(b, t):
        return b * n_t + ((n_t - 1 - t) if rev else t)

    st_shapes = [(N_HEADS, DK, DV + DK), (N_HEADS, 1, DK)]
    st_specs = [pl.BlockSpec((None,) + s, lambda b, t: (b, 0, 0, 0))
                for s in st_shapes]
    h_spec = pl.BlockSpec((ts, nv), lambda b, t: (blk(b, t), 0))
    has_add = add is not None
    in_specs = [
        pl.BlockSpec((ts, nqv), lambda b, t: (blk(b, t), 0)),
        pl.BlockSpec((nq, ts), lambda b, t: (0, blk(b, t))),
        pl.BlockSpec((ts, GATE_PAD), lambda b, t: (blk(b, t), 0)),
        pl.BlockSpec((N_GATE, ts), lambda b, t: (0, blk(b, t))),
    ] + st_specs
    args = [qv, kt, gc, gr, *state]
    scratch = _scan_scratch(n_chunks)
    if has_add:
        in_specs.append(h_spec)
        args.append(add)
    out_cols = nv
    if readout is not None:
        x2d, mod_l, mod_row_fn, g, w_o, gain, w_out = readout
        wout_spec, w_out = _weight(w_out)
        d = x2d.shape[1]
        out_cols = d
        in_specs += [
            pl.BlockSpec((ts, d), lambda b, t: (blk(b, t), 0)),
            pl.BlockSpec((None, 6, d), lambda b, t: (mod_row_fn(b), 0, 0)),
            _const_spec((1, d)),
            _const_spec((d, nv)),
            _const_spec((1, nv)),
            wout_spec,
        ]
        args += [x2d, mod_l, g, w_o, gain, w_out]
        scratch += [pltpu.VMEM((ts, nv), F32), pltpu.VMEM((ts, nv), BF16)]
    out, cf, mf = pl.pallas_call(
        functools.partial(_scan_kernel, rev=rev, has_add=has_add,
                          readout=readout is not None),
        grid=(batch, n_t),
        in_specs=in_specs,
        out_specs=[pl.BlockSpec((ts, out_cols), lambda b, t: (blk(b, t), 0))]
        + st_specs,
        out_shape=[jax.ShapeDtypeStruct((n, out_cols), F32)]
        + [jax.ShapeDtypeStruct((batch,) + s, F32) for s in st_shapes],
        scratch_shapes=scratch,
        compiler_params=_params(2),
        name="mlstm_scan_bwd" if rev else "mlstm_scan_fwd",
    )(*args)
    return out, (cf, mf)


N_PROJ_IN = 9
N_PROJ_OUT = 4


def _proj_scan_kernel(*refs, n_casts):
    proj_in = refs[:N_PROJ_IN]
    c0_ref, m0_ref = refs[N_PROJ_IN:N_PROJ_IN + 2]
    cast_in = refs[N_PROJ_IN + 2:N_PROJ_IN + 2 + n_casts]
    outs = refs[N_PROJ_IN + 2 + n_casts:]
    proj_out = outs[:N_PROJ_OUT]
    n_scan_out = 3
    cast_out = outs[N_PROJ_OUT + n_scan_out:N_PROJ_OUT + n_scan_out + n_casts]
    scan_refs = (outs[N_PROJ_OUT:N_PROJ_OUT + n_scan_out]
                 + outs[N_PROJ_OUT + n_scan_out + n_casts:])
    for src, dst in zip(cast_in, cast_out):
        dst[...] = src[...].astype(BF16)
    _mlstm_proj_kernel(*proj_in, *proj_out)
    _scan_kernel(*proj_out, c0_ref, m0_ref, *scan_refs, rev=True,
                 has_add=False, readout=False)


def _proj_scan(x2d, mod_l, mod_row_fn, g, wts, state, *, batch, ts, casts=()):
    n, d = x2d.shape
    n_t = n // batch // ts
    n_chunks = ts // SCAN_CHUNK
    nq = N_HEADS * DK
    nv = N_HEADS * DV
    nqv = nq + nv

    def blk(b, t):
        return b * n_t + (n_t - 1 - t)

    rows = lambda w: pl.BlockSpec((ts, w), lambda b, t: (blk(b, t), 0))
    cols = lambda h: pl.BlockSpec((h, ts), lambda b, t: (0, blk(b, t)))
    st_shapes = [(N_HEADS, DK, DV + DK), (N_HEADS, 1, DK)]
    st_specs = [pl.BlockSpec((None,) + s, lambda b, t: (b, 0, 0, 0))
                for s in st_shapes]
    steps = batch * n_t
    cast_in_specs, cast_args, cast_out_specs, cast_shapes = [], [], [], []
    for w, layer in casts:
        slab = (w.shape[1] // steps, w.shape[2])
        cast_in_specs.append(pl.BlockSpec(
            (None,) + slab,
            lambda b, t, first=layer * steps: (first + b * n_t + t, 0, 0)))
        cast_args.append(w.reshape((w.shape[0] * steps,) + slab))
        cast_out_specs.append(pl.BlockSpec((None,) + slab,
                                           lambda b, t: (b * n_t + t, 0, 0)))
        cast_shapes.append(jax.ShapeDtypeStruct((steps,) + slab, BF16))
    outs = pl.pallas_call(
        functools.partial(_proj_scan_kernel, n_casts=len(casts)),
        grid=(batch, n_t),
        in_specs=[
            rows(d),
            pl.BlockSpec((None, 6, d), lambda b, t: (mod_row_fn(b), 0, 0)),
            _const_spec((1, d)),
            _const_spec((d, nqv)),
            _const_spec((nq, d)),
            _const_spec((d, GATE_PAD)),
            _const_spec((N_GATE, d)),
            _const_spec((1, GATE_PAD)),
            _const_spec((N_GATE, 1)),
        ] + st_specs + cast_in_specs,
        out_specs=[rows(nqv), cols(nq), rows(GATE_PAD), cols(N_GATE), rows(nv)]
        + st_specs + cast_out_specs,
        out_shape=[
            jax.ShapeDtypeStruct((n, nqv), BF16),
            jax.ShapeDtypeStruct((nq, n), BF16),
            jax.ShapeDtypeStruct((n, GATE_PAD), F32),
            jax.ShapeDtypeStruct((N_GATE, n), F32),
            jax.ShapeDtypeStruct((n, nv), F32),
        ] + [jax.ShapeDtypeStruct((batch,) + s, F32) for s in st_shapes]
        + cast_shapes,
        scratch_shapes=_scan_scratch(n_chunks),
        compiler_params=_params(2),
        name="mlstm_proj_scan_bwd",
    )(x2d, mod_l, g, wts["w_qv"], wts["w_kt"], wts["w_g"], wts["w_gt"],
      wts["b_g"], wts["b_gt"], *state, *cast_args)
    qv, kt, gc, gr, h, cf, mf = outs[:7]
    if not casts:
        return (qv, kt, gc, gr), h, (cf, mf)
    cast = [o.reshape(w.shape[1:]) for o, (w, _) in zip(outs[7:], casts)]
    return (qv, kt, gc, gr), h, (cf, mf), cast


def _mlstm_weights(w_in, b_gate):
    nq = N_HEADS * DK
    o2 = 2 * nq
    o3 = o2 + N_HEADS * DV
    o4 = o3 + D_MODEL
    w_g = w_in[:, o4:]
    return {
        "w_qv": jnp.concatenate([w_in[:, :nq], w_in[:, o2:o3]], axis=1).astype(BF16),
        "w_kt": w_in[:, nq:o2].T.astype(BF16),
        "w_o": w_in[:, o3:o4].astype(BF16),
        "w_g": jnp.pad(w_g, ((0, 0), (0, GATE_PAD - N_GATE))).astype(BF16),
        "w_gt": w_g.T.astype(BF16),
        "b_g": jnp.pad(b_gate, (0, GATE_PAD - N_GATE)).reshape(1, GATE_PAD),
        "b_gt": b_gate.reshape(N_GATE, 1),
    }


def kernel(x, c, ctx, c_ctx, w_mod, b_mod, g_mix, g_ffn, a_w_in, a_b_gate,
           a_head_gain, a_w_out, b_w_in, b_w_conv, b_w_out, f_w_up, f_w_conv,
           f_b_conv, f_w_down, g_final):
    bn, t, d = x.shape
    n_ctx = ctx.shape[1]
    depth = w_mod.shape[0]
    n_mixers = 2
    assert d == D_MODEL and bn <= 7 and t % TOKEN_TILE == 0
    assert n_ctx % SCAN_CHUNK == 0 and TOKEN_TILE % SCAN_CHUNK == 0

    rec_layers = [i for i in range(depth) if i % n_mixers == 0]
    last_rec = max(rec_layers) if rec_layers else -1

    cc = jnp.zeros((8, d), F32).at[:bn].set(c).at[bn].set(c_ctx)
    mod = _mod_all(cc, w_mod, b_mod).reshape(depth, 8, 6, d)

    tm = TOKEN_TILE
    tiles_per_batch = t // tm
    lat_row = lambda i: i // tiles_per_batch
    ctx_row = lambda i: bn

    xs = x.reshape(bn * t, d)
    cs = ctx.reshape(bn * n_ctx, d)
    zero_state = (jnp.zeros((bn, N_HEADS, DK, DV + DK), F32),
                  jnp.zeros((bn, N_HEADS, 1, DK), F32))

    a_w_out_b = a_w_out.astype(BF16)
    ffn_w = conv_w = None

    for i in range(depth):
        j = i // n_mixers
        ctx_read = i <= last_rec
        ctx_live = i < last_rec
        gm = g_mix[i].reshape(1, d)
        gf = g_ffn[i].reshape(1, d)
        if i % n_mixers == 0:
            wts = _mlstm_weights(a_w_in[j], a_b_gate[j])
            w_out = (a_w_out_b, j)
            gain = a_head_gain[j].reshape(1, N_HEADS * DV)
            s_f = s_b = zero_state
            if ctx_read:
                pc, hc, s_b = _proj_scan(cs, mod[i], lambda b: bn, gm, wts,
                                         zero_state, batch=bn, ts=n_ctx)
                ro = ((cs, mod[i], lambda b: bn, gm, wts["w_o"], gain, w_out)
                      if ctx_live else None)
                cs_new, s_f = _scan(*pc, zero_state, batch=bn, ts=n_ctx,
                                    rev=False, add=hc, readout=ro)
            if i == 0:
                px, hl, _, ffn_w = _proj_scan(
                    xs, mod[i], lambda b: b, gm, wts, s_b, batch=bn, ts=tm,
                    casts=[(f_w_up, 0), (f_w_down, 0)])
            else:
                px, hl, _ = _proj_scan(xs, mod[i], lambda b: b, gm, wts, s_b,
                                       batch=bn, ts=tm)
            ro = (xs, mod[i], lambda b: b, gm, wts["w_o"], gain, w_out)
            xs, _ = _scan(*px, s_f, batch=bn, ts=tm, rev=False, add=hl,
                          readout=ro)
            if ctx_live:
                cs = cs_new
        else:
            if conv_w is None:
                conv_w = (b_w_in[j].astype(BF16), b_w_out[j].astype(BF16))
            w_in, w_out = conv_w
            conv_w = None
            xs = _shortconv(xs, mod[i], gm, w_in, b_w_conv[j], w_out,
                            tm=tm, seg=GRID_W, row_fn=lat_row)
            if ctx_live:
                cs = _shortconv(cs, mod[i], gm, w_in, b_w_conv[j], w_out,
                                tm=n_ctx, seg=n_ctx, row_fn=ctx_row)
        if ffn_w is None:
            ffn_w = (f_w_up[i].astype(BF16), f_w_down[i].astype(BF16))
        w_up, w_down = ffn_w
        ffn_w = None
        b_conv = f_b_conv[i].reshape(1, D_FF)
        gfin = g_final.reshape(1, d)
        if i + 1 < depth:
            casts = [(f_w_up, i + 1), (f_w_down, i + 1)]
            if (i + 1) % n_mixers == 1:
                casts += [(b_w_in, (i + 1) // n_mixers),
                          (b_w_out, (i + 1) // n_mixers)]
            xs, cast = _ffn(xs, mod[i], gf, w_up, f_w_conv[i], b_conv, w_down,
                            gfin, batch=bn, strip=FFN_STRIP, final=False,
                            casts=casts)
            ffn_w, conv_w = cast[:2], (cast[2:] or None)
        else:
            xs = _ffn(xs, mod[i], gf, w_up, f_w_conv[i], b_conv, w_down, gfin,
                      batch=bn, strip=FFN_STRIP, final=True)
        if ctx_live:
            cs = _ffn(cs, mod[i], gf, w_up, f_w_conv[i], b_conv, w_down, gfin,
                      batch=bn, strip=0, final=False)

    return xs.reshape(bn, t, d)
```

```python
import functools

import jax
import jax.numpy as jnp
from jax import lax
from jax.experimental import pallas as pl
from jax.experimental.pallas import tpu as pltpu

F32 = jnp.float32
BF16 = jnp.bfloat16

D_MODEL = 1024
GRID_W = 64
N_HEADS = 4
DK = 128
DV = 256
D_FF = 2816
EPS = 1e-6
N_GATE = 16
GATE_PAD = 128

SCAN_CHUNK = 128
FF_CHUNK = 256
DOWN_PIECES = 2
ROW_PIECES = 2
TOKEN_TILE = 1024
FFN_STRIP = 8
VMEM_LIMIT = 56 * 1024 * 1024


def _params(n_axes):
    return pltpu.CompilerParams(
        dimension_semantics=("arbitrary",) * n_axes,
        vmem_limit_bytes=VMEM_LIMIT)


def _const_spec(shape):
    zeros = (0,) * len(shape)
    return pl.BlockSpec(shape, lambda *_: zeros, pipeline_mode=pl.Buffered(1))


def _weight(w):
    if isinstance(w, tuple):
        arr, layer = w
        idx = (layer,) + (0,) * (arr.ndim - 1)
        spec = pl.BlockSpec((None,) + arr.shape[1:], lambda *_: idx,
                            pipeline_mode=pl.Buffered(1))
        return spec, arr
    return _const_spec(w.shape), w


def _dot(a, b):
    return jnp.dot(a, b, preferred_element_type=F32)


def _sigmoid(v):
    return 1.0 / (1.0 + jnp.exp(-v))


def _rmsnorm(x, g):
    ms = jnp.mean(x * x, axis=-1, keepdims=True)
    return (x * lax.rsqrt(ms + EPS)) * g


def _modulate(x, g, shift, scale):
    return _rmsnorm(x, g) * (1.0 + scale) + shift


def _shift_conv(u, w_ref, seg, shift=1):
    n = u.shape[0]
    if seg == n and shift % 8 == 0:
        pad = jnp.zeros((shift, u.shape[1]), u.dtype)
        prev = jnp.concatenate([pad, u[:n - shift]], axis=0)
        nxt = jnp.concatenate([u[shift:], pad], axis=0)
    else:
        pos = lax.broadcasted_iota(jnp.int32, (n, 1), 0) & (seg - 1)
        prev = jnp.where(pos >= shift, pltpu.roll(u, shift, axis=0), 0.0)
        nxt = jnp.where(pos < seg - shift, pltpu.roll(u, n - shift, axis=0), 0.0)
    return prev * w_ref[0:1, :] + u * w_ref[1:2, :] + nxt * w_ref[2:3, :]


def _mod_kernel(cc_ref, w_ref, b_ref, o_ref):
    cc = cc_ref[...]
    s = cc * _sigmoid(cc)
    o_ref[...] = _dot(s, w_ref[...]) + b_ref[...]


MOD_SLAB = 768


def _mod_all(cc, w_mod, b_mod, layers):
    _, d, n = w_mod.shape
    depth = layers
    tn = 1536
    return pl.pallas_call(
        _mod_kernel,
        grid=(depth, n // tn),
        in_specs=[
            pl.BlockSpec((8, d), lambda l, j: (0, 0)),
            pl.BlockSpec((None, d, tn), lambda l, j: (l, 0, j)),
            pl.BlockSpec((None, 1, tn), lambda l, j: (l, 0, j)),
        ],
        out_specs=pl.BlockSpec((None, 8, tn), lambda l, j: (l, 0, j)),
        out_shape=jax.ShapeDtypeStruct((depth, 8, n), F32),
        compiler_params=_params(2),
        name="adaln_mod",
    )(cc, w_mod, b_mod.reshape(-1, 1, n))


def _shortconv_kernel(x_ref, mod_ref, g_ref, win_ref, wconv_ref, wout_ref,
                      o_ref, z_ref, *, seg):
    d = D_MODEL
    cw = 256
    tm = x_ref.shape[0]
    pieces = ROW_PIECES if (tm // ROW_PIECES) % seg == 0 else 1
    pr = tm // pieces
    for p in range(pieces):
        rows = slice(p * pr, (p + 1) * pr)
        x = x_ref[rows, :]
        hx = _modulate(x, g_ref[...], mod_ref[0:1, :], mod_ref[1:2, :]).astype(BF16)
        for j in range(d // cw):
            lo, hi = j * cw, (j + 1) * cw
            bg = _dot(hx, win_ref[:, lo:hi])
            cg = _dot(hx, win_ref[:, d + lo:d + hi])
            xv = _dot(hx, win_ref[:, 2 * d + lo:2 * d + hi])
            cv = _shift_conv(cg * xv, wconv_ref.at[:, lo:hi], seg)
            z_ref[rows, lo:hi] = (bg * cv).astype(BF16)
        y = _dot(z_ref[rows, :], wout_ref[...])
        o_ref[rows, :] = x + mod_ref[2:3, :] * y


def _shortconv(x2d, mod_l, g, w_in, w_conv, w_out, *, tm, seg, row_fn):
    n, d = x2d.shape
    win_spec, w_in = _weight(w_in)
    wout_spec, w_out = _weight(w_out)
    return pl.pallas_call(
        functools.partial(_shortconv_kernel, seg=seg),
        grid=(n // tm,),
        in_specs=[
            pl.BlockSpec((tm, d), lambda i: (i, 0)),
            pl.BlockSpec((None, 6, d), lambda i: (row_fn(i), 0, 0)),
            _const_spec((1, d)),
            win_spec,
            _const_spec((3, d)),
            wout_spec,
        ],
        out_specs=pl.BlockSpec((tm, d), lambda i: (i, 0)),
        out_shape=jax.ShapeDtypeStruct((n, d), F32),
        scratch_shapes=[pltpu.VMEM((tm, d), BF16)],
        compiler_params=_params(1),
        name="shortconv_mixer",
    )(x2d, mod_l, g, w_in, w_conv, w_out)


def _ffn_kernel(*refs, shift, seg, final, n_casts, with_mod):
    (x_ref, mod_ref, g_ref, wup_ref, wconv_ref, bconv_ref, wdown_ref,
     gfin_ref) = refs[:8]
    n_side = n_casts + (3 if with_mod else 0)
    o_ref = refs[8 + n_side]
    act_ref = refs[-1]
    for src, dst in zip(refs[8:8 + n_casts], refs[9 + n_side:9 + n_side + n_casts]):
        dst[...] = src[...].astype(BF16)
    if with_mod:
        _mod_kernel(*refs[8 + n_casts:8 + n_side], refs[-2])
    d = x_ref.shape[-1]
    x = x_ref[...].reshape(-1, d)
    hx = _modulate(x, g_ref[...], mod_ref[3:4, :], mod_ref[4:5, :]).astype(BF16)
    for c in range(D_FF // FF_CHUNK):
        lo, hi = c * FF_CHUNK, (c + 1) * FF_CHUNK
        val = _dot(hx, wup_ref[:, D_FF + lo:D_FF + hi])
        gate = _shift_conv(_dot(hx, wup_ref[:, lo:hi]), wconv_ref.at[:, lo:hi],
                           seg, shift)
        gate = gate + bconv_ref[:, lo:hi]
        act_ref[:, lo:hi] = (gate * _sigmoid(gate) * val).astype(BF16)
    tm = x.shape[0]
    pr = tm // DOWN_PIECES
    lead = o_ref.shape[0] // DOWN_PIECES
    for p in range(DOWN_PIECES):
        rows = slice(p * pr, (p + 1) * pr)
        y = _dot(act_ref[rows, :], wdown_ref[...])
        out = x[rows] + mod_ref[5:6, :] * y
        if final:
            out = _rmsnorm(out, gfin_ref[...])
        o_ref[p * lead:(p + 1) * lead] = out.reshape((lead,) + o_ref.shape[1:])


def _ffn(x2d, mod_l, g, w_up, w_conv, b_conv, w_down, g_final, *, batch,
         strip, final, casts=(), mod_job=None):
    n, d = x2d.shape
    wup_spec, w_up = _weight(w_up)
    wdown_spec, w_down = _weight(w_down)
    per = n // batch
    if strip:
        rows = per // GRID_W
        strips = GRID_W // strip
        tm = rows * strip
        xin = x2d.reshape(batch * rows, GRID_W, d)
        x_spec = pl.BlockSpec((rows, strip, d),
                              lambda i: (i // strips, i % strips, 0))
        grid = (batch * strips,)
        row_fn = lambda i: i // strips
        shift = strip
    else:
        tm = per
        xin = x2d
        x_spec = pl.BlockSpec((tm, d), lambda i: (i, 0))
        grid = (batch,)
        row_fn = lambda i: batch
        shift = 1
    in_specs = [
        x_spec,
        pl.BlockSpec((None, 6, d), lambda i: (row_fn(i), 0, 0)),
        _const_spec((1, d)),
        wup_spec,
        _const_spec((3, D_FF)),
        _const_spec((1, D_FF)),
        wdown_spec,
        _const_spec((1, d)),
    ]
    args = [xin, mod_l, g, w_up, w_conv, b_conv, w_down, g_final]
    out_specs = [x_spec]
    out_shape = [jax.ShapeDtypeStruct(xin.shape, F32)]
    steps = grid[0]
    for w, layer in casts:
        slab = (w.shape[1] // steps, w.shape[2])
        in_specs.append(pl.BlockSpec(
            (None,) + slab, lambda i, first=layer * steps: (first + i, 0, 0)))
        args.append(w.reshape((w.shape[0] * steps,) + slab))
        out_specs.append(pl.BlockSpec((None,) + slab, lambda i: (i, 0, 0)))
        out_shape.append(jax.ShapeDtypeStruct((steps,) + slab, BF16))
    if mod_job is not None:
        cc, w_mod, b_mod, first_layer = mod_job
        depth, _, n_mod = w_mod.shape
        per_layer = n_mod // MOD_SLAB
        n_slabs = (depth - first_layer) * per_layer
        assert n_slabs <= steps
        slab_of = lambda i: jnp.minimum(i, n_slabs - 1)
        in_specs += [
            _const_spec(cc.shape),
            pl.BlockSpec((None, d, MOD_SLAB), lambda i: (
                first_layer + slab_of(i) // per_layer, 0, slab_of(i) % per_layer)),
            pl.BlockSpec((None, 1, MOD_SLAB), lambda i: (
                first_layer + slab_of(i) // per_layer, 0, slab_of(i) % per_layer)),
        ]
        args += [cc, w_mod, b_mod.reshape(depth, 1, n_mod)]
        out_specs.append(pl.BlockSpec((None, cc.shape[0], MOD_SLAB), lambda i: (
            slab_of(i) // per_layer, 0, slab_of(i) % per_layer)))
        out_shape.append(jax.ShapeDtypeStruct(
            (depth - first_layer, cc.shape[0], n_mod), F32))
    outs = pl.pallas_call(
        functools.partial(_ffn_kernel, shift=shift, seg=tm, final=final,
                          n_casts=len(casts), with_mod=mod_job is not None),
        grid=grid,
        in_specs=in_specs,
        out_specs=out_specs,
        out_shape=out_shape,
        scratch_shapes=[pltpu.VMEM((tm, D_FF), BF16)],
        compiler_params=_params(1),
        name="convglu_ffn",
    )(*args)
    out = outs[0].reshape(n, d)
    if not casts and mod_job is None:
        return out
    cast = [o.reshape(w.shape[1:]) for o, (w, _) in zip(outs[1:], casts)]
    if mod_job is None:
        return out, cast
    return out, cast, outs[-1]


def _log_sigmoid(v):
    return -(jnp.maximum(-v, 0.0) + jnp.log(1.0 + jnp.exp(-jnp.abs(v))))


def _chunk_scans(v, axis, chunk, op, identity):
    n = v.shape[axis]
    shape = (n, 1) if axis == 0 else (1, n)
    pos = lax.broadcasted_iota(jnp.int32, shape, axis) & (chunk - 1)
    pre, suf = v, v
    step = 1
    while step < chunk:
        pre = op(pre, jnp.where(pos >= step,
                                pltpu.roll(pre, step, axis=axis), identity))
        suf = op(suf, jnp.where(pos < chunk - step,
                                pltpu.roll(suf, n - step, axis=axis), identity))
        step *= 2
    return pre, suf


def _mlstm_proj_kernel(x_ref, mod_ref, g_ref, wqv_ref, wkt_ref, wg_ref,
                       wgt_ref, bg_ref, bgt_ref,
                       qv_ref, kt_ref, gc_ref, gr_ref):
    nt = (((1,), (1,)), ((), ()))
    half = N_GATE // 2
    lane = lax.broadcasted_iota(jnp.int32, (1, GATE_PAD), 1)
    fwd = (lane & (half - 1)) < N_HEADS
    sub = lax.broadcasted_iota(jnp.int32, (N_GATE, 1), 0)
    tm = x_ref.shape[0]
    pieces = ROW_PIECES if (tm // ROW_PIECES) % SCAN_CHUNK == 0 else 1
    pr = tm // pieces
    for p in range(pieces):
        rows = slice(p * pr, (p + 1) * pr)
        hx = _modulate(x_ref[rows, :], g_ref[...], mod_ref[0:1, :],
                       mod_ref[1:2, :]).astype(BF16)

        gc = _dot(hx, wg_ref[...]) + bg_ref[...]
        pre, suf = _chunk_scans(_log_sigmoid(gc), 0, SCAN_CHUNK, jnp.add, 0.0)
        b = jnp.where(fwd, pre, suf)
        u = gc - pltpu.roll(b, GATE_PAD - half, axis=1)
        pre, suf = _chunk_scans(u, 0, SCAN_CHUNK, jnp.maximum, -jnp.inf)
        gc_ref[rows, :] = jnp.where(lane < half, jnp.where(fwd, pre, suf), b)

        gr = lax.dot_general(wgt_ref[...], hx, nt, preferred_element_type=F32)
        gr = gr + bgt_ref[...]
        pre, suf = _chunk_scans(_log_sigmoid(gr), 1, SCAN_CHUNK, jnp.add, 0.0)
        b = jnp.where((sub & (half - 1)) < N_HEADS, pre, suf)
        gr_ref[0:half, rows] = gr[0:half] - b[half:]
        gr_ref[half:, rows] = b[half:]

        qv_ref[rows, :] = _dot(hx, wqv_ref[...]).astype(BF16)
        kt = lax.dot_general(wkt_ref[...], hx, nt, preferred_element_type=F32)
        kt_ref[:, rows] = (kt * (DK ** -0.5)).astype(BF16)


def _scan_kernel(*refs, rev, has_add, readout):
    it = iter(refs)
    qv_ref, kt_ref, gc_ref, gr_ref, c0_ref, m0_ref = (next(it) for _ in range(6))
    hadd_ref = next(it) if has_add else None
    if readout:
        x_ref, mod_ref, g_ref, wo_ref, gain_ref, wout_ref = (
            next(it) for _ in range(6))
    out_ref, cf_ref, mf_ref = (next(it) for _ in range(3))
    c_s, m_s, a_s, x_s, s_s, w_s, ktw_s = (next(it) for _ in range(7))
    if readout:
        h_s, z_s = next(it), next(it)
    h_ref = h_s if readout else out_ref
    assert not (rev and readout)
    t = pl.program_id(1)
    n_t = pl.num_programs(1)
    L = SCAN_CHUNK
    nq = N_HEADS * DK
    n_chunks = qv_ref.shape[0] // L
    half = N_GATE // 2

    @pl.when(t == 0)
    def _():
        c_s[...] = c0_ref[...]
        m_s[...] = m0_ref[...]

    ti = lax.broadcasted_iota(jnp.int32, (L, L), 0)
    si = lax.broadcasted_iota(jnp.int32, (L, L), 1)
    mask = (si >= ti) if rev else (si <= ti)
    gcol = N_HEADS if rev else 0
    ones = jnp.ones((L, DK), BF16)
    order = range(n_chunks - 1, -1, -1) if rev else range(n_chunks)

    units = [(c, h) for c in order for h in range(N_HEADS)]

    m_in, mx_of, decay_of = {}, {}, {}
    for h in range(N_HEADS):
        gi = gcol + h
        m_prev = m_s[h]
        for c in order:
            r0, r1 = c * L, (c + 1) * L
            u_row = gr_ref[gi:gi + 1, r0:r1]
            b_row = gr_ref[half + gi:half + gi + 1, r0:r1]
            g = b_row[:, 0:1] if rev else b_row[:, L - 1:L]
            mx = jnp.maximum(m_prev, jnp.max(u_row, axis=1, keepdims=True))
            m_in[c, h], mx_of[c, h] = m_prev, mx
            decay_of[c, h] = jnp.exp(m_prev - mx)
            m_prev = g + mx
        m_s[h] = m_prev

    for c, h in units:
        r0, r1 = c * L, (c + 1) * L
        q = qv_ref[r0:r1, h * DK:(h + 1) * DK]
        kt = kt_ref[h * DK:(h + 1) * DK, r0:r1]
        s_s[c, h] = _dot(q, kt)
        u_row = gr_ref[gcol + h:gcol + h + 1, r0:r1]
        ktw_s[c, h] = (kt.astype(F32) * jnp.exp(u_row - mx_of[c, h])).astype(BF16)
    o_piece = 2 if n_chunks % 2 == 0 else 1
    for i, (c, h) in enumerate(units):
        r0, r1 = c * L, (c + 1) * L
        gi = gcol + h
        u_row = gr_ref[gi:gi + 1, r0:r1]
        cm_bc = jnp.broadcast_to(gc_ref[r0:r1, gi:gi + 1], (L, DK))
        r = jnp.maximum(m_in[c, h], cm_bc)
        w = jnp.exp(jnp.where(mask, u_row - r, -jnp.inf)) * s_s[c, h]
        w_s[c, h] = w.astype(BF16)
        if readout and i % (o_piece * N_HEADS) == 0:
            p0 = (i // (o_piece * N_HEADS)) * o_piece * L
            rows = slice(p0, p0 + o_piece * L)
            hx = _modulate(x_ref[rows, :], g_ref[...], mod_ref[0:1, :],
                           mod_ref[1:2, :]).astype(BF16)
            out_ref[rows, :] = _dot(hx, wo_ref[...])

    for c in order:
        r0, r1 = c * L, (c + 1) * L
        for h in range(N_HEADS):
            v = qv_ref[r0:r1, nq + h * DV:nq + (h + 1) * DV]
            v_aug = jnp.concatenate([v, ones], axis=1)
            a_s[h] = _dot(w_s[c, h], v_aug)
            x_s[h] = _dot(ktw_s[c, h], v_aug)
        for h in range(N_HEADS):
            gi = gcol + h
            q = qv_ref[r0:r1, h * DK:(h + 1) * DK]
            m_prev = m_in[c, h]
            c_prev = c_s[h]
            a_aug = a_s[h]
            b_aug = _dot(q, c_prev.astype(BF16))
            cm_bc = jnp.broadcast_to(gc_ref[r0:r1, gi:gi + 1], (L, DK))
            b_bc = jnp.broadcast_to(gc_ref[r0:r1, half + gi:half + gi + 1],
                                    (L, DK))
            r = jnp.maximum(m_prev, cm_bc)
            wi = jnp.exp(m_prev - r)
            den = a_aug[:, DV:] + wi * b_aug[:, DV:]
            inv = 1.0 / jnp.maximum(jnp.abs(den), jnp.exp(-(b_bc + r)))
            beta = wi * inv
            for j in range(DV // DK):
                lo, hi = j * DK, (j + 1) * DK
                cols = slice(h * DV + lo, h * DV + hi)
                hh = inv * a_aug[:, lo:hi] + beta * b_aug[:, lo:hi]
                if has_add:
                    hh = hh + hadd_ref[r0:r1, cols]
                h_ref[r0:r1, cols] = hh
            for j in range((DV + DK) // DK):
                lo, hi = j * DK, (j + 1) * DK
                c_s[h, :, lo:hi] = (decay_of[c, h] * c_prev[:, lo:hi]
                                    + x_s[h, :, lo:hi])

    @pl.when(t == n_t - 1)
    def _():
        cf_ref[...] = c_s[...]
        mf_ref[...] = m_s[...]

    if readout:
        pr = o_piece * L
        for p in range(n_chunks // o_piece):
            rows = slice(p * pr, (p + 1) * pr)
            for h in range(N_HEADS):
                lo, hi = h * DV, (h + 1) * DV
                hh = h_s[rows, lo:hi]
                ms = jnp.mean(hh * hh, axis=-1, keepdims=True)
                hn = hh * lax.rsqrt(ms + EPS) * gain_ref[:, lo:hi]
                z_s[rows, lo:hi] = (hn * _sigmoid(out_ref[rows, lo:hi])
                                    ).astype(BF16)
            y = _dot(z_s[rows, :], wout_ref[...])
            out_ref[rows, :] = x_ref[rows, :] + mod_ref[2:3, :] * y


def _scan_scratch(n_chunks):
    return [
        pltpu.VMEM((N_HEADS, DK, DV + DK), F32),
        pltpu.VMEM((N_HEADS, 1, DK), F32),
        pltpu.VMEM((N_HEADS, SCAN_CHUNK, DV + DK), F32),
        pltpu.VMEM((N_HEADS, DK, DV + DK), F32),
        pltpu.VMEM((n_chunks, N_HEADS, SCAN_CHUNK, SCAN_CHUNK), F32),
        pltpu.VMEM((n_chunks, N_HEADS, SCAN_CHUNK, SCAN_CHUNK), BF16),
        pltpu.VMEM((n_chunks, N_HEADS, DK, SCAN_CHUNK), BF16),
    ]


def _scan(qv, kt, gc, gr, state, *, batch, ts, rev, add=None, readout=None):
    n = qv.shape[0]
    n_t = n // batch // ts
    n_chunks = ts // SCAN_CHUNK
    nq = N_HEADS * DK
    nv = N_HEADS * DV
    nqv = nq + nv

    def blk(b, t):
        return b * n_t + ((n_t - 1 - t) if rev else t)

    st_shapes = [(N_HEADS, DK, DV + DK), (N_HEADS, 1, DK)]
    st_specs = [pl.BlockSpec((None,) + s, lambda b, t: (b, 0, 0, 0))
                for s in st_shapes]
    h_spec = pl.BlockSpec((ts, nv), lambda b, t: (blk(b, t), 0))
    has_add = add is not None
    in_specs = [
        pl.BlockSpec((ts, nqv), lambda b, t: (blk(b, t), 0)),
        pl.BlockSpec((nq, ts), lambda b, t: (0, blk(b, t))),
        pl.BlockSpec((ts, GATE_PAD), lambda b, t: (blk(b, t), 0)),
        pl.BlockSpec((N_GATE, ts), lambda b, t: (0, blk(b, t))),
    ] + st_specs
    args = [qv, kt, gc, gr, *state]
    scratch = _scan_scratch(n_chunks)
    if has_add:
        in_specs.append(h_spec)
        args.append(add)
    out_cols = nv
    if readout is not None:
        x2d, mod_l, mod_row_fn, g, w_o, gain, w_out = readout
        wout_spec, w_out = _weight(w_out)
        d = x2d.shape[1]
        out_cols = d
        in_specs += [
            pl.BlockSpec((ts, d), lambda b, t: (blk(b, t), 0)),
            pl.BlockSpec((None, 6, d), lambda b, t: (mod_row_fn(b), 0, 0)),
            _const_spec((1, d)),
            _const_spec((d, nv)),
            _const_spec((1, nv)),
            wout_spec,
        ]
        args += [x2d, mod_l, g, w_o, gain, w_out]
        scratch += [pltpu.VMEM((ts, nv), F32), pltpu.VMEM((ts, nv), BF16)]
    out, cf, mf = pl.pallas_call(
        functools.partial(_scan_kernel, rev=rev, has_add=has_add,
                          readout=readout is not None),
        grid=(batch, n_t),
        in_specs=in_specs,
        out_specs=[pl.BlockSpec((ts, out_cols), lambda b, t: (blk(b, t), 0))]
        + st_specs,
        out_shape=[jax.ShapeDtypeStruct((n, out_cols), F32)]
        + [jax.ShapeDtypeStruct((batch,) + s, F32) for s in st_shapes],
        scratch_shapes=scratch,
        compiler_params=_params(2),
        name="mlstm_scan_bwd" if rev else "mlstm_scan_fwd",
    )(*args)
    return out, (cf, mf)


N_PROJ_IN = 9
N_PROJ_OUT = 4


def _proj_scan_kernel(*refs, n_casts):
    proj_in = refs[:N_PROJ_IN]
    c0_ref, m0_ref = refs[N_PROJ_IN:N_PROJ_IN + 2]
    cast_in = refs[N_PROJ_IN + 2:N_PROJ_IN + 2 + n_casts]
    outs = refs[N_PROJ_IN + 2 + n_casts:]
    proj_out = outs[:N_PROJ_OUT]
    n_scan_out = 3
    cast_out = outs[N_PROJ_OUT + n_scan_out:N_PROJ_OUT + n_scan_out + n_casts]
    scan_refs = (outs[N_PROJ_OUT:N_PROJ_OUT + n_scan_out]
                 + outs[N_PROJ_OUT + n_scan_out + n_casts:])
    for src, dst in zip(cast_in, cast_out):
        dst[...] = src[...].astype(BF16)
    _mlstm_proj_kernel(*proj_in, *proj_out)
    _scan_kernel(*proj_out, c0_ref, m0_ref, *scan_refs, rev=True,
                 has_add=False, readout=False)


def _proj_scan(x2d, mod_l, mod_row_fn, g, wts, state, *, batch, ts, casts=()):
    n, d = x2d.shape
    n_t = n // batch // ts
    n_chunks = ts // SCAN_CHUNK
    nq = N_HEADS * DK
    nv = N_HEADS * DV
    nqv = nq + nv

    def blk(b, t):
        return b * n_t + (n_t - 1 - t)

    rows = lambda w: pl.BlockSpec((ts, w), lambda b, t: (blk(b, t), 0))
    cols = lambda h: pl.BlockSpec((h, ts), lambda b, t: (0, blk(b, t)))
    st_shapes = [(N_HEADS, DK, DV + DK), (N_HEADS, 1, DK)]
    st_specs = [pl.BlockSpec((None,) + s, lambda b, t: (b, 0, 0, 0))
                for s in st_shapes]
    steps = batch * n_t
    cast_in_specs, cast_args, cast_out_specs, cast_shapes = [], [], [], []
    for w, layer in casts:
        slab = (w.shape[1] // steps, w.shape[2])
        cast_in_specs.append(pl.BlockSpec(
            (None,) + slab,
            lambda b, t, first=layer * steps: (first + b * n_t + t, 0, 0)))
        cast_args.append(w.reshape((w.shape[0] * steps,) + slab))
        cast_out_specs.append(pl.BlockSpec((None,) + slab,
                                           lambda b, t: (b * n_t + t, 0, 0)))
        cast_shapes.append(jax.ShapeDtypeStruct((steps,) + slab, BF16))
    outs = pl.pallas_call(
        functools.partial(_proj_scan_kernel, n_casts=len(casts)),
        grid=(batch, n_t),
        in_specs=[
            rows(d),
            pl.BlockSpec((None, 6, d), lambda b, t: (mod_row_fn(b), 0, 0)),
            _const_spec((1, d)),
            _const_spec((d, nqv)),
            _const_spec((nq, d)),
            _const_spec((d, GATE_PAD)),
            _const_spec((N_GATE, d)),
            _const_spec((1, GATE_PAD)),
            _const_spec((N_GATE, 1)),
        ] + st_specs + cast_in_specs,
        out_specs=[rows(nqv), cols(nq), rows(GATE_PAD), cols(N_GATE), rows(nv)]
        + st_specs + cast_out_specs,
        out_shape=[
            jax.ShapeDtypeStruct((n, nqv), BF16),
            jax.ShapeDtypeStruct((nq, n), BF16),
            jax.ShapeDtypeStruct((n, GATE_PAD), F32),
            jax.ShapeDtypeStruct((N_GATE, n), F32),
            jax.ShapeDtypeStruct((n, nv), F32),
        ] + [jax.ShapeDtypeStruct((batch,) + s, F32) for s in st_shapes]
        + cast_shapes,
        scratch_shapes=_scan_scratch(n_chunks),
        compiler_params=_params(2),
        name="mlstm_proj_scan_bwd",
    )(x2d, mod_l, g, wts["w_qv"], wts["w_kt"], wts["w_g"], wts["w_gt"],
      wts["b_g"], wts["b_gt"], *state, *cast_args)
    qv, kt, gc, gr, h, cf, mf = outs[:7]
    if not casts:
        return (qv, kt, gc, gr), h, (cf, mf)
    cast = [o.reshape(w.shape[1:]) for o, (w, _) in zip(outs[7:], casts)]
    return (qv, kt, gc, gr), h, (cf, mf), cast


def _mlstm_weights(w_in, b_gate):
    nq = N_HEADS * DK
    o2 = 2 * nq
    o3 = o2 + N_HEADS * DV
    o4 = o3 + D_MODEL
    w_g = w_in[:, o4:]
    return {
        "w_qv": jnp.concatenate([w_in[:, :nq], w_in[:, o2:o3]], axis=1).astype(BF16),
        "w_kt": w_in[:, nq:o2].T.astype(BF16),
        "w_o": w_in[:, o3:o4].astype(BF16),
        "w_g": jnp.pad(w_g, ((0, 0), (0, GATE_PAD - N_GATE))).astype(BF16),
        "w_gt": w_g.T.astype(BF16),
        "b_g": jnp.pad(b_gate, (0, GATE_PAD - N_GATE)).reshape(1, GATE_PAD),
        "b_gt": b_gate.reshape(N_GATE, 1),
    }


def kernel(x, c, ctx, c_ctx, w_mod, b_mod, g_mix, g_ffn, a_w_in, a_b_gate,
           a_head_gain, a_w_out, b_w_in, b_w_conv, b_w_out, f_w_up, f_w_conv,
           f_b_conv, f_w_down, g_final):
    bn, t, d = x.shape
    n_ctx = ctx.shape[1]
    depth = w_mod.shape[0]
    n_mixers = 2
    assert d == D_MODEL and bn <= 7 and t % TOKEN_TILE == 0
    assert n_ctx % SCAN_CHUNK == 0 and TOKEN_TILE % SCAN_CHUNK == 0

    rec_layers = [i for i in range(depth) if i % n_mixers == 0]
    last_rec = max(rec_layers) if rec_layers else -1

    cc = jnp.zeros((8, d), F32).at[:bn].set(c).at[bn].set(c_ctx)
    ffn_steps = bn * (GRID_W // FFN_STRIP)
    hide_mod = 0 < (depth - 1) * (6 * d // MOD_SLAB) <= ffn_steps
    n_first = 1 if hide_mod else depth
    mod = list(_mod_all(cc, w_mod, b_mod, n_first).reshape(n_first, 8, 6, d))

    tm = TOKEN_TILE
    tiles_per_batch = t // tm
    lat_row = lambda i: i // tiles_per_batch
    ctx_row = lambda i: bn

    xs = x.reshape(bn * t, d)
    cs = ctx.reshape(bn * n_ctx, d)
    zero_state = (jnp.zeros((bn, N_HEADS, DK, DV + DK), F32),
                  jnp.zeros((bn, N_HEADS, 1, DK), F32))

    a_w_out_b = a_w_out.astype(BF16)
    ffn_w = conv_w = None

    for i in range(depth):
        j = i // n_mixers
        ctx_read = i <= last_rec
        ctx_live = i < last_rec
        gm = g_mix[i].reshape(1, d)
        gf = g_ffn[i].reshape(1, d)
        if i % n_mixers == 0:
            wts = _mlstm_weights(a_w_in[j], a_b_gate[j])
            w_out = (a_w_out_b, j)
            gain = a_head_gain[j].reshape(1, N_HEADS * DV)
            s_f = s_b = zero_state
            if ctx_read:
                pc, hc, s_b = _proj_scan(cs, mod[i], lambda b: bn, gm, wts,
                                         zero_state, batch=bn, ts=n_ctx)
                ro = ((cs, mod[i], lambda b: bn, gm, wts["w_o"], gain, w_out)
                      if ctx_live else None)
                cs_new, s_f = _scan(*pc, zero_state, batch=bn, ts=n_ctx,
                                    rev=False, add=hc, readout=ro)
            if i == 0:
                px, hl, _, ffn_w = _proj_scan(
                    xs, mod[i], lambda b: b, gm, wts, s_b, batch=bn, ts=tm,
                    casts=[(f_w_up, 0), (f_w_down, 0)])
            else:
                px, hl, _ = _proj_scan(xs, mod[i], lambda b: b, gm, wts, s_b,
                                       batch=bn, ts=tm)
            ro = (xs, mod[i], lambda b: b, gm, wts["w_o"], gain, w_out)
            xs, _ = _scan(*px, s_f, batch=bn, ts=tm, rev=False, add=hl,
                          readout=ro)
            if ctx_live:
                cs = cs_new
        else:
            if conv_w is None:
                conv_w = (b_w_in[j].astype(BF16), b_w_out[j].astype(BF16))
            w_in, w_out = conv_w
            conv_w = None
            xs = _shortconv(xs, mod[i], gm, w_in, b_w_conv[j], w_out,
                            tm=tm, seg=GRID_W, row_fn=lat_row)
            if ctx_live:
                cs = _shortconv(cs, mod[i], gm, w_in, b_w_conv[j], w_out,
                                tm=n_ctx, seg=n_ctx, row_fn=ctx_row)
        if ffn_w is None:
            ffn_w = (f_w_up[i].astype(BF16), f_w_down[i].astype(BF16))
        w_up, w_down = ffn_w
        ffn_w = None
        b_conv = f_b_conv[i].reshape(1, D_FF)
        gfin = g_final.reshape(1, d)
        if i + 1 < depth:
            casts = [(f_w_up, i + 1), (f_w_down, i + 1)]
            if (i + 1) % n_mixers == 1:
                casts += [(b_w_in, (i + 1) // n_mixers),
                          (b_w_out, (i + 1) // n_mixers)]
            mod_job = (cc, w_mod, b_mod, 1) if (i == 0 and hide_mod) else None
            res = _ffn(xs, mod[i], gf, w_up, f_w_conv[i], b_conv, w_down,
                       gfin, batch=bn, strip=FFN_STRIP, final=False,
                       casts=casts, mod_job=mod_job)
            xs, cast = res[:2]
            ffn_w, conv_w = cast[:2], (cast[2:] or None)
            if mod_job is not None:
                mod += list(res[2].reshape(depth - 1, 8, 6, d))
        else:
            xs = _ffn(xs, mod[i], gf, w_up, f_w_conv[i], b_conv, w_down, gfin,
                      batch=bn, strip=FFN_STRIP, final=True)
        if ctx_live:
            cs = _ffn(cs, mod[i], gf, w_up, f_w_conv[i], b_conv, w_down, gfin,
                      batch=bn, strip=0, final=False)

    return xs.reshape(bn, t, d)
```

```python
import functools

import jax
import jax.numpy as jnp
from jax import lax
from jax.experimental import pallas as pl
from jax.experimental.pallas import tpu as pltpu

F32 = jnp.float32
BF16 = jnp.bfloat16

D_MODEL = 1024
GRID_W = 64
N_HEADS = 4
DK = 128
DV = 256
D_FF = 2816
EPS = 1e-6
N_GATE = 16
GATE_PAD = 128

SCAN_CHUNK = 128
FF_CHUNK = 256
DOWN_PIECES = 2
ROW_PIECES = 2
TOKEN_TILE = 1024
FFN_STRIP = 8
VMEM_LIMIT = 56 * 1024 * 1024


def _params(n_axes):
    return pltpu.CompilerParams(
        dimension_semantics=("arbitrary",) * n_axes,
        vmem_limit_bytes=VMEM_LIMIT)


def _const_spec(shape):
    zeros = (0,) * len(shape)
    return pl.BlockSpec(shape, lambda *_: zeros, pipeline_mode=pl.Buffered(1))


def _weight(w):
    if isinstance(w, tuple):
        arr, layer = w
        idx = (layer,) + (0,) * (arr.ndim - 1)
        spec = pl.BlockSpec((None,) + arr.shape[1:], lambda *_: idx,
                            pipeline_mode=pl.Buffered(1))
        return spec, arr
    return _const_spec(w.shape), w


def _dot(a, b):
    return jnp.dot(a, b, preferred_element_type=F32)


def _sigmoid(v):
    return 1.0 / (1.0 + jnp.exp(-v))


def _rmsnorm(x, g):
    ms = jnp.mean(x * x, axis=-1, keepdims=True)
    return (x * lax.rsqrt(ms + EPS)) * g


def _modulate(x, g, shift, scale):
    return _rmsnorm(x, g) * (1.0 + scale) + shift


def _shift_conv(u, w_ref, seg, shift=1):
    n = u.shape[0]
    if seg == n and shift % 8 == 0:
        pad = jnp.zeros((shift, u.shape[1]), u.dtype)
        prev = jnp.concatenate([pad, u[:n - shift]], axis=0)
        nxt = jnp.concatenate([u[shift:], pad], axis=0)
    else:
        pos = lax.broadcasted_iota(jnp.int32, (n, 1), 0) & (seg - 1)
        prev = jnp.where(pos >= shift, pltpu.roll(u, shift, axis=0), 0.0)
        nxt = jnp.where(pos < seg - shift, pltpu.roll(u, n - shift, axis=0), 0.0)
    return prev * w_ref[0:1, :] + u * w_ref[1:2, :] + nxt * w_ref[2:3, :]


def _mod_kernel(cc_ref, w_ref, b_ref, o_ref):
    cc = cc_ref[...]
    s = cc * _sigmoid(cc)
    o_ref[...] = _dot(s, w_ref[...]) + b_ref[...]


def _mod_all(cc, w_mod, b_mod):
    depth, d, n = w_mod.shape
    tn = 1536
    return pl.pallas_call(
        _mod_kernel,
        grid=(depth, n // tn),
        in_specs=[
            pl.BlockSpec((8, d), lambda l, j: (0, 0)),
            pl.BlockSpec((None, d, tn), lambda l, j: (l, 0, j)),
            pl.BlockSpec((None, 1, tn), lambda l, j: (l, 0, j)),
        ],
        out_specs=pl.BlockSpec((None, 8, tn), lambda l, j: (l, 0, j)),
        out_shape=jax.ShapeDtypeStruct((depth, 8, n), F32),
        compiler_params=_params(2),
        name="adaln_mod",
    )(cc, w_mod, b_mod.reshape(depth, 1, n))


def _shortconv_kernel(x_ref, mod_ref, g_ref, win_ref, wconv_ref, wout_ref,
                      o_ref, z_ref, *, seg):
    d = D_MODEL
    cw = 256
    tm = x_ref.shape[0]
    pieces = ROW_PIECES if (tm // ROW_PIECES) % seg == 0 else 1
    pr = tm // pieces
    for p in range(pieces):
        rows = slice(p * pr, (p + 1) * pr)
        x = x_ref[rows, :]
        hx = _modulate(x, g_ref[...], mod_ref[0:1, :], mod_ref[1:2, :]).astype(BF16)
        for j in range(d // cw):
            lo, hi = j * cw, (j + 1) * cw
            bg = _dot(hx, win_ref[:, lo:hi])
            cg = _dot(hx, win_ref[:, d + lo:d + hi])
            xv = _dot(hx, win_ref[:, 2 * d + lo:2 * d + hi])
            cv = _shift_conv(cg * xv, wconv_ref.at[:, lo:hi], seg)
            z_ref[rows, lo:hi] = (bg * cv).astype(BF16)
        y = _dot(z_ref[rows, :], wout_ref[...])
        o_ref[rows, :] = x + mod_ref[2:3, :] * y


def _shortconv(x2d, mod_l, g, w_in, w_conv, w_out, *, tm, seg, row_fn):
    n, d = x2d.shape
    win_spec, w_in = _weight(w_in)
    wout_spec, w_out = _weight(w_out)
    return pl.pallas_call(
        functools.partial(_shortconv_kernel, seg=seg),
        grid=(n // tm,),
        in_specs=[
            pl.BlockSpec((tm, d), lambda i: (i, 0)),
            pl.BlockSpec((None, 6, d), lambda i: (row_fn(i), 0, 0)),
            _const_spec((1, d)),
            win_spec,
            _const_spec((3, d)),
            wout_spec,
        ],
        out_specs=pl.BlockSpec((tm, d), lambda i: (i, 0)),
        out_shape=jax.ShapeDtypeStruct((n, d), F32),
        scratch_shapes=[pltpu.VMEM((tm, d), BF16)],
        compiler_params=_params(1),
        name="shortconv_mixer",
    )(x2d, mod_l, g, w_in, w_conv, w_out)


def _ffn_kernel(*refs, shift, seg, final, n_casts):
    (x_ref, mod_ref, g_ref, wup_ref, wconv_ref, bconv_ref, wdown_ref,
     gfin_ref) = refs[:8]
    o_ref = refs[8 + n_casts]
    act_ref = refs[-1]
    for src, dst in zip(refs[8:8 + n_casts], refs[9 + n_casts:-1]):
        dst[...] = src[...].astype(BF16)
    d = x_ref.shape[-1]
    x = x_ref[...].reshape(-1, d)
    hx = _modulate(x, g_ref[...], mod_ref[3:4, :], mod_ref[4:5, :]).astype(BF16)
    for c in range(D_FF // FF_CHUNK):
        lo, hi = c * FF_CHUNK, (c + 1) * FF_CHUNK
        val = _dot(hx, wup_ref[:, D_FF + lo:D_FF + hi])
        gate = _shift_conv(_dot(hx, wup_ref[:, lo:hi]), wconv_ref.at[:, lo:hi],
                           seg, shift)
        gate = gate + bconv_ref[:, lo:hi]
        act_ref[:, lo:hi] = (gate * _sigmoid(gate) * val).astype(BF16)
    tm = x.shape[0]
    pr = tm // DOWN_PIECES
    lead = o_ref.shape[0] // DOWN_PIECES
    for p in range(DOWN_PIECES):
        rows = slice(p * pr, (p + 1) * pr)
        y = _dot(act_ref[rows, :], wdown_ref[...])
        out = x[rows] + mod_ref[5:6, :] * y
        if final:
            out = _rmsnorm(out, gfin_ref[...])
        o_ref[p * lead:(p + 1) * lead] = out.reshape((lead,) + o_ref.shape[1:])


def _ffn(x2d, mod_l, g, w_up, w_conv, b_conv, w_down, g_final, *, batch,
         strip, final, casts=()):
    n, d = x2d.shape
    wup_spec, w_up = _weight(w_up)
    wdown_spec, w_down = _weight(w_down)
    per = n // batch
    if strip:
        rows = per // GRID_W
        strips = GRID_W // strip
        tm = rows * strip
        xin = x2d.reshape(batch * rows, GRID_W, d)
        x_spec = pl.BlockSpec((rows, strip, d),
                              lambda i: (i // strips, i % strips, 0))
        grid = (batch * strips,)
        row_fn = lambda i: i // strips
        shift = strip
    else:
        tm = per
        xin = x2d
        x_spec = pl.BlockSpec((tm, d), lambda i: (i, 0))
        grid = (batch,)
        row_fn = lambda i: batch
        shift = 1
    in_specs = [
        x_spec,
        pl.BlockSpec((None, 6, d), lambda i: (row_fn(i), 0, 0)),
        _const_spec((1, d)),
        wup_spec,
        _const_spec((3, D_FF)),
        _const_spec((1, D_FF)),
        wdown_spec,
        _const_spec((1, d)),
    ]
    args = [xin, mod_l, g, w_up, w_conv, b_conv, w_down, g_final]
    out_specs = [x_spec]
    out_shape = [jax.ShapeDtypeStruct(xin.shape, F32)]
    steps = grid[0]
    for w, layer in casts:
        slab = (w.shape[1] // steps, w.shape[2])
        in_specs.append(pl.BlockSpec(
            (None,) + slab, lambda i, first=layer * steps: (first + i, 0, 0)))
        args.append(w.reshape((w.shape[0] * steps,) + slab))
        out_specs.append(pl.BlockSpec((None,) + slab, lambda i: (i, 0, 0)))
        out_shape.append(jax.ShapeDtypeStruct((steps,) + slab, BF16))
    outs = pl.pallas_call(
        functools.partial(_ffn_kernel, shift=shift, seg=tm, final=final,
                          n_casts=len(casts)),
        grid=grid,
        in_specs=in_specs,
        out_specs=out_specs,
        out_shape=out_shape,
        scratch_shapes=[pltpu.VMEM((tm, D_FF), BF16)],
        compiler_params=_params(1),
        name="convglu_ffn",
    )(*args)
    out = outs[0].reshape(n, d)
    if not casts:
        return out
    return out, [o.reshape(w.shape[1:]) for o, (w, _) in zip(outs[1:], casts)]


def _log_sigmoid(v):
    return -(jnp.maximum(-v, 0.0) + jnp.log(1.0 + jnp.exp(-jnp.abs(v))))


def _chunk_scans(v, axis, chunk, op, identity):
    n = v.shape[axis]
    shape = (n, 1) if axis == 0 else (1, n)
    pos = lax.broadcasted_iota(jnp.int32, shape, axis) & (chunk - 1)
    pre, suf = v, v
    step = 1
    while step < chunk:
        pre = op(pre, jnp.where(pos >= step,
                                pltpu.roll(pre, step, axis=axis), identity))
        suf = op(suf, jnp.where(pos < chunk - step,
                                pltpu.roll(suf, n - step, axis=axis), identity))
        step *= 2
    return pre, suf


def _mlstm_proj_kernel(x_ref, mod_ref, g_ref, wqv_ref, wkt_ref, wg_ref,
                       wgt_ref, bg_ref, bgt_ref,
                       qv_ref, kt_ref, gc_ref, gr_ref):
    nt = (((1,), (1,)), ((), ()))
    half = N_GATE // 2
    lane = lax.broadcasted_iota(jnp.int32, (1, GATE_PAD), 1)
    fwd = (lane & (half - 1)) < N_HEADS
    sub = lax.broadcasted_iota(jnp.int32, (N_GATE, 1), 0)
    tm = x_ref.shape[0]
    pieces = ROW_PIECES if (tm // ROW_PIECES) % SCAN_CHUNK == 0 else 1
    pr = tm // pieces
    for p in range(pieces):
        rows = slice(p * pr, (p + 1) * pr)
        hx = _modulate(x_ref[rows, :], g_ref[...], mod_ref[0:1, :],
                       mod_ref[1:2, :]).astype(BF16)

        gc = _dot(hx, wg_ref[...]) + bg_ref[...]
        pre, suf = _chunk_scans(_log_sigmoid(gc), 0, SCAN_CHUNK, jnp.add, 0.0)
        b = jnp.where(fwd, pre, suf)
        u = gc - pltpu.roll(b, GATE_PAD - half, axis=1)
        pre, suf = _chunk_scans(u, 0, SCAN_CHUNK, jnp.maximum, -jnp.inf)
        gc_ref[rows, :] = jnp.where(lane < half, jnp.where(fwd, pre, suf), b)

        gr = lax.dot_general(wgt_ref[...], hx, nt, preferred_element_type=F32)
        gr = gr + bgt_ref[...]
        pre, suf = _chunk_scans(_log_sigmoid(gr), 1, SCAN_CHUNK, jnp.add, 0.0)
        b = jnp.where((sub & (half - 1)) < N_HEADS, pre, suf)
        gr_ref[0:half, rows] = gr[0:half] - b[half:]
        gr_ref[half:, rows] = b[half:]

        qv_ref[rows, :] = _dot(hx, wqv_ref[...]).astype(BF16)
        kt = lax.dot_general(wkt_ref[...], hx, nt, preferred_element_type=F32)
        kt_ref[:, rows] = (kt * (DK ** -0.5)).astype(BF16)


def _scan_kernel(*refs, rev, has_add, readout):
    it = iter(refs)
    qv_ref, kt_ref, gc_ref, gr_ref, c0_ref, m0_ref = (next(it) for _ in range(6))
    hadd_ref = next(it) if has_add else None
    if readout:
        x_ref, mod_ref, g_ref, wo_ref, gain_ref, wout_ref = (
            next(it) for _ in range(6))
    out_ref, cf_ref, mf_ref = (next(it) for _ in range(3))
    c_s, m_s, a_s, x_s, s_s, w_s, ktw_s = (next(it) for _ in range(7))
    if readout:
        h_s, z_s = next(it), next(it)
    h_ref = h_s if readout else out_ref
    assert not (rev and readout)
    t = pl.program_id(1)
    n_t = pl.num_programs(1)
    L = SCAN_CHUNK
    nq = N_HEADS * DK
    n_chunks = qv_ref.shape[0] // L
    half = N_GATE // 2

    @pl.when(t == 0)
    def _():
        c_s[...] = c0_ref[...]
        m_s[...] = m0_ref[...]

    ti = lax.broadcasted_iota(jnp.int32, (L, L), 0)
    si = lax.broadcasted_iota(jnp.int32, (L, L), 1)
    mask = (si >= ti) if rev else (si <= ti)
    gcol = N_HEADS if rev else 0
    ones = jnp.ones((L, DK), BF16)
    order = range(n_chunks - 1, -1, -1) if rev else range(n_chunks)

    units = [(c, h) for c in order for h in range(N_HEADS)]

    m_in, mx_of, decay_of = {}, {}, {}
    for h in range(N_HEADS):
        gi = gcol + h
        m_prev = m_s[h]
        for c in order:
            r0, r1 = c * L, (c + 1) * L
            u_row = gr_ref[gi:gi + 1, r0:r1]
            b_row = gr_ref[half + gi:half + gi + 1, r0:r1]
            g = b_row[:, 0:1] if rev else b_row[:, L - 1:L]
            mx = jnp.maximum(m_prev, jnp.max(u_row, axis=1, keepdims=True))
            m_in[c, h], mx_of[c, h] = m_prev, mx
            decay_of[c, h] = jnp.exp(m_prev - mx)
            m_prev = g + mx
        m_s[h] = m_prev

    for c, h in units:
        r0, r1 = c * L, (c + 1) * L
        q = qv_ref[r0:r1, h * DK:(h + 1) * DK]
        kt = kt_ref[h * DK:(h + 1) * DK, r0:r1]
        s_s[c, h] = _dot(q, kt)
        u_row = gr_ref[gcol + h:gcol + h + 1, r0:r1]
        ktw_s[c, h] = (kt.astype(F32) * jnp.exp(u_row - mx_of[c, h])).astype(BF16)
    o_piece = 2 if n_chunks % 2 == 0 else 1
    for i, (c, h) in enumerate(units):
        r0, r1 = c * L, (c + 1) * L
        gi = gcol + h
        u_row = gr_ref[gi:gi + 1, r0:r1]
        cm_bc = jnp.broadcast_to(gc_ref[r0:r1, gi:gi + 1], (L, DK))
        r = jnp.maximum(m_in[c, h], cm_bc)
        w = jnp.exp(jnp.where(mask, u_row - r, -jnp.inf)) * s_s[c, h]
        w_s[c, h] = w.astype(BF16)
        if readout and i % (o_piece * N_HEADS) == 0:
            p0 = (i // (o_piece * N_HEADS)) * o_piece * L
            rows = slice(p0, p0 + o_piece * L)
            hx = _modulate(x_ref[rows, :], g_ref[...], mod_ref[0:1, :],
                           mod_ref[1:2, :]).astype(BF16)
            out_ref[rows, :] = _dot(hx, wo_ref[...])

    for c in order:
        r0, r1 = c * L, (c + 1) * L
        for h in range(N_HEADS):
            v = qv_ref[r0:r1, nq + h * DV:nq + (h + 1) * DV]
            v_aug = jnp.concatenate([v, ones], axis=1)
            a_s[h] = _dot(w_s[c, h], v_aug)
            x_s[h] = _dot(ktw_s[c, h], v_aug)
        for h in range(N_HEADS):
            gi = gcol + h
            q = qv_ref[r0:r1, h * DK:(h + 1) * DK]
            m_prev = m_in[c, h]
            c_prev = c_s[h]
            a_aug = a_s[h]
            b_aug = _dot(q, c_prev.astype(BF16))
            cm_bc = jnp.broadcast_to(gc_ref[r0:r1, gi:gi + 1], (L, DK))
            b_bc = jnp.broadcast_to(gc_ref[r0:r1, half + gi:half + gi + 1],
                                    (L, DK))
            r = jnp.maximum(m_prev, cm_bc)
            wi = jnp.exp(m_prev - r)
            den = a_aug[:, DV:] + wi * b_aug[:, DV:]
            inv = 1.0 / jnp.maximum(jnp.abs(den), jnp.exp(-(b_bc + r)))
            beta = wi * inv
            for j in range(DV // DK):
                lo, hi = j * DK, (j + 1) * DK
                cols = slice(h * DV + lo, h * DV + hi)
                hh = inv * a_aug[:, lo:hi] + beta * b_aug[:, lo:hi]
                if has_add:
                    hh = hh + hadd_ref[r0:r1, cols]
                h_ref[r0:r1, cols] = hh
            for j in range((DV + DK) // DK):
                lo, hi = j * DK, (j + 1) * DK
                c_s[h, :, lo:hi] = (decay_of[c, h] * c_prev[:, lo:hi]
                                    + x_s[h, :, lo:hi])

    @pl.when(t == n_t - 1)
    def _():
        cf_ref[...] = c_s[...]
        mf_ref[...] = m_s[...]

    if readout:
        pr = o_piece * L
        for p in range(n_chunks // o_piece):
            rows = slice(p * pr, (p + 1) * pr)
            for h in range(N_HEADS):
                lo, hi = h * DV, (h + 1) * DV
                hh = h_s[rows, lo:hi]
                ms = jnp.mean(hh * hh, axis=-1, keepdims=True)
                hn = hh * lax.rsqrt(ms + EPS) * gain_ref[:, lo:hi]
                z_s[rows, lo:hi] = (hn * _sigmoid(out_ref[rows, lo:hi])
                                    ).astype(BF16)
            y = _dot(z_s[rows, :], wout_ref[...])
            out_ref[rows, :] = x_ref[rows, :] + mod_ref[2:3, :] * y


def _scan_scratch(n_chunks):
    return [
        pltpu.VMEM((N_HEADS, DK, DV + DK), F32),
        pltpu.VMEM((N_HEADS, 1, DK), F32),
        pltpu.VMEM((N_HEADS, SCAN_CHUNK, DV + DK), F32),
        pltpu.VMEM((N_HEADS, DK, DV + DK), F32),
        pltpu.VMEM((n_chunks, N_HEADS, SCAN_CHUNK, SCAN_CHUNK), F32),
        pltpu.VMEM((n_chunks, N_HEADS, SCAN_CHUNK, SCAN_CHUNK), BF16),
        pltpu.VMEM((n_chunks, N_HEADS, DK, SCAN_CHUNK), BF16),
    ]


def _scan(qv, kt, gc, gr, state, *, batch, ts, rev, add=None, readout=None):
    n = qv.shape[0]
    n_t = n // batch // ts
    n_chunks = ts // SCAN_CHUNK
    nq = N_HEADS * DK
    nv = N_HEADS * DV
    nqv = nq + nv

    def blk(b, t):
        return b * n_t + ((n_t - 1 - t) if rev else t)

    st_shapes = [(N_HEADS, DK, DV + DK), (N_HEADS, 1, DK)]
    st_specs = [pl.BlockSpec((None,) + s, lambda b, t: (b, 0, 0, 0))
                for s in st_shapes]
    h_spec = pl.BlockSpec((ts, nv), lambda b, t: (blk(b, t), 0))
    has_add = add is not None
    in_specs = [
        pl.BlockSpec((ts, nqv), lambda b, t: (blk(b, t), 0)),
        pl.BlockSpec((nq, ts), lambda b, t: (0, blk(b, t))),
        pl.BlockSpec((ts, GATE_PAD), lambda b, t: (blk(b, t), 0)),
        pl.BlockSpec((N_GATE, ts), lambda b, t: (0, blk(b, t))),
    ] + st_specs
    args = [qv, kt, gc, gr, *state]
    scratch = _scan_scratch(n_chunks)
    if has_add:
        in_specs.append(h_spec)
        args.append(add)
    out_cols = nv
    if readout is not None:
        x2d, mod_l, mod_row_fn, g, w_o, gain, w_out = readout
        wout_spec, w_out = _weight(w_out)
        d = x2d.shape[1]
        out_cols = d
        in_specs += [
            pl.BlockSpec((ts, d), lambda b, t: (blk(b, t), 0)),
            pl.BlockSpec((None, 6, d), lambda b, t: (mod_row_fn(b), 0, 0)),
            _const_spec((1, d)),
            _const_spec((d, nv)),
            _const_spec((1, nv)),
            wout_spec,
        ]
        args += [x2d, mod_l, g, w_o, gain, w_out]
        scratch += [pltpu.VMEM((ts, nv), F32), pltpu.VMEM((ts, nv), BF16)]
    out, cf, mf = pl.pallas_call(
        functools.partial(_scan_kernel, rev=rev, has_add=has_add,
                          readout=readout is not None),
        grid=(batch, n_t),
        in_specs=in_specs,
        out_specs=[pl.BlockSpec((ts, out_cols), lambda b, t: (blk(b, t), 0))]
        + st_specs,
        out_shape=[jax.ShapeDtypeStruct((n, out_cols), F32)]
        + [jax.ShapeDtypeStruct((batch,) + s, F32) for s in st_shapes],
        scratch_shapes=scratch,
        compiler_params=_params(2),
        name="mlstm_scan_bwd" if rev else "mlstm_scan_fwd",
    )(*args)
    return out, (cf, mf)


N_PROJ_IN = 9
N_PROJ_OUT = 4


def _proj_scan_kernel(*refs, n_casts):
    proj_in = refs[:N_PROJ_IN]
    c0_ref, m0_ref = refs[N_PROJ_IN:N_PROJ_IN + 2]
    cast_in = refs[N_PROJ_IN + 2:N_PROJ_IN + 2 + n_casts]
    outs = refs[N_PROJ_IN + 2 + n_casts:]
    proj_out = outs[:N_PROJ_OUT]
    n_scan_out = 3
    cast_out = outs[N_PROJ_OUT + n_scan_out:N_PROJ_OUT + n_scan_out + n_casts]
    scan_refs = (outs[N_PROJ_OUT:N_PROJ_OUT + n_scan_out]
                 + outs[N_PROJ_OUT + n_scan_out + n_casts:])
    for src, dst in zip(cast_in, cast_out):
        dst[...] = src[...].astype(BF16)
    _mlstm_proj_kernel(*proj_in, *proj_out)
    _scan_kernel(*proj_out, c0_ref, m0_ref, *scan_refs, rev=True,
                 has_add=False, readout=False)


def _proj_scan(x2d, mod_l, mod_row_fn, g, wts, state, *, batch, ts, casts=()):
    n, d = x2d.shape
    n_t = n // batch // ts
    n_chunks = ts // SCAN_CHUNK
    nq = N_HEADS * DK
    nv = N_HEADS * DV
    nqv = nq + nv

    def blk(b, t):
        return b * n_t + (n_t - 1 - t)

    rows = lambda w: pl.BlockSpec((ts, w), lambda b, t: (blk(b, t), 0))
    cols = lambda h: pl.BlockSpec((h, ts), lambda b, t: (0, blk(b, t)))
    st_shapes = [(N_HEADS, DK, DV + DK), (N_HEADS, 1, DK)]
    st_specs = [pl.BlockSpec((None,) + s, lambda b, t: (b, 0, 0, 0))
                for s in st_shapes]
    steps = batch * n_t
    cast_in_specs, cast_args, cast_out_specs, cast_shapes = [], [], [], []
    for w, layer in casts:
        slab = (w.shape[1] // steps, w.shape[2])
        cast_in_specs.append(pl.BlockSpec(
            (None,) + slab,
            lambda b, t, first=layer * steps: (first + b * n_t + t, 0, 0)))
        cast_args.append(w.reshape((w.shape[0] * steps,) + slab))
        cast_out_specs.append(pl.BlockSpec((None,) + slab,
                                           lambda b, t: (b * n_t + t, 0, 0)))
        cast_shapes.append(jax.ShapeDtypeStruct((steps,) + slab, BF16))
    outs = pl.pallas_call(
        functools.partial(_proj_scan_kernel, n_casts=len(casts)),
        grid=(batch, n_t),
        in_specs=[
            rows(d),
            pl.BlockSpec((None, 6, d), lambda b, t: (mod_row_fn(b), 0, 0)),
            _const_spec((1, d)),
            _const_spec((d, nqv)),
            _const_spec((nq, d)),
            _const_spec((d, GATE_PAD)),
            _const_spec((N_GATE, d)),
            _const_spec((1, GATE_PAD)),
            _const_spec((N_GATE, 1)),
        ] + st_specs + cast_in_specs,
        out_specs=[rows(nqv), cols(nq), rows(GATE_PAD), cols(N_GATE), rows(nv)]
        + st_specs + cast_out_specs,
        out_shape=[
            jax.ShapeDtypeStruct((n, nqv), BF16),
            jax.ShapeDtypeStruct((nq, n), BF16),
            jax.ShapeDtypeStruct((n, GATE_PAD), F32),
            jax.ShapeDtypeStruct((N_GATE, n), F32),
            jax.ShapeDtypeStruct((n, nv), F32),
        ] + [jax.ShapeDtypeStruct((batch,) + s, F32) for s in st_shapes]
        + cast_shapes,
        scratch_shapes=_scan_scratch(n_chunks),
        compiler_params=_params(2),
        name="mlstm_proj_scan_bwd",
    )(x2d, mod_l, g, wts["w_qv"], wts["w_kt"], wts["w_g"], wts["w_gt"],
      wts["b_g"], wts["b_gt"], *state, *cast_args)
    qv, kt, gc, gr, h, cf, mf = outs[:7]
    if not casts:
        return (qv, kt, gc, gr), h, (cf, mf)
    cast = [o.reshape(w.shape[1:]) for o, (w, _) in zip(outs[7:], casts)]
    return (qv, kt, gc, gr), h, (cf, mf), cast


def _mlstm_weights(w_in, b_gate):
    nq = N_HEADS * DK
    o2 = 2 * nq
    o3 = o2 + N_HEADS * DV
    o4 = o3 + D_MODEL
    w_g = w_in[:, o4:]
    return {
        "w_qv": jnp.concatenate([w_in[:, :nq], w_in[:, o2:o3]], axis=1).astype(BF16),
        "w_kt": w_in[:, nq:o2].T.astype(BF16),
        "w_o": w_in[:, o3:o4].astype(BF16),
        "w_g": jnp.pad(w_g, ((0, 0), (0, GATE_PAD - N_GATE))).astype(BF16),
        "w_gt": w_g.T.astype(BF16),
        "b_g": jnp.pad(b_gate, (0, GATE_PAD - N_GATE)).reshape(1, GATE_PAD),
        "b_gt": b_gate.reshape(N_GATE, 1),
    }


def kernel(x, c, ctx, c_ctx, w_mod, b_mod, g_mix, g_ffn, a_w_in, a_b_gate,
           a_head_gain, a_w_out, b_w_in, b_w_conv, b_w_out, f_w_up, f_w_conv,
           f_b_conv, f_w_down, g_final):
    bn, t, d = x.shape
    n_ctx = ctx.shape[1]
    depth = w_mod.shape[0]
    n_mixers = 2
    assert d == D_MODEL and bn <= 7 and t % TOKEN_TILE == 0
    assert n_ctx % SCAN_CHUNK == 0 and TOKEN_TILE % SCAN_CHUNK == 0

    rec_layers = [i for i in range(depth) if i % n_mixers == 0]
    last_rec = max(rec_layers) if rec_layers else -1

    cc = jnp.zeros((8, d), F32).at[:bn].set(c).at[bn].set(c_ctx)
    mod = _mod_all(cc, w_mod, b_mod).reshape(depth, 8, 6, d)

    tm = TOKEN_TILE
    tiles_per_batch = t // tm
    lat_row = lambda i: i // tiles_per_batch
    ctx_row = lambda i: bn

    xs = x.reshape(bn * t, d)
    cs = ctx.reshape(bn * n_ctx, d)
    zero_state = (jnp.zeros((bn, N_HEADS, DK, DV + DK), F32),
                  jnp.zeros((bn, N_HEADS, 1, DK), F32))

    ffn_w = conv_w = mlstm_w = None

    for i in range(depth):
        j = i // n_mixers
        ctx_read = i <= last_rec
        ctx_live = i < last_rec
        gm = g_mix[i].reshape(1, d)
        gf = g_ffn[i].reshape(1, d)
        if i % n_mixers == 0:
            if mlstm_w is None:
                mlstm_w = (a_w_in[j], a_w_out[j].astype(BF16))
            wts = _mlstm_weights(mlstm_w[0], a_b_gate[j])
            w_out = mlstm_w[1]
            mlstm_w = None
            gain = a_head_gain[j].reshape(1, N_HEADS * DV)
            s_f = s_b = zero_state
            if ctx_read:
                pc, hc, s_b = _proj_scan(cs, mod[i], lambda b: bn, gm, wts,
                                         zero_state, batch=bn, ts=n_ctx)
                ro = ((cs, mod[i], lambda b: bn, gm, wts["w_o"], gain, w_out)
                      if ctx_live else None)
                cs_new, s_f = _scan(*pc, zero_state, batch=bn, ts=n_ctx,
                                    rev=False, add=hc, readout=ro)
            if i == 0:
                px, hl, _, ffn_w = _proj_scan(
                    xs, mod[i], lambda b: b, gm, wts, s_b, batch=bn, ts=tm,
                    casts=[(f_w_up, 0), (f_w_down, 0)])
            else:
                px, hl, _ = _proj_scan(xs, mod[i], lambda b: b, gm, wts, s_b,
                                       batch=bn, ts=tm)
            ro = (xs, mod[i], lambda b: b, gm, wts["w_o"], gain, w_out)
            xs, _ = _scan(*px, s_f, batch=bn, ts=tm, rev=False, add=hl,
                          readout=ro)
            if ctx_live:
                cs = cs_new
        else:
            if conv_w is None:
                conv_w = (b_w_in[j].astype(BF16), b_w_out[j].astype(BF16))
            w_in, w_out = conv_w
            conv_w = None
            xs = _shortconv(xs, mod[i], gm, w_in, b_w_conv[j], w_out,
                            tm=tm, seg=GRID_W, row_fn=lat_row)
            if ctx_live:
                cs = _shortconv(cs, mod[i], gm, w_in, b_w_conv[j], w_out,
                                tm=n_ctx, seg=n_ctx, row_fn=ctx_row)
        if ffn_w is None:
            ffn_w = (f_w_up[i].astype(BF16), f_w_down[i].astype(BF16))
        w_up, w_down = ffn_w
        ffn_w = None
        b_conv = f_b_conv[i].reshape(1, D_FF)
        gfin = g_final.reshape(1, d)
        if i + 1 < depth:
            casts = [(f_w_up, i + 1), (f_w_down, i + 1)]
            nxt = (i + 1) // n_mixers
            if (i + 1) % n_mixers == 1:
                casts += [(b_w_in, nxt), (b_w_out, nxt)]
            else:
                casts += [(a_w_in, nxt), (a_w_out, nxt)]
            xs, cast = _ffn(xs, mod[i], gf, w_up, f_w_conv[i], b_conv, w_down,
                            gfin, batch=bn, strip=FFN_STRIP, final=False,
                            casts=casts)
            ffn_w = cast[:2]
            if (i + 1) % n_mixers == 1:
                conv_w = cast[2:]
            else:
                mlstm_w = cast[2:]
        else:
            xs = _ffn(xs, mod[i], gf, w_up, f_w_conv[i], b_conv, w_down, gfin,
                      batch=bn, strip=FFN_STRIP, final=True)
        if ctx_live:
            cs = _ffn(cs, mod[i], gf, w_up, f_w_conv[i], b_conv, w_down, gfin,
                      batch=bn, strip=0, final=False)

    return xs.reshape(bn, t, d)
```

```python
import functools

import jax
import jax.numpy as jnp
from jax import lax
from jax.experimental import pallas as pl
from jax.experimental.pallas import tpu as pltpu

F32 = jnp.float32
BF16 = jnp.bfloat16

D_MODEL = 1024
GRID_W = 64
N_HEADS = 4
DK = 128
DV = 256
D_FF = 2816
EPS = 1e-6
N_GATE = 16
GATE_PAD = 128

SCAN_CHUNK = 128
FF_CHUNK = 256
DOWN_PIECES = 2
ROW_PIECES = 2
TOKEN_TILE = 1024
STAGE_CHUNKS = 4
FFN_STRIP = 8
VMEM_LIMIT = 56 * 1024 * 1024


def _params(n_axes):
    return pltpu.CompilerParams(
        dimension_semantics=("arbitrary",) * n_axes,
        vmem_limit_bytes=VMEM_LIMIT)


def _const_spec(shape):
    zeros = (0,) * len(shape)
    return pl.BlockSpec(shape, lambda *_: zeros, pipeline_mode=pl.Buffered(1))


def _weight(w):
    if isinstance(w, tuple):
        arr, layer = w
        idx = (layer,) + (0,) * (arr.ndim - 1)
        spec = pl.BlockSpec((None,) + arr.shape[1:], lambda *_: idx,
                            pipeline_mode=pl.Buffered(1))
        return spec, arr
    return _const_spec(w.shape), w


def _dot(a, b):
    return jnp.dot(a, b, preferred_element_type=F32)


def _sigmoid(v):
    return 1.0 / (1.0 + jnp.exp(-v))


def _rmsnorm(x, g):
    ms = jnp.mean(x * x, axis=-1, keepdims=True)
    return (x * lax.rsqrt(ms + EPS)) * g


def _modulate(x, g, shift, scale):
    return _rmsnorm(x, g) * (1.0 + scale) + shift


def _shift_conv(u, w_ref, seg, shift=1):
    n = u.shape[0]
    if seg == n and shift % 8 == 0:
        pad = jnp.zeros((shift, u.shape[1]), u.dtype)
        prev = jnp.concatenate([pad, u[:n - shift]], axis=0)
        nxt = jnp.concatenate([u[shift:], pad], axis=0)
    else:
        pos = lax.broadcasted_iota(jnp.int32, (n, 1), 0) & (seg - 1)
        prev = jnp.where(pos >= shift, pltpu.roll(u, shift, axis=0), 0.0)
        nxt = jnp.where(pos < seg - shift, pltpu.roll(u, n - shift, axis=0), 0.0)
    return prev * w_ref[0:1, :] + u * w_ref[1:2, :] + nxt * w_ref[2:3, :]


def _mod_kernel(cc_ref, w_ref, b_ref, o_ref):
    cc = cc_ref[...]
    s = cc * _sigmoid(cc)
    o_ref[...] = _dot(s, w_ref[...]) + b_ref[...]


def _mod_all(cc, w_mod, b_mod):
    depth, d, n = w_mod.shape
    tn = 1536
    return pl.pallas_call(
        _mod_kernel,
        grid=(depth, n // tn),
        in_specs=[
            pl.BlockSpec((8, d), lambda l, j: (0, 0)),
            pl.BlockSpec((None, d, tn), lambda l, j: (l, 0, j)),
            pl.BlockSpec((None, 1, tn), lambda l, j: (l, 0, j)),
        ],
        out_specs=pl.BlockSpec((None, 8, tn), lambda l, j: (l, 0, j)),
        out_shape=jax.ShapeDtypeStruct((depth, 8, n), F32),
        compiler_params=_params(2),
        name="adaln_mod",
    )(cc, w_mod, b_mod.reshape(depth, 1, n))


def _shortconv_kernel(x_ref, mod_ref, g_ref, win_ref, wconv_ref, wout_ref,
                      o_ref, z_ref, *, seg):
    d = D_MODEL
    cw = 256
    tm = x_ref.shape[0]
    pieces = ROW_PIECES if (tm // ROW_PIECES) % seg == 0 else 1
    pr = tm // pieces
    for p in range(pieces):
        rows = slice(p * pr, (p + 1) * pr)
        x = x_ref[rows, :]
        hx = _modulate(x, g_ref[...], mod_ref[0:1, :], mod_ref[1:2, :]).astype(BF16)
        for j in range(d // cw):
            lo, hi = j * cw, (j + 1) * cw
            bg = _dot(hx, win_ref[:, lo:hi])
            cg = _dot(hx, win_ref[:, d + lo:d + hi])
            xv = _dot(hx, win_ref[:, 2 * d + lo:2 * d + hi])
            cv = _shift_conv(cg * xv, wconv_ref.at[:, lo:hi], seg)
            z_ref[rows, lo:hi] = (bg * cv).astype(BF16)
        y = _dot(z_ref[rows, :], wout_ref[...])
        o_ref[rows, :] = x + mod_ref[2:3, :] * y


def _shortconv(x2d, mod_l, g, w_in, w_conv, w_out, *, tm, seg, row_fn):
    n, d = x2d.shape
    win_spec, w_in = _weight(w_in)
    wout_spec, w_out = _weight(w_out)
    return pl.pallas_call(
        functools.partial(_shortconv_kernel, seg=seg),
        grid=(n // tm,),
        in_specs=[
            pl.BlockSpec((tm, d), lambda i: (i, 0)),
            pl.BlockSpec((None, 6, d), lambda i: (row_fn(i), 0, 0)),
            _const_spec((1, d)),
            win_spec,
            _const_spec((3, d)),
            wout_spec,
        ],
        out_specs=pl.BlockSpec((tm, d), lambda i: (i, 0)),
        out_shape=jax.ShapeDtypeStruct((n, d), F32),
        scratch_shapes=[pltpu.VMEM((tm, d), BF16)],
        compiler_params=_params(1),
        name="shortconv_mixer",
    )(x2d, mod_l, g, w_in, w_conv, w_out)


def _ffn_kernel(*refs, shift, seg, final, n_casts):
    (x_ref, mod_ref, g_ref, wup_ref, wconv_ref, bconv_ref, wdown_ref,
     gfin_ref) = refs[:8]
    o_ref = refs[8 + n_casts]
    act_ref = refs[-1]
    for src, dst in zip(refs[8:8 + n_casts], refs[9 + n_casts:-1]):
        dst[...] = src[...].astype(BF16)
    d = x_ref.shape[-1]
    x = x_ref[...].reshape(-1, d)
    hx = _modulate(x, g_ref[...], mod_ref[3:4, :], mod_ref[4:5, :]).astype(BF16)
    for c in range(D_FF // FF_CHUNK):
        lo, hi = c * FF_CHUNK, (c + 1) * FF_CHUNK
        val = _dot(hx, wup_ref[:, D_FF + lo:D_FF + hi])
        gate = _shift_conv(_dot(hx, wup_ref[:, lo:hi]), wconv_ref.at[:, lo:hi],
                           seg, shift)
        gate = gate + bconv_ref[:, lo:hi]
        act_ref[:, lo:hi] = (gate * _sigmoid(gate) * val).astype(BF16)
    tm = x.shape[0]
    pr = tm // DOWN_PIECES
    lead = o_ref.shape[0] // DOWN_PIECES
    for p in range(DOWN_PIECES):
        rows = slice(p * pr, (p + 1) * pr)
        y = _dot(act_ref[rows, :], wdown_ref[...])
        out = x[rows] + mod_ref[5:6, :] * y
        if final:
            out = _rmsnorm(out, gfin_ref[...])
        o_ref[p * lead:(p + 1) * lead] = out.reshape((lead,) + o_ref.shape[1:])


def _ffn(x2d, mod_l, g, w_up, w_conv, b_conv, w_down, g_final, *, batch,
         strip, final, casts=()):
    n, d = x2d.shape
    wup_spec, w_up = _weight(w_up)
    wdown_spec, w_down = _weight(w_down)
    per = n // batch
    if strip:
        rows = per // GRID_W
        strips = GRID_W // strip
        tm = rows * strip
        xin = x2d.reshape(batch * rows, GRID_W, d)
        x_spec = pl.BlockSpec((rows, strip, d),
                              lambda i: (i // strips, i % strips, 0))
        grid = (batch * strips,)
        row_fn = lambda i: i // strips
        shift = strip
    else:
        tm = per
        xin = x2d
        x_spec = pl.BlockSpec((tm, d), lambda i: (i, 0))
        grid = (batch,)
        row_fn = lambda i: batch
        shift = 1
    in_specs = [
        x_spec,
        pl.BlockSpec((None, 6, d), lambda i: (row_fn(i), 0, 0)),
        _const_spec((1, d)),
        wup_spec,
        _const_spec((3, D_FF)),
        _const_spec((1, D_FF)),
        wdown_spec,
        _const_spec((1, d)),
    ]
    args = [xin, mod_l, g, w_up, w_conv, b_conv, w_down, g_final]
    out_specs = [x_spec]
    out_shape = [jax.ShapeDtypeStruct(xin.shape, F32)]
    steps = grid[0]
    for w, layer in casts:
        slab = (w.shape[1] // steps, w.shape[2])
        in_specs.append(pl.BlockSpec(
            (None,) + slab, lambda i, first=layer * steps: (first + i, 0, 0)))
        args.append(w.reshape((w.shape[0] * steps,) + slab))
        out_specs.append(pl.BlockSpec((None,) + slab, lambda i: (i, 0, 0)))
        out_shape.append(jax.ShapeDtypeStruct((steps,) + slab, BF16))
    outs = pl.pallas_call(
        functools.partial(_ffn_kernel, shift=shift, seg=tm, final=final,
                          n_casts=len(casts)),
        grid=grid,
        in_specs=in_specs,
        out_specs=out_specs,
        out_shape=out_shape,
        scratch_shapes=[pltpu.VMEM((tm, D_FF), BF16)],
        compiler_params=_params(1),
        name="convglu_ffn",
    )(*args)
    out = outs[0].reshape(n, d)
    if not casts:
        return out
    return out, [o.reshape(w.shape[1:]) for o, (w, _) in zip(outs[1:], casts)]


def _log_sigmoid(v):
    return -(jnp.maximum(-v, 0.0) + jnp.log(1.0 + jnp.exp(-jnp.abs(v))))


def _chunk_scans(v, axis, chunk, op, identity):
    n = v.shape[axis]
    shape = (n, 1) if axis == 0 else (1, n)
    pos = lax.broadcasted_iota(jnp.int32, shape, axis) & (chunk - 1)
    pre, suf = v, v
    step = 1
    while step < chunk:
        pre = op(pre, jnp.where(pos >= step,
                                pltpu.roll(pre, step, axis=axis), identity))
        suf = op(suf, jnp.where(pos < chunk - step,
                                pltpu.roll(suf, n - step, axis=axis), identity))
        step *= 2
    return pre, suf


def _mlstm_proj_kernel(x_ref, mod_ref, g_ref, wqv_ref, wkt_ref, wg_ref,
                       wgt_ref, bg_ref, bgt_ref,
                       qv_ref, kt_ref, gc_ref, gr_ref):
    nt = (((1,), (1,)), ((), ()))
    half = N_GATE // 2
    lane = lax.broadcasted_iota(jnp.int32, (1, GATE_PAD), 1)
    fwd = (lane & (half - 1)) < N_HEADS
    sub = lax.broadcasted_iota(jnp.int32, (N_GATE, 1), 0)
    tm = x_ref.shape[0]
    pieces = ROW_PIECES if (tm // ROW_PIECES) % SCAN_CHUNK == 0 else 1
    pr = tm // pieces
    for p in range(pieces):
        rows = slice(p * pr, (p + 1) * pr)
        hx = _modulate(x_ref[rows, :], g_ref[...], mod_ref[0:1, :],
                       mod_ref[1:2, :]).astype(BF16)

        gc = _dot(hx, wg_ref[...]) + bg_ref[...]
        pre, suf = _chunk_scans(_log_sigmoid(gc), 0, SCAN_CHUNK, jnp.add, 0.0)
        b = jnp.where(fwd, pre, suf)
        u = gc - pltpu.roll(b, GATE_PAD - half, axis=1)
        pre, suf = _chunk_scans(u, 0, SCAN_CHUNK, jnp.maximum, -jnp.inf)
        gc_ref[rows, :] = jnp.where(lane < half, jnp.where(fwd, pre, suf), b)

        gr = lax.dot_general(wgt_ref[...], hx, nt, preferred_element_type=F32)
        gr = gr + bgt_ref[...]
        pre, suf = _chunk_scans(_log_sigmoid(gr), 1, SCAN_CHUNK, jnp.add, 0.0)
        b = jnp.where((sub & (half - 1)) < N_HEADS, pre, suf)
        gr_ref[0:half, rows] = gr[0:half] - b[half:]
        gr_ref[half:, rows] = b[half:]

        qv_ref[rows, :] = _dot(hx, wqv_ref[...]).astype(BF16)
        kt = lax.dot_general(wkt_ref[...], hx, nt, preferred_element_type=F32)
        kt_ref[:, rows] = (kt * (DK ** -0.5)).astype(BF16)


def _scan_kernel(*refs, rev, has_add, readout):
    it = iter(refs)
    qv_ref, kt_ref, gc_ref, gr_ref, c0_ref, m0_ref = (next(it) for _ in range(6))
    hadd_ref = next(it) if has_add else None
    if readout:
        x_ref, mod_ref, g_ref, wo_ref, gain_ref, wout_ref = (
            next(it) for _ in range(6))
    out_ref, cf_ref, mf_ref = (next(it) for _ in range(3))
    c_s, m_s, a_s, x_s, s_s, w_s, ktw_s = (next(it) for _ in range(7))
    if readout:
        h_s, z_s = next(it), next(it)
    h_ref = h_s if readout else out_ref
    assert not (rev and readout)
    t = pl.program_id(1)
    n_t = pl.num_programs(1)
    L = SCAN_CHUNK
    nq = N_HEADS * DK
    n_chunks = qv_ref.shape[0] // L
    half = N_GATE // 2

    @pl.when(t == 0)
    def _():
        c_s[...] = c0_ref[...]
        m_s[...] = m0_ref[...]

    ti = lax.broadcasted_iota(jnp.int32, (L, L), 0)
    si = lax.broadcasted_iota(jnp.int32, (L, L), 1)
    mask = (si >= ti) if rev else (si <= ti)
    gcol = N_HEADS if rev else 0
    ones = jnp.ones((L, DK), BF16)
    order = range(n_chunks - 1, -1, -1) if rev else range(n_chunks)

    m_in, mx_of, decay_of = {}, {}, {}
    for h in range(N_HEADS):
        gi = gcol + h
        m_prev = m_s[h]
        for c in order:
            r0, r1 = c * L, (c + 1) * L
            u_row = gr_ref[gi:gi + 1, r0:r1]
            b_row = gr_ref[half + gi:half + gi + 1, r0:r1]
            g = b_row[:, 0:1] if rev else b_row[:, L - 1:L]
            mx = jnp.maximum(m_prev, jnp.max(u_row, axis=1, keepdims=True))
            m_in[c, h], mx_of[c, h] = m_prev, mx
            decay_of[c, h] = jnp.exp(m_prev - mx)
            m_prev = g + mx
        m_s[h] = m_prev

    o_piece = 2 if n_chunks % 2 == 0 else 1
    order = list(order)
    for g0 in range(0, n_chunks, STAGE_CHUNKS):
        grp = order[g0:g0 + STAGE_CHUNKS]
        units = [(c, h) for c in grp for h in range(N_HEADS)]
        base = g0 * N_HEADS
        for c, h in units:
            r0, r1 = c * L, (c + 1) * L
            q = qv_ref[r0:r1, h * DK:(h + 1) * DK]
            kt = kt_ref[h * DK:(h + 1) * DK, r0:r1]
            s_s[c, h] = _dot(q, kt)
            u_row = gr_ref[gcol + h:gcol + h + 1, r0:r1]
            ktw_s[c, h] = (kt.astype(F32) * jnp.exp(u_row - mx_of[c, h])).astype(BF16)
        for i, (c, h) in enumerate(units, start=base):
            r0, r1 = c * L, (c + 1) * L
            gi = gcol + h
            u_row = gr_ref[gi:gi + 1, r0:r1]
            cm_bc = jnp.broadcast_to(gc_ref[r0:r1, gi:gi + 1], (L, DK))
            r = jnp.maximum(m_in[c, h], cm_bc)
            w = jnp.exp(jnp.where(mask, u_row - r, -jnp.inf)) * s_s[c, h]
            w_s[c, h] = w.astype(BF16)
            if readout and i % (o_piece * N_HEADS) == 0:
                p0 = (i // (o_piece * N_HEADS)) * o_piece * L
                rows = slice(p0, p0 + o_piece * L)
                hx = _modulate(x_ref[rows, :], g_ref[...], mod_ref[0:1, :],
                               mod_ref[1:2, :]).astype(BF16)
                out_ref[rows, :] = _dot(hx, wo_ref[...])

        for c in grp:
            r0, r1 = c * L, (c + 1) * L
            for h in range(N_HEADS):
                v = qv_ref[r0:r1, nq + h * DV:nq + (h + 1) * DV]
                v_aug = jnp.concatenate([v, ones], axis=1)
                a_s[h] = _dot(w_s[c, h], v_aug)
                x_s[h] = _dot(ktw_s[c, h], v_aug)
            for h in range(N_HEADS):
                gi = gcol + h
                q = qv_ref[r0:r1, h * DK:(h + 1) * DK]
                m_prev = m_in[c, h]
                c_prev = c_s[h]
                a_aug = a_s[h]
                b_aug = _dot(q, c_prev.astype(BF16))
                cm_bc = jnp.broadcast_to(gc_ref[r0:r1, gi:gi + 1], (L, DK))
                b_bc = jnp.broadcast_to(gc_ref[r0:r1, half + gi:half + gi + 1],
                                        (L, DK))
                r = jnp.maximum(m_prev, cm_bc)
                wi = jnp.exp(m_prev - r)
                den = a_aug[:, DV:] + wi * b_aug[:, DV:]
                inv = 1.0 / jnp.maximum(jnp.abs(den), jnp.exp(-(b_bc + r)))
                beta = wi * inv
                for j in range(DV // DK):
                    lo, hi = j * DK, (j + 1) * DK
                    cols = slice(h * DV + lo, h * DV + hi)
                    hh = inv * a_aug[:, lo:hi] + beta * b_aug[:, lo:hi]
                    if has_add:
                        hh = hh + hadd_ref[r0:r1, cols]
                    h_ref[r0:r1, cols] = hh
                for j in range((DV + DK) // DK):
                    lo, hi = j * DK, (j + 1) * DK
                    c_s[h, :, lo:hi] = (decay_of[c, h] * c_prev[:, lo:hi]
                                        + x_s[h, :, lo:hi])

    @pl.when(t == n_t - 1)
    def _():
        cf_ref[...] = c_s[...]
        mf_ref[...] = m_s[...]

    if readout:
        pr = o_piece * L
        for p in range(n_chunks // o_piece):
            rows = slice(p * pr, (p + 1) * pr)
            for h in range(N_HEADS):
                lo, hi = h * DV, (h + 1) * DV
                hh = h_s[rows, lo:hi]
                ms = jnp.mean(hh * hh, axis=-1, keepdims=True)
                hn = hh * lax.rsqrt(ms + EPS) * gain_ref[:, lo:hi]
                z_s[rows, lo:hi] = (hn * _sigmoid(out_ref[rows, lo:hi])
                                    ).astype(BF16)
            y = _dot(z_s[rows, :], wout_ref[...])
            out_ref[rows, :] = x_ref[rows, :] + mod_ref[2:3, :] * y


def _scan_scratch(n_chunks):
    return [
        pltpu.VMEM((N_HEADS, DK, DV + DK), F32),
        pltpu.VMEM((N_HEADS, 1, DK), F32),
        pltpu.VMEM((N_HEADS, SCAN_CHUNK, DV + DK), F32),
        pltpu.VMEM((N_HEADS, DK, DV + DK), F32),
        pltpu.VMEM((n_chunks, N_HEADS, SCAN_CHUNK, SCAN_CHUNK), F32),
        pltpu.VMEM((n_chunks, N_HEADS, SCAN_CHUNK, SCAN_CHUNK), BF16),
        pltpu.VMEM((n_chunks, N_HEADS, DK, SCAN_CHUNK), BF16),
    ]


def _scan(qv, kt, gc, gr, state, *, batch, ts, rev, add=None, readout=None):
    n = qv.shape[0]
    n_t = n // batch // ts
    n_chunks = ts // SCAN_CHUNK
    nq = N_HEADS * DK
    nv = N_HEADS * DV
    nqv = nq + nv

    def blk(b, t):
        return b * n_t + ((n_t - 1 - t) if rev else t)

    st_shapes = [(N_HEADS, DK, DV + DK), (N_HEADS, 1, DK)]
    st_specs = [pl.BlockSpec((None,) + s, lambda b, t: (b, 0, 0, 0))
                for s in st_shapes]
    h_spec = pl.BlockSpec((ts, nv), lambda b, t: (blk(b, t), 0))
    has_add = add is not None
    in_specs = [
        pl.BlockSpec((ts, nqv), lambda b, t: (blk(b, t), 0)),
        pl.BlockSpec((nq, ts), lambda b, t: (0, blk(b, t))),
        pl.BlockSpec((ts, GATE_PAD), lambda b, t: (blk(b, t), 0)),
        pl.BlockSpec((N_GATE, ts), lambda b, t: (0, blk(b, t))),
    ] + st_specs
    args = [qv, kt, gc, gr, *state]
    scratch = _scan_scratch(n_chunks)
    if has_add:
        in_specs.append(h_spec)
        args.append(add)
    out_cols = nv
    if readout is not None:
        x2d, mod_l, mod_row_fn, g, w_o, gain, w_out = readout
        wout_spec, w_out = _weight(w_out)
        d = x2d.shape[1]
        out_cols = d
        in_specs += [
            pl.BlockSpec((ts, d), lambda b, t: (blk(b, t), 0)),
            pl.BlockSpec((None, 6, d), lambda b, t: (mod_row_fn(b), 0, 0)),
            _const_spec((1, d)),
            _const_spec((d, nv)),
            _const_spec((1, nv)),
            wout_spec,
        ]
        args += [x2d, mod_l, g, w_o, gain, w_out]
        scratch += [pltpu.VMEM((ts, nv), F32), pltpu.VMEM((ts, nv), BF16)]
    out, cf, mf = pl.pallas_call(
        functools.partial(_scan_kernel, rev=rev, has_add=has_add,
                          readout=readout is not None),
        grid=(batch, n_t),
        in_specs=in_specs,
        out_specs=[pl.BlockSpec((ts, out_cols), lambda b, t: (blk(b, t), 0))]
        + st_specs,
        out_shape=[jax.ShapeDtypeStruct((n, out_cols), F32)]
        + [jax.ShapeDtypeStruct((batch,) + s, F32) for s in st_shapes],
        scratch_shapes=scratch,
        compiler_params=_params(2),
        name="mlstm_scan_bwd" if rev else "mlstm_scan_fwd",
    )(*args)
    return out, (cf, mf)


N_PROJ_IN = 9
N_PROJ_OUT = 4


def _proj_scan_kernel(*refs, n_casts):
    proj_in = refs[:N_PROJ_IN]
    c0_ref, m0_ref = refs[N_PROJ_IN:N_PROJ_IN + 2]
    cast_in = refs[N_PROJ_IN + 2:N_PROJ_IN + 2 + n_casts]
    outs = refs[N_PROJ_IN + 2 + n_casts:]
    proj_out = outs[:N_PROJ_OUT]
    n_scan_out = 3
    cast_out = outs[N_PROJ_OUT + n_scan_out:N_PROJ_OUT + n_scan_out + n_casts]
    scan_refs = (outs[N_PROJ_OUT:N_PROJ_OUT + n_scan_out]
                 + outs[N_PROJ_OUT + n_scan_out + n_casts:])
    for src, dst in zip(cast_in, cast_out):
        dst[...] = src[...].astype(BF16)
    _mlstm_proj_kernel(*proj_in, *proj_out)
    _scan_kernel(*proj_out, c0_ref, m0_ref, *scan_refs, rev=True,
                 has_add=False, readout=False)


def _proj_scan(x2d, mod_l, mod_row_fn, g, wts, state, *, batch, ts, casts=()):
    n, d = x2d.shape
    n_t = n // batch // ts
    n_chunks = ts // SCAN_CHUNK
    nq = N_HEADS * DK
    nv = N_HEADS * DV
    nqv = nq + nv

    def blk(b, t):
        return b * n_t + (n_t - 1 - t)

    rows = lambda w: pl.BlockSpec((ts, w), lambda b, t: (blk(b, t), 0))
    cols = lambda h: pl.BlockSpec((h, ts), lambda b, t: (0, blk(b, t)))
    st_shapes = [(N_HEADS, DK, DV + DK), (N_HEADS, 1, DK)]
    st_specs = [pl.BlockSpec((None,) + s, lambda b, t: (b, 0, 0, 0))
                for s in st_shapes]
    steps = batch * n_t
    cast_in_specs, cast_args, cast_out_specs, cast_shapes = [], [], [], []
    for w, layer in casts:
        slab = (w.shape[1] // steps, w.shape[2])
        cast_in_specs.append(pl.BlockSpec(
            (None,) + slab,
            lambda b, t, first=layer * steps: (first + b * n_t + t, 0, 0)))
        cast_args.append(w.reshape((w.shape[0] * steps,) + slab))
        cast_out_specs.append(pl.BlockSpec((None,) + slab,
                                           lambda b, t: (b * n_t + t, 0, 0)))
        cast_shapes.append(jax.ShapeDtypeStruct((steps,) + slab, BF16))
    outs = pl.pallas_call(
        functools.partial(_proj_scan_kernel, n_casts=len(casts)),
        grid=(batch, n_t),
        in_specs=[
            rows(d),
            pl.BlockSpec((None, 6, d), lambda b, t: (mod_row_fn(b), 0, 0)),
            _const_spec((1, d)),
            _const_spec((d, nqv)),
            _const_spec((nq, d)),
            _const_spec((d, GATE_PAD)),
            _const_spec((N_GATE, d)),
            _const_spec((1, GATE_PAD)),
            _const_spec((N_GATE, 1)),
        ] + st_specs + cast_in_specs,
        out_specs=[rows(nqv), cols(nq), rows(GATE_PAD), cols(N_GATE), rows(nv)]
        + st_specs + cast_out_specs,
        out_shape=[
            jax.ShapeDtypeStruct((n, nqv), BF16),
            jax.ShapeDtypeStruct((nq, n), BF16),
            jax.ShapeDtypeStruct((n, GATE_PAD), F32),
            jax.ShapeDtypeStruct((N_GATE, n), F32),
            jax.ShapeDtypeStruct((n, nv), F32),
        ] + [jax.ShapeDtypeStruct((batch,) + s, F32) for s in st_shapes]
        + cast_shapes,
        scratch_shapes=_scan_scratch(n_chunks),
        compiler_params=_params(2),
        name="mlstm_proj_scan_bwd",
    )(x2d, mod_l, g, wts["w_qv"], wts["w_kt"], wts["w_g"], wts["w_gt"],
      wts["b_g"], wts["b_gt"], *state, *cast_args)
    qv, kt, gc, gr, h, cf, mf = outs[:7]
    if not casts:
        return (qv, kt, gc, gr), h, (cf, mf)
    cast = [o.reshape(w.shape[1:]) for o, (w, _) in zip(outs[7:], casts)]
    return (qv, kt, gc, gr), h, (cf, mf), cast


def _mlstm_weights(w_in, b_gate):
    nq = N_HEADS * DK
    o2 = 2 * nq
    o3 = o2 + N_HEADS * DV
    o4 = o3 + D_MODEL
    w_g = w_in[:, o4:]
    return {
        "w_qv": jnp.concatenate([w_in[:, :nq], w_in[:, o2:o3]], axis=1).astype(BF16),
        "w_kt": w_in[:, nq:o2].T.astype(BF16),
        "w_o": w_in[:, o3:o4].astype(BF16),
        "w_g": jnp.pad(w_g, ((0, 0), (0, GATE_PAD - N_GATE))).astype(BF16),
        "w_gt": w_g.T.astype(BF16),
        "b_g": jnp.pad(b_gate, (0, GATE_PAD - N_GATE)).reshape(1, GATE_PAD),
        "b_gt": b_gate.reshape(N_GATE, 1),
    }


def kernel(x, c, ctx, c_ctx, w_mod, b_mod, g_mix, g_ffn, a_w_in, a_b_gate,
           a_head_gain, a_w_out, b_w_in, b_w_conv, b_w_out, f_w_up, f_w_conv,
           f_b_conv, f_w_down, g_final):
    bn, t, d = x.shape
    n_ctx = ctx.shape[1]
    depth = w_mod.shape[0]
    n_mixers = 2
    assert d == D_MODEL and bn <= 7 and t % TOKEN_TILE == 0
    assert n_ctx % SCAN_CHUNK == 0 and TOKEN_TILE % SCAN_CHUNK == 0

    rec_layers = [i for i in range(depth) if i % n_mixers == 0]
    last_rec = max(rec_layers) if rec_layers else -1

    cc = jnp.zeros((8, d), F32).at[:bn].set(c).at[bn].set(c_ctx)
    mod = _mod_all(cc, w_mod, b_mod).reshape(depth, 8, 6, d)

    tm = TOKEN_TILE
    tiles_per_batch = t // tm
    lat_row = lambda i: i // tiles_per_batch
    ctx_row = lambda i: bn

    xs = x.reshape(bn * t, d)
    cs = ctx.reshape(bn * n_ctx, d)
    zero_state = (jnp.zeros((bn, N_HEADS, DK, DV + DK), F32),
                  jnp.zeros((bn, N_HEADS, 1, DK), F32))

    ffn_w = conv_w = mlstm_w = None

    for i in range(depth):
        j = i // n_mixers
        ctx_read = i <= last_rec
        ctx_live = i < last_rec
        gm = g_mix[i].reshape(1, d)
        gf = g_ffn[i].reshape(1, d)
        if i % n_mixers == 0:
            if mlstm_w is None:
                mlstm_w = (a_w_in[j], a_w_out[j].astype(BF16))
            wts = _mlstm_weights(mlstm_w[0], a_b_gate[j])
            w_out = mlstm_w[1]
            mlstm_w = None
            gain = a_head_gain[j].reshape(1, N_HEADS * DV)
            s_f = s_b = zero_state
            if ctx_read:
                pc, hc, s_b = _proj_scan(cs, mod[i], lambda b: bn, gm, wts,
                                         zero_state, batch=bn, ts=n_ctx)
                ro = ((cs, mod[i], lambda b: bn, gm, wts["w_o"], gain, w_out)
                      if ctx_live else None)
                cs_new, s_f = _scan(*pc, zero_state, batch=bn, ts=n_ctx,
                                    rev=False, add=hc, readout=ro)
            if i == 0:
                px, hl, _, ffn_w = _proj_scan(
                    xs, mod[i], lambda b: b, gm, wts, s_b, batch=bn, ts=tm,
                    casts=[(f_w_up, 0), (f_w_down, 0)])
            else:
                px, hl, _ = _proj_scan(xs, mod[i], lambda b: b, gm, wts, s_b,
                                       batch=bn, ts=tm)
            ro = (xs, mod[i], lambda b: b, gm, wts["w_o"], gain, w_out)
            xs, _ = _scan(*px, s_f, batch=bn, ts=tm, rev=False, add=hl,
                          readout=ro)
            if ctx_live:
                cs = cs_new
        else:
            if conv_w is None:
                conv_w = (b_w_in[j].astype(BF16), b_w_out[j].astype(BF16))
            w_in, w_out = conv_w
            conv_w = None
            xs = _shortconv(xs, mod[i], gm, w_in, b_w_conv[j], w_out,
                            tm=tm, seg=GRID_W, row_fn=lat_row)
            if ctx_live:
                cs = _shortconv(cs, mod[i], gm, w_in, b_w_conv[j], w_out,
                                tm=n_ctx, seg=n_ctx, row_fn=ctx_row)
        if ffn_w is None:
            ffn_w = (f_w_up[i].astype(BF16), f_w_down[i].astype(BF16))
        w_up, w_down = ffn_w
        ffn_w = None
        b_conv = f_b_conv[i].reshape(1, D_FF)
        gfin = g_final.reshape(1, d)
        if i + 1 < depth:
            casts = [(f_w_up, i + 1), (f_w_down, i + 1)]
            nxt = (i + 1) // n_mixers
            if (i + 1) % n_mixers == 1:
                casts += [(b_w_in, nxt), (b_w_out, nxt)]
            else:
                casts += [(a_w_in, nxt), (a_w_out, nxt)]
            xs, cast = _ffn(xs, mod[i], gf, w_up, f_w_conv[i], b_conv, w_down,
                            gfin, batch=bn, strip=FFN_STRIP, final=False,
                            casts=casts)
            ffn_w = cast[:2]
            if (i + 1) % n_mixers == 1:
                conv_w = cast[2:]
            else:
                mlstm_w = cast[2:]
        else:
            xs = _ffn(xs, mod[i], gf, w_up, f_w_conv[i], b_conv, w_down, gfin,
                      batch=bn, strip=FFN_STRIP, final=True)
        if ctx_live:
            cs = _ffn(cs, mod[i], gf, w_up, f_w_conv[i], b_conv, w_down, gfin,
                      batch=bn, strip=0, final=False)

    return xs.reshape(bn, t, d)
```
